```python
import functools
import jax
import jax.numpy as jnp
from jax import lax
import numpy as np

D_MODEL = 1024
BATCH = 2
SEQ = 8192
DEPTH = 1
DEC_BATCH = 32
DEC_SEQ = 4
PAST_LEN = 16384
PAGE_SIZE = 128

N_HEADS = 8
N_KV_HEADS = 2
HEAD_DIM = 64
GROUP = N_HEADS // N_KV_HEADS
IDX_HEADS = 4
IDX_DIM = 64
TOPK_MAX = 256
Q_BLOCK = 128
ROPE_THETA = 10000.0
M_HEADS = 4
M_QK = 64
M_V = 128
M_CHUNK = 128
FORGET_BIAS = 3.0
N_EXPERTS = 32
TOP_K = 4
EXPERT_FF = D_MODEL
SWIGLU_LIMIT = 7.0
SWIGLU_ALPHA = 1.702
MOE_BLOCK = 128
PLE_DIM = 256
EPS = 1e-6

A_WIDTH = N_HEADS * HEAD_DIM
M_WIDTH = M_HEADS * M_V
IN_SIZES = (A_WIDTH, N_KV_HEADS * HEAD_DIM, N_KV_HEADS * HEAD_DIM, IDX_HEADS * IDX_DIM, IDX_DIM, IDX_HEADS,
            M_HEADS * M_QK, M_HEADS * M_QK, M_WIDTH, M_HEADS, M_HEADS, M_WIDTH, D_MODEL, D_MODEL)
N_IN = sum(IN_SIZES)

kernel_name = 'dsa_mlstm_gated_hybrid_moe_step'

f32 = jnp.float32


def rmsnorm(x, g):
    xf = x.astype(f32)
    y = xf * lax.rsqrt(jnp.mean(xf * xf, axis=-1, keepdims=True) + EPS)
    return (y * g.astype(f32)).astype(x.dtype)


def rope(x, pos):
    d = x.shape[-1]
    half = d // 2
    inv = ROPE_THETA ** (-jnp.arange(half, dtype=f32) / half)
    ang = pos.astype(f32)[:, None] * inv[None, :]
    shape = (1, pos.shape[0]) + (1,) * (x.ndim - 3) + (half,)
    cos = jnp.cos(ang).reshape(shape)
    sin = jnp.sin(ang).reshape(shape)
    xf = x.astype(f32)
    x1, x2 = xf[..., :half], xf[..., half:]
    return jnp.concatenate([x1 * cos - x2 * sin, x2 * cos + x1 * sin], axis=-1).astype(x.dtype)


def take_rows(a, idx):
    return jax.vmap(lambda ab, ib: ab[ib])(a, idx)


def project(h, w_in, b_gate, pos):
    B, T = h.shape[:2]
    z = h @ w_in
    offs = []
    acc = 0
    for s in IN_SIZES[:-1]:
        acc += s
        offs.append(acc)
    (aq, ak, av, iq, ik, iw, mq, mk, mv, mi, mf, mo, ga, gb) = jnp.split(z, offs, axis=-1)
    q = rope(aq.reshape(B, T, N_HEADS, HEAD_DIM), pos)
    k = rope(ak.reshape(B, T, N_KV_HEADS, HEAD_DIM), pos)
    v = av.reshape(B, T, N_KV_HEADS, HEAD_DIM)
    qi = rope(iq.reshape(B, T, IDX_HEADS, IDX_DIM), pos)
    ki = rope(ik, pos)
    wi = iw * (IDX_HEADS ** -0.5 * IDX_DIM ** -0.5)
    mq = mq.reshape(B, T, M_HEADS, M_QK)
    mk = mk.reshape(B, T, M_HEADS, M_QK) * (M_QK ** -0.5)
    mv = mv.reshape(B, T, M_HEADS, M_V)
    bg = b_gate.astype(f32)
    ig = mi.astype(f32) + bg[:M_HEADS]
    lf = jax.nn.log_sigmoid(mf.astype(f32) + bg[M_HEADS:])
    return (q, k, v, qi, wi, ki, mq, mk, mv, ig, lf,
            jax.nn.sigmoid(mo), jax.nn.sigmoid(ga), jax.nn.sigmoid(gb))


def index_select(qi, wi, ki, qpos, n_sel):
    s = jnp.einsum('bthd,bsd->bths', qi.astype(f32), ki.astype(f32))
    score = jnp.einsum('bth,bths->bts', wi.astype(f32), jax.nn.relu(s))
    adm = jnp.arange(ki.shape[1], dtype=jnp.int32)[None, :] <= qpos[:, None]
    score = jnp.where(adm[None], score, -jnp.inf)
    _, idx = lax.top_k(score, n_sel)
    valid = idx <= qpos[None, :, None]
    return idx, valid


def attend(q, kg, vg, valid):
    B, T = q.shape[:2]
    qg = q.astype(f32).reshape(B, T, N_KV_HEADS, GROUP, HEAD_DIM)
    logits = jnp.einsum('btkgd,btskd->btkgs', qg, kg.astype(f32)) * (HEAD_DIM ** -0.5)
    logits = jnp.where(valid[:, :, None, None, :], logits, -jnp.inf)
    probs = jax.nn.softmax(logits, axis=-1)
    out = jnp.einsum('btkgs,btskd->btkgd', probs, vg.astype(f32))
    return out.reshape(B, T, A_WIDTH).astype(q.dtype)


def dsa_prompt(q, k, v, qi, wi, ki):
    B, T = q.shape[:2]
    qb = min(Q_BLOCK, T)
    nb = T // qb
    n_sel = min(TOPK_MAX, T // 4)

    def to_blocks(a):
        return jnp.moveaxis(a.reshape((B, nb, qb) + a.shape[2:]), 1, 0)

    def block(args):
        q_b, qi_b, wi_b, t0 = args
        qpos = t0 + jnp.arange(qb, dtype=jnp.int32)
        idx, valid = index_select(qi_b, wi_b, ki, qpos, n_sel)
        return attend(q_b, take_rows(k, idx), take_rows(v, idx), valid)

    out = lax.map(block, (to_blocks(q), to_blocks(qi), to_blocks(wi),
                          jnp.arange(nb, dtype=jnp.int32) * qb))
    return jnp.moveaxis(out, 0, 1).reshape(B, T, A_WIDTH)


def dsa_sample(q, k, v, qi, wi, ki, ck, cv, cki, page_table):
    B, T = q.shape[:2]
    n_pages = page_table.shape[1]
    past = n_pages * PAGE_SIZE
    ki_past = cki[page_table].reshape(B, past, IDX_DIM)
    ki_all = jnp.concatenate([ki_past.astype(ki.dtype), ki], axis=1)
    n_sel = min(TOPK_MAX, (past + T) // 4)
    qpos = past + jnp.arange(T, dtype=jnp.int32)
    idx, valid = index_select(qi, wi, ki_all, qpos, n_sel)
    in_past = (idx < past)[..., None, None]
    pidx = jnp.minimum(idx, past - 1)
    phys = take_rows(page_table, pidx // PAGE_SIZE)
    off = pidx % PAGE_SIZE
    nidx = jnp.clip(idx - past, 0, T - 1)
    kg = jnp.where(in_past, ck[phys, off].astype(k.dtype), take_rows(k, nidx))
    vg = jnp.where(in_past, cv[phys, off].astype(v.dtype), take_rows(v, nidx))
    return attend(q, kg, vg, valid)


def mlstm(q, k, v, ig, lf, C0, n0, m0):
    B, T, NH, _ = q.shape
    L = M_CHUNK if T % M_CHUNK == 0 else T
    nc = T // L

    def chunks(a):
        a = a.astype(f32).reshape((B, nc, L) + a.shape[2:])
        return jnp.moveaxis(jnp.moveaxis(a, 1, 0), 3, 2)

    tril = jnp.tril(jnp.ones((L, L), dtype=bool))

    def step(carry, xs):
        C, n, m = carry
        qc, kc, vc, ic, fc = xs
        b = jnp.cumsum(fc, axis=-1)
        dmat = b[..., :, None] - b[..., None, :] + ic[..., None, :]
        dmat = jnp.where(tril, dmat, -jnp.inf)
        inter = b + m[..., None]
        m_t = jnp.maximum(inter, jnp.max(dmat, axis=-1))
        s = jnp.einsum('bhtd,bhsd->bhts', qc, kc) * jnp.exp(dmat - m_t[..., None])
        w_inter = jnp.exp(inter - m_t)
        num = jnp.einsum('bhts,bhsv->bhtv', s, vc) + w_inter[..., None] * jnp.einsum('bhvd,bhtd->bhtv', C, qc)
        den = jnp.sum(s, axis=-1) + w_inter * jnp.einsum('bhd,bhtd->bht', n, qc)
        h = num / jnp.maximum(jnp.abs(den), jnp.exp(-m_t))[..., None]
        b_last = b[..., -1]
        g = b_last[..., None] - b + ic
        m_new = jnp.maximum(b_last + m, jnp.max(g, axis=-1))
        w_k = jnp.exp(g - m_new[..., None])
        decay = jnp.exp(b_last + m - m_new)
        C_new = decay[..., None, None] * C + jnp.einsum('bhs,bhsv,bhsd->bhvd', w_k, vc, kc)
        n_new = decay[..., None] * n + jnp.einsum('bhs,bhsd->bhd', w_k, kc)
        return (C_new, n_new, m_new), h

    (C, n, m), hs = lax.scan(step, (C0.astype(f32), n0.astype(f32), m0.astype(f32)),
                             (chunks(q), chunks(k), chunks(v), chunks(ig), chunks(lf)))
    h = jnp.moveaxis(jnp.moveaxis(hs, 0, 1), 2, 3).reshape(B, T, NH, M_V)
    return h, C, n, m


def moe(h, w_router, b_router, w_gu, b_gu, w_dn, b_dn):
    B, T, D = h.shape
    n_tok = B * T
    x2 = h.reshape(n_tok, D)
    logits = x2.astype(f32) @ w_router.astype(f32) + b_router.astype(f32)
    top_v, top_e = lax.top_k(logits, TOP_K)
    gates = jax.nn.softmax(top_v, axis=-1)
    n_asg = n_tok * TOP_K
    flat_e = top_e.reshape(n_asg)
    order = jnp.argsort(flat_e)
    e_sorted = flat_e[order]
    tok_sorted = (order // TOP_K).astype(jnp.int32)
    w_sorted = gates.reshape(n_asg)[order]
    counts = jnp.bincount(flat_e, length=N_EXPERTS)
    padded = (counts + MOE_BLOCK - 1) // MOE_BLOCK * MOE_BLOCK
    pad_end = jnp.cumsum(padded)
    grp_start = jnp.cumsum(counts) - counts
    dest = (pad_end - padded)[e_sorted] + jnp.arange(n_asg, dtype=jnp.int32) - grp_start[e_sorted]
    n_blocks = (n_asg + MOE_BLOCK - 1) // MOE_BLOCK + N_EXPERTS
    n_rows = n_blocks * MOE_BLOCK
    row_tok = jnp.zeros((n_rows,), jnp.int32).at[dest].set(tok_sorted)
    blk_e = jnp.minimum(jnp.searchsorted(pad_end, jnp.arange(n_blocks, dtype=jnp.int32) * MOE_BLOCK, side='right'),
                        N_EXPERTS - 1).astype(jnp.int32)
    xs = x2[row_tok].reshape(n_blocks, MOE_BLOCK, D)

    def expert_block(args):
        xb, e = args
        gu = xb @ w_gu[e] + b_gu[e]
        gate = jnp.minimum(gu[:, :EXPERT_FF], SWIGLU_LIMIT)
        up = jnp.clip(gu[:, EXPERT_FF:], -SWIGLU_LIMIT, SWIGLU_LIMIT)
        act = (up + 1.0) * (gate * jax.nn.sigmoid(SWIGLU_ALPHA * gate))
        return act @ w_dn[e] + b_dn[e]

    ys = lax.map(expert_block, (xs, blk_e)).reshape(n_rows, D)
    out = jnp.zeros((n_tok, D), f32).at[tok_sorted].add(ys[dest].astype(f32) * w_sorted[:, None])
    return out.reshape(B, T, D).astype(h.dtype)


def layer(x, p, pos, attn_fn, C0, n0, m0, g_mix, w_in, b_gate, g_mnorm, w_up_a, w_up_b, w_out,
          g_ffn, w_router, b_router, w_gu, b_gu, w_dn, b_dn, g_ple, w_ple_gate, w_ple_proj):
    B, T = x.shape[:2]
    h = rmsnorm(x, g_mix)
    (q, k, v, qi, wi, ki, mq, mk, mv, ig, lf, og, ga, gb) = project(h, w_in, b_gate, pos)
    o_a = attn_fn(q, k, v, qi, wi, ki)
    hm, C, n, m = mlstm(mq, mk, mv, ig, lf, C0, n0, m0)
    hm = rmsnorm(hm.astype(x.dtype), g_mnorm.reshape(M_HEADS, M_V)).reshape(B, T, M_WIDTH)
    o_b = og * hm
    merged = ga * (o_a @ w_up_a) + gb * (o_b @ w_up_b)
    x = x + merged @ w_out
    x = x + moe(rmsnorm(x, g_ffn), w_router, b_router, w_gu, b_gu, w_dn, b_dn)
    x = x + jax.nn.sigmoid(rmsnorm(x, g_ple) @ w_ple_gate) * (p @ w_ple_proj)
    return x, (k, v, ki, C, n, m)


def setup_inputs(seed: int = 0) -> dict:
    key = jax.random.key(seed)
    ks = jax.random.split(key, 32)

    def nrm(i, shape, scale):
        return scale * jax.random.normal(ks[i], shape, jnp.float32)

    n_pages = PAST_LEN // PAGE_SIZE
    n_pool = (DEC_BATCH * n_pages * 5) // 4
    page_table = jax.random.permutation(ks[8], n_pool)[:DEC_BATCH * n_pages].reshape(DEC_BATCH, n_pages).astype(jnp.int32)
    forget_offset = jnp.where(jnp.arange(2 * M_HEADS) >= M_HEADS, FORGET_BIAS, 0.0).astype(jnp.float32)
    return {
        'x_prompt': nrm(0, (BATCH, SEQ, D_MODEL), 1.0),
        'x_sample': nrm(1, (DEC_BATCH, DEC_SEQ, D_MODEL), 1.0),
        'cache_k': nrm(2, (DEPTH, n_pool, PAGE_SIZE, N_KV_HEADS, HEAD_DIM), 1.0),
        'cache_v': nrm(3, (DEPTH, n_pool, PAGE_SIZE, N_KV_HEADS, HEAD_DIM), 1.0),
        'cache_idx_k': nrm(4, (DEPTH, n_pool, PAGE_SIZE, IDX_DIM), 1.0),
        'state_C': nrm(5, (DEPTH, DEC_BATCH, M_HEADS, M_V, M_QK), 0.3),
        'state_n': nrm(6, (DEPTH, DEC_BATCH, M_HEADS, M_QK), 0.3),
        'state_m': nrm(7, (DEPTH, DEC_BATCH, M_HEADS), 0.5),
        'page_table': page_table,
        'p_prompt': nrm(9, (DEPTH, BATCH, SEQ, PLE_DIM), 1.0),
        'p_sample': nrm(10, (DEPTH, DEC_BATCH, DEC_SEQ, PLE_DIM), 1.0),
        'g_mix': 1.0 + nrm(11, (DEPTH, D_MODEL), 0.02),
        'w_in': nrm(12, (DEPTH, D_MODEL, N_IN), D_MODEL ** -0.5),
        'b_gate': nrm(13, (DEPTH, 2 * M_HEADS), 0.1) + forget_offset,
        'g_mnorm': 1.0 + nrm(14, (DEPTH, M_WIDTH), 0.02),
        'w_up_a': nrm(15, (DEPTH, A_WIDTH, D_MODEL), A_WIDTH ** -0.5),
        'w_up_b': nrm(16, (DEPTH, M_WIDTH, D_MODEL), M_WIDTH ** -0.5),
        'w_out': nrm(17, (DEPTH, D_MODEL, D_MODEL), D_MODEL ** -0.5),
        'g_ffn': 1.0 + nrm(18, (DEPTH, D_MODEL), 0.02),
        'w_router': nrm(19, (DEPTH, D_MODEL, N_EXPERTS), D_MODEL ** -0.5),
        'b_router': nrm(20, (DEPTH, N_EXPERTS), 0.01),
        'w_gu': nrm(21, (DEPTH, N_EXPERTS, D_MODEL, 2 * EXPERT_FF), D_MODEL ** -0.5),
        'b_gu': nrm(22, (DEPTH, N_EXPERTS, 2 * EXPERT_FF), 0.02),
        'w_dn': nrm(23, (DEPTH, N_EXPERTS, EXPERT_FF, D_MODEL), EXPERT_FF ** -0.5),
        'b_dn': nrm(24, (DEPTH, N_EXPERTS, D_MODEL), 0.02),
        'g_ple': 1.0 + nrm(25, (DEPTH, D_MODEL), 0.02),
        'w_ple_gate': nrm(26, (DEPTH, D_MODEL, D_MODEL), D_MODEL ** -0.5),
        'w_ple_proj': nrm(27, (DEPTH, PLE_DIM, D_MODEL), PLE_DIM ** -0.5),
        'g_final': 1.0 + nrm(28, (D_MODEL,), 0.02),
    }


def reference(x_prompt, x_sample, cache_k, cache_v, cache_idx_k, state_C, state_n, state_m, page_table,
              p_prompt, p_sample, g_mix, w_in, b_gate, g_mnorm, w_up_a, w_up_b, w_out, g_ffn,
              w_router, b_router, w_gu, b_gu, w_dn, b_dn, g_ple, w_ple_gate, w_ple_proj, g_final):
    Bp, Tp = x_prompt.shape[:2]
    Ts = x_sample.shape[1]
    past = page_table.shape[1] * PAGE_SIZE
    pos_p = jnp.arange(Tp, dtype=jnp.int32)
    pos_s = past + jnp.arange(Ts, dtype=jnp.int32)
    C0 = jnp.zeros((Bp, M_HEADS, M_V, M_QK), f32)
    n0 = jnp.zeros((Bp, M_HEADS, M_QK), f32)
    m0 = jnp.zeros((Bp, M_HEADS), f32)
    xp, xs = x_prompt, x_sample
    st_p, st_s = [], []
    for i in range(DEPTH):
        lw = (g_mix[i], w_in[i], b_gate[i], g_mnorm[i], w_up_a[i], w_up_b[i], w_out[i], g_ffn[i],
              w_router[i], b_router[i], w_gu[i], b_gu[i], w_dn[i], b_dn[i], g_ple[i], w_ple_gate[i], w_ple_proj[i])
        xp, sp = layer(xp, p_prompt[i], pos_p, dsa_prompt, C0, n0, m0, *lw)
        attn_s = functools.partial(dsa_sample, ck=cache_k[i], cv=cache_v[i], cki=cache_idx_k[i], page_table=page_table)
        xs, ss = layer(xs, p_sample[i], pos_s, attn_s, state_C[i], state_n[i], state_m[i], *lw)
        st_p.append(sp)
        st_s.append(ss)
    y_prompt = rmsnorm(xp, g_final)
    y_sample = rmsnorm(xs, g_final)
    k_prompt = jnp.stack([s[0] for s in st_p])
    v_prompt = jnp.stack([s[1] for s in st_p])
    idxk_prompt = jnp.stack([s[2] for s in st_p])
    C_prompt = jnp.stack([s[3] for s in st_p])
    n_prompt = jnp.stack([s[4] for s in st_p])
    m_prompt = jnp.stack([s[5] for s in st_p])
    k_sample = jnp.stack([s[0] for s in st_s])
    v_sample = jnp.stack([s[1] for s in st_s])
    idxk_sample = jnp.stack([s[2] for s in st_s])
    C_sample = jnp.stack([s[3] for s in st_s])
    n_sample = jnp.stack([s[4] for s in st_s])
    m_sample = jnp.stack([s[5] for s in st_s])
    return (y_prompt, y_sample, k_prompt, v_prompt, idxk_prompt, C_prompt, n_prompt, m_prompt,
            k_sample, v_sample, idxk_sample, C_sample, n_sample, m_sample)
```

```python
import functools

import jax
import jax.numpy as jnp
from jax import lax
from jax.experimental import pallas as pl
from jax.experimental.pallas import tpu as pltpu

f32 = jnp.float32
bf16 = jnp.bfloat16
i32 = jnp.int32

D_MODEL = 1024
PAGE_SIZE = 128
N_HEADS = 8
N_KV_HEADS = 2
HEAD_DIM = 64
GROUP = N_HEADS // N_KV_HEADS
IDX_HEADS = 4
IDX_DIM = 64
TOPK_MAX = 256
ROPE_THETA = 10000.0
M_HEADS = 4
M_QK = 64
M_V = 128
N_EXPERTS = 32
TOP_K = 4
EXPERT_FF = D_MODEL
SWIGLU_LIMIT = 7.0
SWIGLU_ALPHA = 1.702
PLE_DIM = 256
EPS = 1e-6
A_WIDTH = N_HEADS * HEAD_DIM
M_WIDTH = M_HEADS * M_V

LANES = 128
SUBLANES = 8
VMEM_LIMIT = 56 * 1024 * 1024

M_CHUNK = 128
TQ = 128
KC = 512
PAGES_PER_STEP = 8
FFN_BLK = 256
OUT_TM = 128

INT_MIN = -2147483648
INT_MAX = 2147483647
NEG_BIG = -1e30

SM_KI = 0
SM_WI = IDX_DIM
SM_IG = SM_WI + IDX_HEADS
SM_LF = SM_IG + M_HEADS


def _cparams(sem):
    return pltpu.CompilerParams(dimension_semantics=sem, vmem_limit_bytes=VMEM_LIMIT)


def _rms(x, g):
    return x * lax.rsqrt(jnp.mean(x * x, axis=-1, keepdims=True) + EPS) * g


def _sigmoid(x):
    return 1.0 / (1.0 + jnp.exp(-x))


def _log_sigmoid(x):
    return jnp.minimum(x, 0.0) - jnp.log1p(jnp.exp(-jnp.abs(x)))


def _sortable(x):
    bits = lax.bitcast_convert_type(x, i32)
    key = bits ^ ((bits >> 31) & INT_MAX)
    return jnp.where(x == 0.0, 0, key)


_G_Q = (0, 512)
_G_K = (512, 640)
_G_V = (640, 768)
_G_QI = (768, 1024)
_G_SM = (1024, 1152)
_G_MQ = (1152, 1408)
_G_MK = (1408, 1664)
_G_MV = (1664, 2176)
_G_MO = (2176, 2688)
_G_GA = (2688, 3712)
_G_GB = (3712, 4736)
_W_COLS = 4736


def _pack_w_in(w_in):
    o = [0]
    for s in (A_WIDTH, 128, 128, 256, 64, 4, 256, 256, 512, 4, 4, 512, 1024, 1024):
        o.append(o[-1] + s)
    aq, ak, av, iq, ik, iw, mq, mk, mv, mi, mf, mo, ga, gb = [w_in[:, o[i]:o[i + 1]] for i in range(14)]
    pad = jnp.zeros((w_in.shape[0], LANES - (IDX_DIM + IDX_HEADS + 2 * M_HEADS)), w_in.dtype)
    small = jnp.concatenate([ik, iw, mi, mf, pad], axis=1)
    w = jnp.concatenate([aq, ak, av, iq, small, mq, mk, mv, mo, ga, gb], axis=1)
    return w.astype(bf16)


def _in_kernel(x_ref, cos_ref, sin_ref, g_ref, w_ref, bias_ref,
               q_ref, k_ref, v_ref, qi_ref, sm_ref, mq_ref, mk_ref, mv_ref, og_ref, ga_ref, gb_ref):
    x = x_ref[...]
    hb = _rms(x, g_ref[...]).astype(bf16)
    cos = cos_ref[...]
    sin = sin_ref[...]
    tm = x.shape[0]
    lane = lax.broadcasted_iota(i32, (tm, LANES), 1)
    first_half = (lane % HEAD_DIM) < (HEAD_DIM // 2)

    def rope(z):
        rot = jnp.where(first_half, pltpu.roll(z, LANES - HEAD_DIM // 2, 1), pltpu.roll(z, HEAD_DIM // 2, 1))
        return z * cos + rot * sin

    def proj(grp):
        return jnp.dot(hb, w_ref[:, grp[0]:grp[1]], preferred_element_type=f32)

    z = proj(_G_Q)
    for j in range(4):
        q_ref[:, j * LANES:(j + 1) * LANES] = rope(z[:, j * LANES:(j + 1) * LANES])
    k_ref[...] = rope(proj(_G_K))
    v_ref[...] = proj(_G_V)
    z = proj(_G_QI)
    for j in range(2):
        qi_ref[:, j * LANES:(j + 1) * LANES] = rope(z[:, j * LANES:(j + 1) * LANES])
    z = proj(_G_SM)
    zb = z + bias_ref[...]
    sm = jnp.where(lane < SM_WI, rope(z),
                   jnp.where(lane < SM_IG, z * (IDX_HEADS ** -0.5 * IDX_DIM ** -0.5),
                             jnp.where(lane < SM_LF, zb,
                                       jnp.where(lane < SM_LF + M_HEADS, _log_sigmoid(zb), 0.0))))
    sm_ref[...] = sm
    mq_ref[...] = proj(_G_MQ)
    mk_ref[...] = proj(_G_MK) * (M_QK ** -0.5)
    mv_ref[...] = proj(_G_MV)
    og_ref[...] = _sigmoid(proj(_G_MO))
    ga_ref[...] = _sigmoid(proj(_G_GA))
    gb_ref[...] = _sigmoid(proj(_G_GB))


def _rope_tables(pos):
    half = HEAD_DIM // 2
    inv = ROPE_THETA ** (-jnp.arange(half, dtype=f32) / half)
    ang = pos.astype(f32)[:, None] * inv[None, :]
    cos = jnp.cos(ang)
    sin = jnp.sin(ang)
    cos128 = jnp.tile(cos, (1, 4))
    sin128 = jnp.tile(jnp.concatenate([-sin, sin], axis=1), (1, 2))
    return cos128, sin128


def in_proj(x2d, cos128, sin128, g_mix, w_packed, bias128):
    n = x2d.shape[0]
    tm = min(256, n)
    assert n % tm == 0
    row = lambda w: pl.BlockSpec((tm, w), lambda i: (i, 0))
    full = lambda a: pl.BlockSpec(a.shape, lambda i: (0,) * a.ndim)
    widths = (512, 128, 128, 256, 128, 256, 256, 512, 512, 1024, 1024)
    return pl.pallas_call(
        _in_kernel,
        grid=(n // tm,),
        in_specs=[row(D_MODEL), row(LANES), row(LANES), full(g_mix), full(w_packed), full(bias128)],
        out_specs=[row(w) for w in widths],
        out_shape=[jax.ShapeDtypeStruct((n, w), f32) for w in widths],
        compiler_params=_cparams(("parallel",)),
        name="in_proj",
    )(x2d, cos128, sin128, g_mix, w_packed, bias128)


def _mlstm_kernel(mq_ref, mk_ref, mv_ref, og_ref, sm_ref, gr_ref, c0_ref, n0_ref, m0_ref, gn_ref,
                  ob_ref, c_ref, n_ref, m_ref):
    ci = pl.program_id(1)
    L = mq_ref.shape[1]

    @pl.when(ci == 0)
    def _():
        c_ref[...] = c0_ref[...]
        n_ref[...] = n0_ref[...]
        m_ref[...] = m0_ref[...]

    sm = sm_ref[0]
    gr = gr_ref[0]
    row = lax.broadcasted_iota(i32, (L, L), 0)
    col = lax.broadcasted_iota(i32, (L, L), 1)
    tril = row >= col
    for hd in range(M_HEADS):
        q = mq_ref[0, :, hd * M_QK:(hd + 1) * M_QK]
        k = mk_ref[0, :, hd * M_QK:(hd + 1) * M_QK]
        v = mv_ref[0, :, hd * M_V:(hd + 1) * M_V]
        ig_r = gr[hd:hd + 1, :]
        lf_r = gr[M_HEADS + hd:M_HEADS + hd + 1, :]
        ig_c = sm[:, SM_IG + hd:SM_IG + hd + 1]
        lf_c = sm[:, SM_LF + hd:SM_LF + hd + 1]
        C = c_ref[0, hd]
        nrow = n_ref[0, hd]
        m_prev = m_ref[0, hd]
        b_c = jnp.sum(jnp.where(tril, lf_r, 0.0), axis=1, keepdims=True)
        b_r = jnp.sum(jnp.where(tril, 0.0, lf_c) + jnp.where(row == col, lf_c, 0.0), axis=0, keepdims=True)
        dmat = jnp.where(tril, b_c - b_r + ig_r, -jnp.inf)
        inter = b_c + m_prev
        m_t = jnp.maximum(inter, jnp.max(dmat, axis=1, keepdims=True))
        qb = q.astype(bf16)
        kb = k.astype(bf16)
        qk = lax.dot_general(qb, kb, (((1,), (1,)), ((), ())), preferred_element_type=f32)
        s = qk * jnp.exp(dmat - m_t)
        w_inter = jnp.exp(inter - m_t)
        qc = lax.dot_general(qb, C.astype(bf16), (((1,), (1,)), ((), ())), preferred_element_type=f32)
        num = jnp.dot(s.astype(bf16), v.astype(bf16), preferred_element_type=f32) + w_inter * qc
        den = jnp.sum(s, axis=1, keepdims=True) + w_inter * jnp.sum(q * nrow, axis=1, keepdims=True)
        h = num / jnp.maximum(jnp.abs(den), jnp.exp(-m_t))
        b_last = b_c[L - 1:L, :]
        g_c = b_last - b_c + ig_c
        m_new = jnp.maximum(b_last + m_prev, jnp.max(g_c, axis=0, keepdims=True))
        w_k = jnp.exp(g_c - m_new)
        decay = jnp.exp(b_last + m_prev - m_new)
        wv = (w_k * v).astype(bf16)
        c_ref[0, hd] = decay * C + lax.dot_general(wv, kb, (((0,), (0,)), ((), ())), preferred_element_type=f32)
        n_ref[0, hd] = decay * nrow + jnp.sum(w_k * k, axis=0, keepdims=True)
        m_ref[0, hd] = m_new
        gn = gn_ref[:, hd * M_V:(hd + 1) * M_V]
        ob_ref[0, :, hd * M_V:(hd + 1) * M_V] = og_ref[0, :, hd * M_V:(hd + 1) * M_V] * _rms(h, gn)


def mlstm(mq, mk, mv, og, sm, grow, c0, n0, m0, g_mnorm):
    b, t = mq.shape[:2]
    L = M_CHUNK
    assert t % L == 0
    tok = lambda w: pl.BlockSpec((1, L, w), lambda bi, ci: (bi, ci, 0))
    st = lambda a: pl.BlockSpec((1,) + a.shape[1:], lambda bi, ci: (bi,) + (0,) * (a.ndim - 1))
    return pl.pallas_call(
        _mlstm_kernel,
        grid=(b, t // L),
        in_specs=[tok(256), tok(256), tok(512), tok(512), tok(LANES),
                  pl.BlockSpec((1, 2 * M_HEADS, L), lambda bi, ci: (bi, 0, ci)),
                  st(c0), st(n0), st(m0), pl.BlockSpec(g_mnorm.shape, lambda bi, ci: (0, 0))],
        out_specs=[tok(512), st(c0), st(n0), st(m0)],
        out_shape=[jax.ShapeDtypeStruct((b, t, M_WIDTH), f32), jax.ShapeDtypeStruct(c0.shape, f32),
                   jax.ShapeDtypeStruct(n0.shape, f32), jax.ShapeDtypeStruct(m0.shape, f32)],
        compiler_params=_cparams(("parallel", "arbitrary")),
        name="mlstm",
    )(mq, mk, mv, og, sm, grow, c0, n0, m0, g_mnorm)


def mlstm_branch(mq, mk, mv, og, sm, c0, n0, m0, g_mnorm):
    b, t = mq.shape[:2]
    tp = -(-t // M_CHUNK) * M_CHUNK
    if tp != t:
        pad = lambda a: jnp.pad(a, ((0, 0), (0, tp - t), (0, 0)))
        mq, mk, mv, og = pad(mq), pad(mk), pad(mv), pad(og)
        sm_pad = jnp.zeros((b, tp - t, LANES), f32).at[:, :, SM_IG:SM_IG + M_HEADS].set(NEG_BIG)
        sm = jnp.concatenate([sm, sm_pad], axis=1)
    grow = sm[:, :, SM_IG:SM_IG + 2 * M_HEADS].transpose(0, 2, 1)
    ob, c, n, m = mlstm(mq, mk, mv, og, sm, grow, c0.astype(f32), n0.astype(f32).reshape(b, M_HEADS, 1, M_QK),
                        m0.astype(f32).reshape(b, M_HEADS, 1, 1), g_mnorm.reshape(1, M_WIDTH))
    return ob[:, :t], c, n.reshape(b, M_HEADS, M_QK), m.reshape(b, M_HEADS)


def _select_threshold(get_chunk, n_chunks, chunk_w, rows, n_sel, idx_bits, dynamic):
    def count(pred):
        def body(c, acc):
            keys, base = get_chunk(c)
            idx = base + lax.broadcasted_iota(i32, (rows, chunk_w), 1)
            hit = jnp.where(pred(keys, idx), 1.0, 0.0)
            for j in range(chunk_w // LANES):
                acc = acc + hit[:, j * LANES:(j + 1) * LANES]
            return acc
        acc0 = jnp.zeros((rows, LANES), f32)
        if dynamic:
            acc = lax.fori_loop(0, n_chunks, body, acc0)
        else:
            acc = acc0
            for c in range(n_chunks):
                acc = body(c, acc)
        return jnp.sum(acc, axis=1, keepdims=True)

    kf = float(n_sel)

    def bit_step(it, thr_u):
        cand_u = thr_u | lax.shift_left(jnp.int32(1), 31 - it)
        cand_s = cand_u ^ INT_MIN
        cnt = count(lambda keys, idx: keys >= cand_s)
        return jnp.where(cnt >= kf, cand_u, thr_u)

    thr_u = lax.fori_loop(0, 32, bit_step, jnp.zeros((rows, 1), i32))
    thr = jnp.maximum(thr_u ^ INT_MIN, INT_MIN + 1)
    need = kf - count(lambda keys, idx: keys > thr)
    n_eq = count(lambda keys, idx: keys == thr)
    overflow = jnp.max(n_eq - need) > 0.0

    def tie_search():
        def step(it, jm):
            cand = jm | lax.shift_left(jnp.int32(1), idx_bits - 1 - it)
            below = count(lambda keys, idx: (keys == thr) & (idx < cand))
            return jnp.where(below < need, cand, jm)
        return lax.fori_loop(0, idx_bits, step, jnp.zeros((rows, 1), i32))

    jm = lax.cond(overflow, tie_search, lambda: jnp.full((rows, 1), INT_MAX, i32))
    return thr, jm


def _selected(keys, idx, thr, jm):
    return (keys > thr) | ((keys == thr) & (idx <= jm))


def _dsa_prompt_kernel(q_ref, qi_ref, sm_ref, kit_ref, kt_ref, vd_ref, o_ref,
                       keys_ref, m_ref, l_ref, acc_ref, *, n_sel, idx_bits):
    qb = pl.program_id(1)
    tq = q_ref.shape[1]
    kc = kit_ref.shape[3]
    n_chunks = (qb * tq + tq - 1) // kc + 1
    lane = lax.broadcasted_iota(i32, (tq, LANES), 1)
    lo = lane < HEAD_DIM
    t_col = qb * tq + lax.broadcasted_iota(i32, (tq, 1), 0)
    sm = sm_ref[0]

    qi = qi_ref[0]
    qi_h = []
    for h in range(IDX_HEADS):
        blk = qi[:, (h // 2) * LANES:(h // 2 + 1) * LANES]
        qi_h.append(jnp.where(lo if h % 2 == 0 else ~lo, blk, 0.0).astype(bf16))
    w_h = [sm[:, SM_WI + h:SM_WI + h + 1] for h in range(IDX_HEADS)]

    def score_chunk(c, carry):
        kt = kit_ref[0, c]
        sc = jnp.zeros((tq, kc), f32)
        for h in range(IDX_HEADS):
            s = jnp.dot(qi_h[h], kt, preferred_element_type=f32)
            sc = sc + w_h[h] * jnp.maximum(s, 0.0)
        idx = c * kc + lax.broadcasted_iota(i32, (tq, kc), 1)
        keys_ref[c] = jnp.where(idx <= t_col, _sortable(sc), INT_MIN)
        return carry

    lax.fori_loop(0, n_chunks, score_chunk, 0)

    thr, jm = _select_threshold(lambda c: (keys_ref[c], c * kc), n_chunks, kc, tq, n_sel, idx_bits, True)

    q = q_ref[0] * (HEAD_DIM ** -0.5)
    q_g = []
    for g in range(N_KV_HEADS):
        parts = []
        for j in range(GROUP):
            h = g * GROUP + j
            blk = q[:, (h // 2) * LANES:(h // 2 + 1) * LANES]
            parts.append(jnp.where(lo if h % 2 == 0 else ~lo, blk, 0.0).astype(bf16))
        q_g.append(jnp.concatenate(parts, axis=0))
    m_ref[...] = jnp.full(m_ref.shape, NEG_BIG, f32)
    l_ref[...] = jnp.zeros(l_ref.shape, f32)
    acc_ref[...] = jnp.zeros(acc_ref.shape, f32)

    def attend_chunk(c, carry):
        keys = keys_ref[c]
        idx = c * kc + lax.broadcasted_iota(i32, (tq, kc), 1)
        sel = _selected(keys, idx, thr, jm)
        for g in range(N_KV_HEADS):
            lg = jnp.dot(q_g[g], kt_ref[0, g, c], preferred_element_type=f32).reshape(GROUP, tq, kc)
            lg = jnp.where(sel[None], lg, NEG_BIG)
            m_old = m_ref[g]
            m_new = jnp.maximum(m_old, jnp.max(lg, axis=-1, keepdims=True))
            p = jnp.exp(lg - m_new)
            alpha = jnp.exp(m_old - m_new)
            l_ref[g] = alpha * l_ref[g] + jnp.sum(p, axis=-1, keepdims=True)
            pv = jnp.dot(p.reshape(GROUP * tq, kc).astype(bf16), vd_ref[0, g, c], preferred_element_type=f32)
            acc_ref[g] = alpha * acc_ref[g] + pv.reshape(GROUP, tq, LANES)
            m_ref[g] = m_new
        return carry

    lax.fori_loop(0, n_chunks, attend_chunk, 0)

    for g in range(N_KV_HEADS):
        out = acc_ref[g] / l_ref[g]
        for jp in range(GROUP // 2):
            o_ref[0, :, (g * 2 + jp) * LANES:(g * 2 + jp + 1) * LANES] = jnp.where(lo, out[2 * jp], out[2 * jp + 1])


def dsa_prompt(q, qi, sm, kit2, kt2, vd):
    b, t = q.shape[:2]
    nc, kc = kit2.shape[1], kit2.shape[3]
    tq = min(TQ, t)
    n_sel = min(TOPK_MAX, t // 4)
    idx_bits = max(1, (t - 1).bit_length())
    tok = lambda w: pl.BlockSpec((1, tq, w), lambda bi, qb: (bi, qb, 0))
    return pl.pallas_call(
        functools.partial(_dsa_prompt_kernel, n_sel=n_sel, idx_bits=idx_bits),
        grid=(b, t // tq),
        in_specs=[tok(A_WIDTH), tok(IDX_HEADS * IDX_DIM), tok(LANES),
                  pl.BlockSpec((1, nc, LANES, kc), lambda bi, qb: (bi, 0, 0, 0)),
                  pl.BlockSpec((1, N_KV_HEADS, nc, LANES, kc), lambda bi, qb: (bi, 0, 0, 0, 0)),
                  pl.BlockSpec((1, N_KV_HEADS, nc, kc, LANES), lambda bi, qb: (bi, 0, 0, 0, 0))],
        out_specs=tok(A_WIDTH),
        out_shape=jax.ShapeDtypeStruct((b, t, A_WIDTH), f32),
        scratch_shapes=[pltpu.VMEM((nc, tq, kc), i32),
                        pltpu.VMEM((N_KV_HEADS, GROUP, tq, 1), f32),
                        pltpu.VMEM((N_KV_HEADS, GROUP, tq, 1), f32),
                        pltpu.VMEM((N_KV_HEADS, GROUP, tq, LANES), f32)],
        compiler_params=_cparams(("parallel", "arbitrary")),
        name="dsa_prompt",
    )(q, qi, sm, kit2, kt2, vd)


def dsa_prompt_branch(q, qi, sm, k, v):
    b, t = q.shape[:2]
    kc = min(KC, t)
    nc = t // kc
    kit = sm[:, :, :IDX_DIM].astype(bf16).reshape(b, nc, kc, IDX_DIM).transpose(0, 1, 3, 2)
    kit2 = jnp.concatenate([kit, kit], axis=2)
    kt = k.astype(bf16).reshape(b, nc, kc, N_KV_HEADS, HEAD_DIM).transpose(0, 3, 1, 4, 2)
    kt2 = jnp.concatenate([kt, kt], axis=3)
    vb = v.astype(bf16).reshape(b, nc, kc, N_KV_HEADS, HEAD_DIM).transpose(0, 3, 1, 2, 4)
    vd = jnp.concatenate([vb, vb], axis=4)
    return dsa_prompt(q, qi, sm, kit2, kt2, vd)


def _idx_scores(qi, w, kpage):
    s = lax.dot_general(qi, kpage.astype(bf16), (((1,), (1,)), ((), ())), preferred_element_type=f32)
    r = (w * jnp.maximum(s, 0.0)).reshape(IDX_HEADS, SUBLANES, s.shape[1])
    sc = r[0]
    for h in range(1, IDX_HEADS):
        sc = sc + r[h]
    return sc


def _ds_score_kernel(pt_ref, qi_ref, w_ref, kin_ref, *rest):
    g = PAGES_PER_STEP
    pages = rest[:g]
    keys_ref = rest[g]
    s = pl.program_id(1)
    last = pl.num_programs(1) - 1
    qi = qi_ref[0]
    w = w_ref[0]

    @pl.when(s < last)
    def _():
        for j in range(g):
            keys_ref[0, :, j * PAGE_SIZE:(j + 1) * PAGE_SIZE] = _sortable(_idx_scores(qi, w, pages[j][0]))

    @pl.when(s == last)
    def _():
        keys_ref[0] = jnp.full(keys_ref.shape[1:], INT_MIN, i32)
        sc = _idx_scores(qi, w, kin_ref[0])
        tok = lax.broadcasted_iota(i32, sc.shape, 0)
        j = lax.broadcasted_iota(i32, sc.shape, 1)
        keys_ref[0, :, 0:PAGE_SIZE] = jnp.where(j <= tok, _sortable(sc), INT_MIN)


def _ds_thr_kernel(keys_ref, thr_ref, jm_ref, *, n_sel, idx_bits, chunk_w):
    rows, width = keys_ref.shape

    def get_chunk(c):
        return keys_ref[:, c * chunk_w:(c + 1) * chunk_w], c * chunk_w

    thr, jm = _select_threshold(get_chunk, width // chunk_w, chunk_w, rows, n_sel, idx_bits, False)
    thr_ref[...] = jnp.broadcast_to(thr, thr_ref.shape)
    jm_ref[...] = jnp.broadcast_to(jm, jm_ref.shape)


def _ds_attn_kernel(pt_ref, q_ref, keys_ref, thr_ref, jm_ref, knew_ref, vnew_ref, *rest, past):
    g = PAGES_PER_STEP
    kpages = rest[:g]
    vpages = rest[g:2 * g]
    o_ref, m_ref, l_ref, acc_ref = rest[2 * g:]
    s = pl.program_id(1)
    last = pl.num_programs(1) - 1
    q = q_ref[0]
    thr = thr_ref[0][:, 0:1]
    jm = jm_ref[0][:, 0:1]

    def process(keys, base, kpage, vpage):
        idx = base + lax.broadcasted_iota(i32, keys.shape, 1)
        sel = _selected(keys, idx, thr, jm)
        lg = lax.dot_general(q, kpage.astype(bf16), (((1,), (1,)), ((), ())), preferred_element_type=f32)
        lg = jnp.where(sel[None], lg.reshape(N_HEADS, SUBLANES, PAGE_SIZE), NEG_BIG)
        m_old = m_ref[...]
        m_new = jnp.maximum(m_old, jnp.max(lg, axis=-1, keepdims=True))
        p = jnp.exp(lg - m_new)
        alpha = jnp.exp(m_old - m_new)
        l_ref[...] = alpha * l_ref[...] + jnp.sum(p, axis=-1, keepdims=True)
        pv = jnp.dot(p.reshape(N_HEADS * SUBLANES, PAGE_SIZE).astype(bf16), vpage.astype(bf16),
                     preferred_element_type=f32)
        acc_ref[...] = alpha * acc_ref[...] + pv.reshape(N_HEADS, SUBLANES, LANES)
        m_ref[...] = m_new

    @pl.when(s == 0)
    def _():
        m_ref[...] = jnp.full(m_ref.shape, NEG_BIG, f32)
        l_ref[...] = jnp.zeros(l_ref.shape, f32)
        acc_ref[...] = jnp.zeros(acc_ref.shape, f32)

    @pl.when(s < last)
    def _():
        for j in range(g):
            keys = keys_ref[0, :, j * PAGE_SIZE:(j + 1) * PAGE_SIZE]
            process(keys, (s * g + j) * PAGE_SIZE, kpages[j][0], vpages[j][0])

    @pl.when(s == last)
    def _():
        process(keys_ref[0, :, 0:PAGE_SIZE], past, knew_ref[0], vnew_ref[0])
        o_ref[0] = (acc_ref[...] / l_ref[...]).reshape(N_HEADS * SUBLANES, LANES)


def dsa_sample(q, qi, sm, k, v, cache_k, cache_v, cache_idx_k, page_table):
    b, t = q.shape[:2]
    assert t <= SUBLANES
    n_pages = page_table.shape[1]
    g = PAGES_PER_STEP
    assert n_pages % g == 0
    ns = n_pages // g
    past = n_pages * PAGE_SIZE
    n_sel = min(TOPK_MAX, (past + t) // 4)
    idx_bits = (past + t - 1).bit_length()
    pt = page_table.reshape(-1).astype(i32)
    padt = lambda a: jnp.pad(a, ((0, 0), (0, SUBLANES - t)) + ((0, 0),) * (a.ndim - 2))

    qi_r = padt(qi.reshape(b, t, IDX_HEADS, IDX_DIM)).transpose(0, 2, 1, 3).reshape(b, IDX_HEADS * SUBLANES, IDX_DIM)
    w_r = padt(sm[:, :, SM_WI:SM_WI + IDX_HEADS]).transpose(0, 2, 1).reshape(b, IDX_HEADS * SUBLANES, 1)
    padk = lambda a: jnp.pad(a, ((0, 0), (0, PAGE_SIZE - t), (0, 0)))
    ki_new = padk(sm[:, :, :IDX_DIM])
    k_new = padk(k)
    v_new = padk(v)
    qh = padt(q.reshape(b, t, N_HEADS, HEAD_DIM)).transpose(0, 2, 1, 3) * (HEAD_DIM ** -0.5)
    grp = (jnp.arange(N_HEADS) // GROUP)[None, :, None, None]
    q_r = jnp.concatenate([jnp.where(grp == 0, qh, 0.0), jnp.where(grp == 1, qh, 0.0)], axis=-1)
    q_r = q_r.reshape(b, N_HEADS * SUBLANES, LANES).astype(bf16)
    ck = cache_k.reshape(cache_k.shape[0], PAGE_SIZE, N_KV_HEADS * HEAD_DIM)
    cv = cache_v.reshape(cache_v.shape[0], PAGE_SIZE, N_KV_HEADS * HEAD_DIM)

    def page_spec(j, w):
        return pl.BlockSpec((1, PAGE_SIZE, w),
                            lambda bi, s, ptr: (ptr[bi * n_pages + jnp.minimum(s, ns - 1) * g + j], 0, 0))

    per_b = lambda a: pl.BlockSpec((1,) + a.shape[1:], lambda bi, s, ptr: (bi,) + (0,) * (a.ndim - 1))
    blk_w = g * PAGE_SIZE
    keys_spec = pl.BlockSpec((1, SUBLANES, blk_w), lambda bi, s, ptr: (bi, 0, s))
    width = past + blk_w

    keys = pl.pallas_call(
        _ds_score_kernel,
        grid_spec=pltpu.PrefetchScalarGridSpec(
            num_scalar_prefetch=1, grid=(b, ns + 1),
            in_specs=[per_b(qi_r), per_b(w_r), per_b(ki_new)] + [page_spec(j, IDX_DIM) for j in range(g)],
            out_specs=keys_spec),
        out_shape=jax.ShapeDtypeStruct((b, SUBLANES, width), i32),
        compiler_params=_cparams(("parallel", "arbitrary")),
        name="dsa_sample_scores",
    )(pt, qi_r.astype(bf16), w_r, ki_new, *([cache_idx_k] * g))

    rows = b * SUBLANES
    thr, jm = pl.pallas_call(
        functools.partial(_ds_thr_kernel, n_sel=n_sel, idx_bits=idx_bits, chunk_w=blk_w),
        out_shape=[jax.ShapeDtypeStruct((rows, LANES), i32)] * 2,
        compiler_params=pltpu.CompilerParams(vmem_limit_bytes=VMEM_LIMIT),
        name="dsa_sample_threshold",
    )(keys.reshape(rows, width))
    thr = thr.reshape(b, SUBLANES, LANES)
    jm = jm.reshape(b, SUBLANES, LANES)

    out = pl.pallas_call(
        functools.partial(_ds_attn_kernel, past=past),
        grid_spec=pltpu.PrefetchScalarGridSpec(
            num_scalar_prefetch=1, grid=(b, ns + 1),
            in_specs=[per_b(q_r), keys_spec, per_b(thr), per_b(jm), per_b(k_new), per_b(v_new)]
            + [page_spec(j, LANES) for j in range(g)] * 2,
            out_specs=per_b(q_r),
            scratch_shapes=[pltpu.VMEM((N_HEADS, SUBLANES, 1), f32), pltpu.VMEM((N_HEADS, SUBLANES, 1), f32),
                            pltpu.VMEM((N_HEADS, SUBLANES, LANES), f32)]),
        out_shape=jax.ShapeDtypeStruct((b, N_HEADS * SUBLANES, LANES), f32),
        compiler_params=_cparams(("parallel", "arbitrary")),
        name="dsa_sample_attention",
    )(pt, q_r, keys, thr, jm, k_new, v_new, *([ck] * g), *([cv] * g))
    out = out.reshape(b, N_HEADS, SUBLANES, N_KV_HEADS, HEAD_DIM)[:, :, :t]
    out = jnp.concatenate([out[:, :GROUP, :, 0], out[:, GROUP:, :, 1]], axis=1)
    return out.transpose(0, 2, 1, 3).reshape(b, t, A_WIDTH)


def _mid_kernel(x_ref, oa_ref, ob_ref, ga_ref, gb_ref, wa_ref, wb_ref, wo_ref, gf_ref, wrh_ref, wrl_ref, br_ref,
                x1_ref, h2_ref, te_ref, tg_ref):
    a = jnp.dot(oa_ref[...].astype(bf16), wa_ref[...], preferred_element_type=f32)
    b = jnp.dot(ob_ref[...].astype(bf16), wb_ref[...], preferred_element_type=f32)
    merged = ga_ref[...] * a + gb_ref[...] * b
    x1 = x_ref[...] + jnp.dot(merged.astype(bf16), wo_ref[...], preferred_element_type=f32)
    x1_ref[...] = x1
    h2 = _rms(x1, gf_ref[...])
    h2_ref[...] = h2
    hi = h2.astype(bf16)
    lo = (h2 - hi.astype(f32)).astype(bf16)
    lg = (jnp.dot(hi, wrh_ref[...], preferred_element_type=f32) + jnp.dot(lo, wrh_ref[...], preferred_element_type=f32)
          + jnp.dot(hi, wrl_ref[...], preferred_element_type=f32)) + br_ref[...]
    lane = lax.broadcasted_iota(i32, lg.shape, 1)
    lane_f = lane.astype(f32)
    vals, ids = [], []
    for _ in range(TOP_K):
        m = jnp.max(lg, axis=1, keepdims=True)
        idx = jnp.min(jnp.where(lg == m, lane_f, float(LANES)), axis=1, keepdims=True).astype(i32)
        vals.append(m)
        ids.append(idx)
        lg = jnp.where(lane == idx, -jnp.inf, lg)
    ex = [jnp.exp(v - vals[0]) for v in vals]
    tot = ex[0] + ex[1] + ex[2] + ex[3]
    te = jnp.zeros(lg.shape, i32)
    tg = jnp.zeros(lg.shape, f32)
    for j in range(TOP_K):
        te = jnp.where(lane == j, ids[j], te)
        tg = jnp.where(lane == j, ex[j] / tot, tg)
    te_ref[...] = te
    tg_ref[...] = tg


def mid(x, oa, ob, ga, gb, wa, wb, wo, g_ffn, wr_hi, wr_lo, br):
    n = x.shape[0]
    tm = min(256, n)
    assert n % tm == 0
    row = lambda w: pl.BlockSpec((tm, w), lambda i: (i, 0))
    full = lambda a: pl.BlockSpec(a.shape, lambda i: (0,) * a.ndim)
    return pl.pallas_call(
        _mid_kernel,
        grid=(n // tm,),
        in_specs=[row(D_MODEL), row(A_WIDTH), row(M_WIDTH), row(D_MODEL), row(D_MODEL),
                  full(wa), full(wb), full(wo), full(g_ffn), full(wr_hi), full(wr_lo), full(br)],
        out_specs=[row(D_MODEL), row(D_MODEL), row(LANES), row(LANES)],
        out_shape=[jax.ShapeDtypeStruct((n, D_MODEL), f32), jax.ShapeDtypeStruct((n, D_MODEL), f32),
                   jax.ShapeDtypeStruct((n, LANES), i32), jax.ShapeDtypeStruct((n, LANES), f32)],
        compiler_params=_cparams(("parallel",)),
        name="mid",
    )(x, oa, ob, ga, gb, wa, wb, wo, g_ffn, wr_hi, wr_lo, br)


def _rank_kernel(te_ref, dest_ref, meta_ref, cnt_ref, carry_ref, *, blk):
    ph = pl.program_id(0)
    i = pl.program_id(1)
    tm = te_ref.shape[0]
    lane = lax.broadcasted_iota(i32, (tm, LANES), 1)
    te = te_ref[...]
    oh = jnp.zeros((tm, LANES), f32)
    for j in range(TOP_K):
        oh = oh + jnp.where(lane == te[:, j:j + 1], 1.0, 0.0)
    tile_cnt = jnp.sum(oh, axis=0, keepdims=True)

    @pl.when((ph == 0) & (i == 0))
    def _():
        cnt_ref[...] = jnp.zeros(cnt_ref.shape, f32)

    @pl.when(ph == 0)
    def _():
        cnt_ref[...] += tile_cnt

    @pl.when((ph == 1) & (i == 0))
    def _():
        cnt = cnt_ref[...]
        padded = jnp.floor((cnt + (blk - 1)) / blk) * blk
        r = lax.broadcasted_iota(i32, (LANES, LANES), 0)
        c = lax.broadcasted_iota(i32, (LANES, LANES), 1)
        col = jnp.sum(jnp.where(r == c, padded, 0.0), axis=1, keepdims=True)
        start = jnp.sum(jnp.where(r < c, col, 0.0), axis=0, keepdims=True)
        carry_ref[...] = start
        pad_end = start + padded
        nbp = meta_ref.shape[0]
        jb = (lax.broadcasted_iota(i32, (nbp, LANES), 0) * blk).astype(f32)
        lane2 = lax.broadcasted_iota(i32, (nbp, LANES), 1)
        be = jnp.sum(jnp.where((pad_end <= jb) & (lane2 < N_EXPERTS), 1.0, 0.0), axis=1, keepdims=True)
        be = jnp.minimum(be, float(N_EXPERTS - 1))
        n_act = jnp.sum(jnp.where(lane2 == N_EXPERTS - 1, pad_end, 0.0), axis=1, keepdims=True) / blk
        meta_ref[...] = jnp.where(lane2 == 0, be, jnp.where(lane2 == 1, n_act, 0.0)).astype(i32)

    @pl.when(ph == 1)
    def _():
        r = lax.broadcasted_iota(i32, (tm, tm), 0)
        c = lax.broadcasted_iota(i32, (tm, tm), 1)
        before = jnp.where(c < r, 1.0, 0.0).astype(bf16)
        pos = carry_ref[...] + jnp.dot(before, oh.astype(bf16), preferred_element_type=f32)
        d = jnp.zeros((tm, LANES), f32)
        for j in range(TOP_K):
            dj = jnp.sum(jnp.where(lane == te[:, j:j + 1], pos, 0.0), axis=1, keepdims=True)
            d = jnp.where(lane == j, dj, d)
        dest_ref[...] = d.astype(i32)
        carry_ref[...] += tile_cnt


def rank(te, blk, n_blocks):
    n = te.shape[0]
    tm = min(256, n)
    assert n % tm == 0
    nbp = -(-n_blocks // SUBLANES) * SUBLANES
    return pl.pallas_call(
        functools.partial(_rank_kernel, blk=blk),
        grid=(2, n // tm),
        in_specs=[pl.BlockSpec((tm, LANES), lambda ph, i: (i, 0))],
        out_specs=[pl.BlockSpec((tm, LANES), lambda ph, i: (i * ph, 0)),
                   pl.BlockSpec((nbp, LANES), lambda ph, i: (0, 0))],
        out_shape=[jax.ShapeDtypeStruct((n, LANES), i32), jax.ShapeDtypeStruct((nbp, LANES), i32)],
        scratch_shapes=[pltpu.VMEM((1, LANES), f32), pltpu.VMEM((1, LANES), f32)],
        compiler_params=_cparams(("arbitrary", "arbitrary")),
        name="rank",
    )(te)


def _ffn_kernel(be_ref, nact_ref, rt_ref, x_hbm, wgu_ref, bgu_ref, wdn_ref, bdn_ref, y_ref,
                xbuf, sem, wgu_bf, wdn_bf):
    i = pl.program_id(0)
    n_act = nact_ref[0]
    blk = xbuf.shape[1]

    def row_copy(b, r, slot):
        tok = rt_ref[b * blk + r]
        return pltpu.make_async_copy(x_hbm.at[pl.ds(tok, 1)], xbuf.at[slot, pl.ds(r, 1)], sem.at[slot])

    def start_rows(b, slot):
        def body(r, c):
            row_copy(b, r, slot).start()
            return c
        lax.fori_loop(0, blk, body, 0)

    def wait_rows(b, slot):
        def body(r, c):
            row_copy(b, r, slot).wait()
            return c
        lax.fori_loop(0, blk, body, 0)

    @pl.when((i == 0) & (n_act > 0))
    def _():
        start_rows(0, 0)

    @pl.when(i + 1 < n_act)
    def _():
        start_rows(i + 1, (i + 1) % 2)

    @pl.when(i >= n_act)
    def _():
        y_ref[...] = jnp.zeros(y_ref.shape, f32)

    @pl.when(i < n_act)
    def _():
        slot = i % 2
        wait_rows(i, slot)

        @pl.when((i == 0) | (be_ref[i] != be_ref[jnp.maximum(i - 1, 0)]))
        def _():
            wgu_bf[...] = wgu_ref[0].astype(bf16)
            wdn_bf[...] = wdn_ref[0].astype(bf16)

        x = xbuf[slot].astype(bf16)
        gu = jnp.dot(x, wgu_bf[...], preferred_element_type=f32) + bgu_ref[0]
        gate = jnp.minimum(gu[:, :EXPERT_FF], SWIGLU_LIMIT)
        up = jnp.clip(gu[:, EXPERT_FF:], -SWIGLU_LIMIT, SWIGLU_LIMIT)
        act = (up + 1.0) * (gate * _sigmoid(SWIGLU_ALPHA * gate))
        y_ref[...] = jnp.dot(act.astype(bf16), wdn_bf[...], preferred_element_type=f32) + bdn_ref[0]


def ffn(blk_e, n_act, row_tok, h2, w_gu, b_gu, w_dn, b_dn, blk, n_blocks):
    d = h2.shape[1]
    ff2 = w_gu.shape[2]
    return pl.pallas_call(
        _ffn_kernel,
        grid_spec=pltpu.PrefetchScalarGridSpec(
            num_scalar_prefetch=3, grid=(n_blocks,),
            in_specs=[pl.BlockSpec(memory_space=pl.ANY),
                      pl.BlockSpec((1, d, ff2), lambda i, be, na, rt: (be[i], 0, 0)),
                      pl.BlockSpec((1, 1, ff2), lambda i, be, na, rt: (be[i], 0, 0)),
                      pl.BlockSpec((1, ff2 // 2, d), lambda i, be, na, rt: (be[i], 0, 0)),
                      pl.BlockSpec((1, 1, d), lambda i, be, na, rt: (be[i], 0, 0))],
            out_specs=pl.BlockSpec((blk, d), lambda i, be, na, rt: (i, 0)),
            scratch_shapes=[pltpu.VMEM((2, blk, d), f32), pltpu.SemaphoreType.DMA((2,)),
                            pltpu.VMEM((d, ff2), bf16), pltpu.VMEM((ff2 // 2, d), bf16)]),
        out_shape=jax.ShapeDtypeStruct((n_blocks * blk, d), f32),
        compiler_params=_cparams(("arbitrary",)),
        name="ffn",
    )(blk_e, n_act, row_tok, h2, w_gu, b_gu.reshape(b_gu.shape[0], 1, ff2), w_dn, b_dn.reshape(b_dn.shape[0], 1, d))


def _out_kernel(dest_ref, x1_ref, tg_ref, p_ref, ys_hbm, gp_ref, wg_ref, wp_ref, gfin_ref, y_ref, buf, sem):
    i = pl.program_id(0)
    n = pl.num_programs(0)
    tm = x1_ref.shape[0]

    def row_copy(t, r, j, slot):
        d = dest_ref[(t * tm + r) * TOP_K + j]
        return pltpu.make_async_copy(ys_hbm.at[pl.ds(d, 1)], buf.at[slot, j, pl.ds(r, 1)], sem.at[slot])

    def start_rows(t, slot):
        def body(r, c):
            for j in range(TOP_K):
                row_copy(t, r, j, slot).start()
            return c
        lax.fori_loop(0, tm, body, 0)

    def wait_rows(t, slot):
        def body(r, c):
            for j in range(TOP_K):
                row_copy(t, r, j, slot).wait()
            return c
        lax.fori_loop(0, tm, body, 0)

    @pl.when(i == 0)
    def _():
        start_rows(0, 0)

    @pl.when(i + 1 < n)
    def _():
        start_rows(i + 1, (i + 1) % 2)

    slot = i % 2
    wait_rows(i, slot)
    tg = tg_ref[...]
    x2 = x1_ref[...]
    for j in range(TOP_K):
        x2 = x2 + tg[:, j:j + 1] * buf[slot, j]
    hn = _rms(x2, gp_ref[...]).astype(bf16)
    gate = _sigmoid(jnp.dot(hn, wg_ref[...], preferred_element_type=f32))
    x3 = x2 + gate * jnp.dot(p_ref[...].astype(bf16), wp_ref[...], preferred_element_type=f32)
    y_ref[...] = _rms(x3, gfin_ref[...])


def out_stage(dest_flat, x1, tg, p, ys, g_ple, wg, wp, g_final):
    n, d = x1.shape
    tm = min(OUT_TM, n)
    assert n % tm == 0
    row = lambda w: pl.BlockSpec((tm, w), lambda i, ds: (i, 0))
    full = lambda a: pl.BlockSpec(a.shape, lambda i, ds: (0,) * a.ndim)
    return pl.pallas_call(
        _out_kernel,
        grid_spec=pltpu.PrefetchScalarGridSpec(
            num_scalar_prefetch=1, grid=(n // tm,),
            in_specs=[row(d), row(LANES), row(p.shape[1]), pl.BlockSpec(memory_space=pl.ANY),
                      full(g_ple), full(wg), full(wp), full(g_final)],
            out_specs=row(d),
            scratch_shapes=[pltpu.VMEM((2, TOP_K, tm, d), f32), pltpu.SemaphoreType.DMA((2,))]),
        out_shape=jax.ShapeDtypeStruct((n, d), f32),
        compiler_params=_cparams(("arbitrary",)),
        name="out",
    )(dest_flat, x1, tg, p, ys, g_ple, wg, wp, g_final)


def tail(x, oa, ob, ga, gb, p, w):
    n = x.shape[0]
    x1, h2, te, tg = mid(x, oa, ob, ga, gb, w["wa"], w["wb"], w["wo"], w["g_ffn"], w["wr_hi"], w["wr_lo"], w["br"])
    blk = FFN_BLK
    n_blocks = -(-(n * TOP_K) // blk) + N_EXPERTS
    dest, meta = rank(te, blk, n_blocks)
    dest_flat = dest[:, :TOP_K].reshape(-1)
    blk_e = meta[:n_blocks, 0]
    n_act = meta[0:1, 1]
    tok = jnp.repeat(jnp.arange(n, dtype=i32), TOP_K)
    row_tok = jnp.zeros((n_blocks * blk,), i32).at[dest_flat].set(tok)
    ys = ffn(blk_e, n_act, row_tok, h2, w["w_gu"], w["b_gu"], w["w_dn"], w["b_dn"], blk, n_blocks)
    return out_stage(dest_flat, x1, tg, p, ys, w["g_ple"], w["wg"], w["wp"], w["g_final"])


def _prep_weights(g_mix, w_in, b_gate, g_mnorm, w_up_a, w_up_b, w_out, g_ffn, w_router, b_router,
                  w_gu, b_gu, w_dn, b_dn, g_ple, w_ple_gate, w_ple_proj, g_final):
    wr = jnp.pad(w_router.astype(f32), ((0, 0), (0, LANES - N_EXPERTS)))
    wr_hi = wr.astype(bf16)
    wr_lo = (wr - wr_hi.astype(f32)).astype(bf16)
    br = jnp.full((1, LANES), -jnp.inf, f32).at[0, :N_EXPERTS].set(b_router.astype(f32))
    bias = jnp.zeros((1, LANES), f32).at[0, SM_IG:SM_IG + 2 * M_HEADS].set(b_gate.astype(f32))
    return dict(g_mix=g_mix.reshape(1, -1), w_in=_pack_w_in(w_in), bias=bias, g_mnorm=g_mnorm,
                wa=w_up_a.astype(bf16), wb=w_up_b.astype(bf16), wo=w_out.astype(bf16), g_ffn=g_ffn.reshape(1, -1),
                wr_hi=wr_hi, wr_lo=wr_lo, br=br, w_gu=w_gu, b_gu=b_gu, w_dn=w_dn, b_dn=b_dn,
                g_ple=g_ple.reshape(1, -1), wg=w_ple_gate.astype(bf16), wp=w_ple_proj.astype(bf16),
                g_final=g_final.reshape(1, -1))


def _layer(x, p, pos, attn_fn, c0, n0, m0, w):
    b, t = x.shape[:2]
    n = b * t
    cos, sin = _rope_tables(pos)
    cos = jnp.tile(cos, (b, 1))
    sin = jnp.tile(sin, (b, 1))
    x2d = x.reshape(n, D_MODEL)
    q, k, v, qi, sm, mq, mk, mv, og, ga, gb = in_proj(x2d, cos, sin, w["g_mix"], w["w_in"], w["bias"])
    r3 = lambda a: a.reshape(b, t, a.shape[-1])
    o_a = attn_fn(r3(q), r3(qi), r3(sm), r3(k), r3(v))
    o_b, c, nn, m = mlstm_branch(r3(mq), r3(mk), r3(mv), r3(og), r3(sm), c0, n0, m0, w["g_mnorm"])
    y = tail(x2d, o_a.reshape(n, A_WIDTH), o_b.reshape(n, M_WIDTH), ga, gb, p.reshape(n, PLE_DIM), w)
    state = (k.reshape(b, t, N_KV_HEADS, HEAD_DIM), v.reshape(b, t, N_KV_HEADS, HEAD_DIM),
             sm[:, :IDX_DIM].reshape(b, t, IDX_DIM), c, nn, m)
    return y.reshape(b, t, D_MODEL), state


def kernel(x_prompt, x_sample, cache_k, cache_v, cache_idx_k, state_C, state_n, state_m, page_table,
           p_prompt, p_sample, g_mix, w_in, b_gate, g_mnorm, w_up_a, w_up_b, w_out, g_ffn,
           w_router, b_router, w_gu, b_gu, w_dn, b_dn, g_ple, w_ple_gate, w_ple_proj, g_final):
    assert x_prompt.shape[-1] == D_MODEL and w_in.shape[0] == 1, "single-layer model of width D_MODEL"
    bp, tp = x_prompt.shape[:2]
    ts = x_sample.shape[1]
    past = page_table.shape[1] * PAGE_SIZE
    w = _prep_weights(g_mix[0], w_in[0], b_gate[0], g_mnorm[0], w_up_a[0], w_up_b[0], w_out[0], g_ffn[0],
                      w_router[0], b_router[0], w_gu[0], b_gu[0], w_dn[0], b_dn[0], g_ple[0],
                      w_ple_gate[0], w_ple_proj[0], g_final)
    zeros = lambda *s: jnp.zeros(s, f32)
    yp, sp = _layer(x_prompt, p_prompt[0], jnp.arange(tp, dtype=i32), dsa_prompt_branch,
                    zeros(bp, M_HEADS, M_V, M_QK), zeros(bp, M_HEADS, M_QK), zeros(bp, M_HEADS), w)
    attn_s = functools.partial(dsa_sample, cache_k=cache_k[0], cache_v=cache_v[0], cache_idx_k=cache_idx_k[0],
                               page_table=page_table)
    ys, ss = _layer(x_sample, p_sample[0], past + jnp.arange(ts, dtype=i32), attn_s,
                    state_C[0], state_n[0], state_m[0], w)
    return (yp, ys) + tuple(s[None] for s in sp) + tuple(s[None] for s in ss)
```

```python
import functools

import jax
import jax.numpy as jnp
from jax import lax
from jax.experimental import pallas as pl
from jax.experimental.pallas import tpu as pltpu

f32 = jnp.float32
bf16 = jnp.bfloat16
i32 = jnp.int32

D_MODEL = 1024
PAGE_SIZE = 128
N_HEADS = 8
N_KV_HEADS = 2
HEAD_DIM = 64
GROUP = N_HEADS // N_KV_HEADS
IDX_HEADS = 4
IDX_DIM = 64
TOPK_MAX = 256
ROPE_THETA = 10000.0
M_HEADS = 4
M_QK = 64
M_V = 128
N_EXPERTS = 32
TOP_K = 4
EXPERT_FF = D_MODEL
SWIGLU_LIMIT = 7.0
SWIGLU_ALPHA = 1.702
PLE_DIM = 256
EPS = 1e-6
A_WIDTH = N_HEADS * HEAD_DIM
M_WIDTH = M_HEADS * M_V

LANES = 128
SUBLANES = 8
VMEM_LIMIT = 56 * 1024 * 1024

M_CHUNK = 128
TQ = 128
KC = 512
PAGES_PER_STEP = 16
FFN_BLK = 256
OUT_TM = 128

INT_MIN = -2147483648
INT_MAX = 2147483647
NEG_BIG = -1e30
LOG2E = 1.4426950408889634
ROW_SLICE = 16

SM_KI = 0
SM_WI = IDX_DIM
SM_IG = SM_WI + IDX_HEADS
SM_LF = SM_IG + M_HEADS


def _cparams(sem):
    return pltpu.CompilerParams(dimension_semantics=sem, vmem_limit_bytes=VMEM_LIMIT)


def _rms(x, g):
    return x * lax.rsqrt(jnp.mean(x * x, axis=-1, keepdims=True) + EPS) * g


def _sigmoid(x):
    return 1.0 / (1.0 + jnp.exp(-x))


def _log_sigmoid(x):
    return jnp.minimum(x, 0.0) - jnp.log1p(jnp.exp(-jnp.abs(x)))


def _sortable(x):
    bits = lax.bitcast_convert_type(x, i32)
    key = bits ^ ((bits >> 31) & INT_MAX)
    return jnp.where(x == 0.0, 0, key)


_G_Q = (0, 512)
_G_K = (512, 640)
_G_V = (640, 768)
_G_QI = (768, 1024)
_G_SM = (1024, 1152)
_G_MQ = (1152, 1408)
_G_MK = (1408, 1664)
_G_MV = (1664, 2176)
_G_MO = (2176, 2688)
_G_GA = (2688, 3712)
_G_GB = (3712, 4736)
_W_COLS = 4736


def _pack_w_in(w_in):
    o = [0]
    for s in (A_WIDTH, 128, 128, 256, 64, 4, 256, 256, 512, 4, 4, 512, 1024, 1024):
        o.append(o[-1] + s)
    aq, ak, av, iq, ik, iw, mq, mk, mv, mi, mf, mo, ga, gb = [w_in[:, o[i]:o[i + 1]] for i in range(14)]
    pad = jnp.zeros((w_in.shape[0], LANES - (IDX_DIM + IDX_HEADS + 2 * M_HEADS)), w_in.dtype)
    small = jnp.concatenate([ik, iw, mi, mf, pad], axis=1)
    w = jnp.concatenate([aq, ak, av, iq, small, mq, mk, mv, mo, ga, gb], axis=1)
    return w.astype(bf16)


def _in_kernel(x_ref, cos_ref, sin_ref, g_ref, w_ref, bias_ref,
               q_ref, k_ref, v_ref, qi_ref, sm_ref, mq_ref, mk_ref, mv_ref, og_ref, ga_ref, gb_ref):
    x = x_ref[...]
    hb = _rms(x, g_ref[...]).astype(bf16)
    cos = cos_ref[...]
    sin = sin_ref[...]
    tm = x.shape[0]
    lane = lax.broadcasted_iota(i32, (tm, LANES), 1)
    first_half = (lane % HEAD_DIM) < (HEAD_DIM // 2)

    def rope(z):
        rot = jnp.where(first_half, pltpu.roll(z, LANES - HEAD_DIM // 2, 1), pltpu.roll(z, HEAD_DIM // 2, 1))
        return z * cos + rot * sin

    def proj(grp):
        return jnp.dot(hb, w_ref[:, grp[0]:grp[1]], preferred_element_type=f32)

    z = proj(_G_Q)
    for j in range(4):
        q_ref[:, j * LANES:(j + 1) * LANES] = rope(z[:, j * LANES:(j + 1) * LANES])
    k_ref[...] = rope(proj(_G_K))
    v_ref[...] = proj(_G_V)
    z = proj(_G_QI)
    for j in range(2):
        qi_ref[:, j * LANES:(j + 1) * LANES] = rope(z[:, j * LANES:(j + 1) * LANES])
    z = proj(_G_SM)
    zb = z + bias_ref[...]
    sm = jnp.where(lane < SM_WI, rope(z),
                   jnp.where(lane < SM_IG, z * (IDX_HEADS ** -0.5 * IDX_DIM ** -0.5),
                             jnp.where(lane < SM_LF, zb,
                                       jnp.where(lane < SM_LF + M_HEADS, _log_sigmoid(zb), 0.0))))
    sm_ref[...] = sm
    mq_ref[...] = proj(_G_MQ)
    mk_ref[...] = proj(_G_MK) * (M_QK ** -0.5)
    mv_ref[...] = proj(_G_MV)
    og_ref[...] = _sigmoid(proj(_G_MO))
    ga_ref[...] = _sigmoid(proj(_G_GA))
    gb_ref[...] = _sigmoid(proj(_G_GB))


def _rope_tables(pos):
    half = HEAD_DIM // 2
    inv = ROPE_THETA ** (-jnp.arange(half, dtype=f32) / half)
    ang = pos.astype(f32)[:, None] * inv[None, :]
    cos = jnp.cos(ang)
    sin = jnp.sin(ang)
    cos128 = jnp.tile(cos, (1, 4))
    sin128 = jnp.tile(jnp.concatenate([-sin, sin], axis=1), (1, 2))
    return cos128, sin128


def in_proj(x2d, cos128, sin128, g_mix, w_packed, bias128):
    n = x2d.shape[0]
    tm = min(256, n)
    assert n % tm == 0
    row = lambda w: pl.BlockSpec((tm, w), lambda i: (i, 0))
    full = lambda a: pl.BlockSpec(a.shape, lambda i: (0,) * a.ndim)
    widths = (512, 128, 128, 256, 128, 256, 256, 512, 512, 1024, 1024)
    return pl.pallas_call(
        _in_kernel,
        grid=(n // tm,),
        in_specs=[row(D_MODEL), row(LANES), row(LANES), full(g_mix), full(w_packed), full(bias128)],
        out_specs=[row(w) for w in widths],
        out_shape=[jax.ShapeDtypeStruct((n, w), f32) for w in widths],
        compiler_params=_cparams(("parallel",)),
        name="in_proj",
    )(x2d, cos128, sin128, g_mix, w_packed, bias128)


def _mlstm_kernel(mq_ref, mk_ref, mv_ref, og_ref, sm_ref, gr_ref, c0_ref, n0_ref, m0_ref, gn_ref,
                  ob_ref, c_ref, n_ref, m_ref):
    ci = pl.program_id(1)
    L = mq_ref.shape[1]

    @pl.when(ci == 0)
    def _():
        c_ref[...] = c0_ref[...]
        n_ref[...] = n0_ref[...]
        m_ref[...] = m0_ref[...]

    sm = sm_ref[0]
    gr = gr_ref[0]
    row = lax.broadcasted_iota(i32, (L, L), 0)
    col = lax.broadcasted_iota(i32, (L, L), 1)
    tril = row >= col
    for hd in range(M_HEADS):
        q = mq_ref[0, :, hd * M_QK:(hd + 1) * M_QK]
        k = mk_ref[0, :, hd * M_QK:(hd + 1) * M_QK]
        v = mv_ref[0, :, hd * M_V:(hd + 1) * M_V]
        ig_r = gr[hd:hd + 1, :]
        lf_r = gr[M_HEADS + hd:M_HEADS + hd + 1, :]
        ig_c = sm[:, SM_IG + hd:SM_IG + hd + 1]
        lf_c = sm[:, SM_LF + hd:SM_LF + hd + 1]
        C = c_ref[0, hd]
        nrow = n_ref[0, hd]
        m_prev = m_ref[0, hd]
        b_c = jnp.sum(jnp.where(tril, lf_r, 0.0), axis=1, keepdims=True)
        b_r = jnp.sum(jnp.where(tril, 0.0, lf_c) + jnp.where(row == col, lf_c, 0.0), axis=0, keepdims=True)
        dmat = jnp.where(tril, b_c - b_r + ig_r, -jnp.inf)
        inter = b_c + m_prev
        m_t = jnp.maximum(inter, jnp.max(dmat, axis=1, keepdims=True))
        qb = q.astype(bf16)
        kb = k.astype(bf16)
        qk = lax.dot_general(qb, kb, (((1,), (1,)), ((), ())), preferred_element_type=f32)
        s = qk * jnp.exp(dmat - m_t)
        w_inter = jnp.exp(inter - m_t)
        qc = lax.dot_general(qb, C.astype(bf16), (((1,), (1,)), ((), ())), preferred_element_type=f32)
        num = jnp.dot(s.astype(bf16), v.astype(bf16), preferred_element_type=f32) + w_inter * qc
        den = jnp.sum(s, axis=1, keepdims=True) + w_inter * jnp.sum(q * nrow, axis=1, keepdims=True)
        h = num / jnp.maximum(jnp.abs(den), jnp.exp(-m_t))
        b_last = b_c[L - 1:L, :]
        g_c = b_last - b_c + ig_c
        m_new = jnp.maximum(b_last + m_prev, jnp.max(g_c, axis=0, keepdims=True))
        w_k = jnp.exp(g_c - m_new)
        decay = jnp.exp(b_last + m_prev - m_new)
        wv = (w_k * v).astype(bf16)
        c_ref[0, hd] = decay * C + lax.dot_general(wv, kb, (((0,), (0,)), ((), ())), preferred_element_type=f32)
        n_ref[0, hd] = decay * nrow + jnp.sum(w_k * k, axis=0, keepdims=True)
        m_ref[0, hd] = m_new
        gn = gn_ref[:, hd * M_V:(hd + 1) * M_V]
        ob_ref[0, :, hd * M_V:(hd + 1) * M_V] = og_ref[0, :, hd * M_V:(hd + 1) * M_V] * _rms(h, gn)


def mlstm(mq, mk, mv, og, sm, grow, c0, n0, m0, g_mnorm):
    b, t = mq.shape[:2]
    L = M_CHUNK
    assert t % L == 0
    tok = lambda w: pl.BlockSpec((1, L, w), lambda bi, ci: (bi, ci, 0))
    st = lambda a: pl.BlockSpec((1,) + a.shape[1:], lambda bi, ci: (bi,) + (0,) * (a.ndim - 1))
    return pl.pallas_call(
        _mlstm_kernel,
        grid=(b, t // L),
        in_specs=[tok(256), tok(256), tok(512), tok(512), tok(LANES),
                  pl.BlockSpec((1, 2 * M_HEADS, L), lambda bi, ci: (bi, 0, ci)),
                  st(c0), st(n0), st(m0), pl.BlockSpec(g_mnorm.shape, lambda bi, ci: (0, 0))],
        out_specs=[tok(512), st(c0), st(n0), st(m0)],
        out_shape=[jax.ShapeDtypeStruct((b, t, M_WIDTH), f32), jax.ShapeDtypeStruct(c0.shape, f32),
                   jax.ShapeDtypeStruct(n0.shape, f32), jax.ShapeDtypeStruct(m0.shape, f32)],
        compiler_params=_cparams(("parallel", "arbitrary")),
        name="mlstm",
    )(mq, mk, mv, og, sm, grow, c0, n0, m0, g_mnorm)


def mlstm_branch(mq, mk, mv, og, sm, c0, n0, m0, g_mnorm):
    b, t = mq.shape[:2]
    tp = -(-t // M_CHUNK) * M_CHUNK
    if tp != t:
        pad = lambda a: jnp.pad(a, ((0, 0), (0, tp - t), (0, 0)))
        mq, mk, mv, og = pad(mq), pad(mk), pad(mv), pad(og)
        sm_pad = jnp.zeros((b, tp - t, LANES), f32).at[:, :, SM_IG:SM_IG + M_HEADS].set(NEG_BIG)
        sm = jnp.concatenate([sm, sm_pad], axis=1)
    grow = sm[:, :, SM_IG:SM_IG + 2 * M_HEADS].transpose(0, 2, 1)
    ob, c, n, m = mlstm(mq, mk, mv, og, sm, grow, c0.astype(f32), n0.astype(f32).reshape(b, M_HEADS, 1, M_QK),
                        m0.astype(f32).reshape(b, M_HEADS, 1, 1), g_mnorm.reshape(1, M_WIDTH))
    return ob[:, :t], c, n.reshape(b, M_HEADS, M_QK), m.reshape(b, M_HEADS)


def _select_threshold(get_chunk, n_chunks, chunk_w, rows, n_sel, idx_bits, dynamic):
    def count(pred):
        def body(c, acc):
            keys, base = get_chunk(c)
            idx = base + lax.broadcasted_iota(i32, (rows, chunk_w), 1)
            hit = jnp.where(pred(keys, idx), 1.0, 0.0)
            for j in range(chunk_w // LANES):
                acc = acc + hit[:, j * LANES:(j + 1) * LANES]
            return acc
        acc0 = jnp.zeros((rows, LANES), f32)
        if dynamic:
            acc = lax.fori_loop(0, n_chunks, body, acc0)
        else:
            acc = acc0
            for c in range(n_chunks):
                acc = body(c, acc)
        return jnp.sum(acc, axis=1, keepdims=True)

    kf = float(n_sel)

    def unsettled(st):
        it, _, cnt = st
        return (it < 32) & (jnp.max(jnp.abs(cnt - kf)) > 0.0)

    def bit_step(st):
        it, thr_u, cnt = st
        cand_u = thr_u | lax.shift_left(jnp.int32(1), 31 - it)
        cand_s = cand_u ^ INT_MIN
        c = count(lambda keys, idx: keys >= cand_s)
        take = c >= kf
        return it + 1, jnp.where(take, cand_u, thr_u), jnp.where(take, c, cnt)

    total = jnp.zeros((rows, 1), f32) + (n_chunks * chunk_w).astype(f32) if dynamic else \
        jnp.full((rows, 1), float(n_chunks * chunk_w), f32)
    _, thr_u, _ = lax.while_loop(unsettled, bit_step, (jnp.int32(0), jnp.zeros((rows, 1), i32), total))
    thr = jnp.maximum(thr_u ^ INT_MIN, INT_MIN + 1)
    need = kf - count(lambda keys, idx: keys > thr)
    n_eq = count(lambda keys, idx: keys == thr)
    overflow = jnp.max(n_eq - need) > 0.0

    def tie_search():
        def step(it, jm):
            cand = jm | lax.shift_left(jnp.int32(1), idx_bits - 1 - it)
            below = count(lambda keys, idx: (keys == thr) & (idx < cand))
            return jnp.where(below < need, cand, jm)
        return lax.fori_loop(0, idx_bits, step, jnp.zeros((rows, 1), i32))

    jm = lax.cond(overflow, tie_search, lambda: jnp.full((rows, 1), INT_MAX, i32))
    return thr, jm


def _selected(keys, idx, thr, jm):
    return (keys > thr) | ((keys == thr) & (idx <= jm))


def _dsa_prompt_kernel(q_ref, qi_ref, sm_ref, kit_ref, kt_ref, vd_ref, o_ref,
                       keys_ref, bias_ref, lg_ref, p_ref, m_ref, acc_ref, *, n_sel, idx_bits):
    qb = pl.program_id(1)
    tq = q_ref.shape[1]
    kc = kit_ref.shape[3]
    n_chunks = (qb * tq + tq - 1) // kc + 1
    lane = lax.broadcasted_iota(i32, (tq, LANES), 1)
    lo = lane < HEAD_DIM
    t_col = qb * tq + lax.broadcasted_iota(i32, (tq, 1), 0)
    sm = sm_ref[0]

    qi = qi_ref[0]
    qi_h = []
    for h in range(IDX_HEADS):
        blk = qi[:, (h // 2) * LANES:(h // 2 + 1) * LANES]
        qi_h.append(jnp.where(lo if h % 2 == 0 else ~lo, blk, 0.0).astype(bf16))
    w_h = [sm[:, SM_WI + h:SM_WI + h + 1] for h in range(IDX_HEADS)]

    def score_chunk(c, carry):
        kt = kit_ref[0, c]
        sc = jnp.zeros((tq, kc), f32)
        for h in range(IDX_HEADS):
            s = jnp.dot(qi_h[h], kt, preferred_element_type=f32)
            sc = sc + w_h[h] * jnp.maximum(s, 0.0)
        idx = c * kc + lax.broadcasted_iota(i32, (tq, kc), 1)
        keys_ref[c] = jnp.where(idx <= t_col, _sortable(sc), INT_MIN)
        return carry

    lax.fori_loop(0, n_chunks, score_chunk, 0)

    thr, jm = _select_threshold(lambda c: (keys_ref[c], c * kc), n_chunks, kc, tq, n_sel, idx_bits, True)

    q = q_ref[0] * (HEAD_DIM ** -0.5 * LOG2E)
    q_g = []
    for g in range(N_KV_HEADS):
        parts = []
        for j in range(GROUP):
            h = g * GROUP + j
            blk = q[:, (h // 2) * LANES:(h // 2 + 1) * LANES]
            parts.append(jnp.where(lo if h % 2 == 0 else ~lo, blk, 0.0).astype(bf16))
        q_g.append(jnp.concatenate(parts, axis=0))
    m_ref[...] = jnp.full(m_ref.shape, NEG_BIG, f32)
    acc_ref[...] = jnp.zeros(acc_ref.shape, f32)
    rs = min(ROW_SLICE, tq)

    def attend_chunk(c, carry):
        keys = keys_ref[c]
        idx = c * kc + lax.broadcasted_iota(i32, (tq, kc), 1)
        bias_ref[...] = jnp.where(_selected(keys, idx, thr, jm), 0.0, NEG_BIG)
        for g in range(N_KV_HEADS):
            lg_ref[...] = jnp.dot(q_g[g], kt_ref[0, g, c], preferred_element_type=f32)
            for r0 in range(0, GROUP * tq, rs):
                x = lg_ref[r0:r0 + rs] + bias_ref[r0 % tq:r0 % tq + rs]
                m_old = m_ref[g, r0:r0 + rs]
                m_new = jnp.maximum(m_old, jnp.max(x, axis=-1, keepdims=True))
                p_ref[r0:r0 + rs] = jnp.exp2(x - m_new).astype(bf16)
                acc_ref[g, r0:r0 + rs] = jnp.exp2(m_old - m_new) * acc_ref[g, r0:r0 + rs]
                m_ref[g, r0:r0 + rs] = m_new
            acc_ref[g] += jnp.dot(p_ref[...], vd_ref[0, g, c], preferred_element_type=f32)
        return carry

    lax.fori_loop(0, n_chunks, attend_chunk, 0)

    for g in range(N_KV_HEADS):
        acc = acc_ref[g]
        out = acc / acc[:, HEAD_DIM:HEAD_DIM + 1]
        for jp in range(GROUP // 2):
            even = out[(2 * jp) * tq:(2 * jp + 1) * tq]
            odd = pltpu.roll(out[(2 * jp + 1) * tq:(2 * jp + 2) * tq], HEAD_DIM, 1)
            o_ref[0, :, (g * 2 + jp) * LANES:(g * 2 + jp + 1) * LANES] = jnp.where(lo, even, odd)


def dsa_prompt(q, qi, sm, kit2, kt2, vd):
    b, t = q.shape[:2]
    nc, kc = kit2.shape[1], kit2.shape[3]
    tq = min(TQ, t)
    n_sel = min(TOPK_MAX, t // 4)
    idx_bits = max(1, (t - 1).bit_length())
    tok = lambda w: pl.BlockSpec((1, tq, w), lambda bi, qb: (bi, qb, 0))
    return pl.pallas_call(
        functools.partial(_dsa_prompt_kernel, n_sel=n_sel, idx_bits=idx_bits),
        grid=(b, t // tq),
        in_specs=[tok(A_WIDTH), tok(IDX_HEADS * IDX_DIM), tok(LANES),
                  pl.BlockSpec((1, nc, LANES, kc), lambda bi, qb: (bi, 0, 0, 0)),
                  pl.BlockSpec((1, N_KV_HEADS, nc, LANES, kc), lambda bi, qb: (bi, 0, 0, 0, 0)),
                  pl.BlockSpec((1, N_KV_HEADS, nc, kc, LANES), lambda bi, qb: (bi, 0, 0, 0, 0))],
        out_specs=tok(A_WIDTH),
        out_shape=jax.ShapeDtypeStruct((b, t, A_WIDTH), f32),
        scratch_shapes=[pltpu.VMEM((nc, tq, kc), i32),
                        pltpu.VMEM((tq, kc), f32),
                        pltpu.VMEM((GROUP * tq, kc), f32),
                        pltpu.VMEM((GROUP * tq, kc), bf16),
                        pltpu.VMEM((N_KV_HEADS, GROUP * tq, 1), f32),
                        pltpu.VMEM((N_KV_HEADS, GROUP * tq, LANES), f32)],
        compiler_params=_cparams(("parallel", "arbitrary")),
        name="dsa_prompt",
    )(q, qi, sm, kit2, kt2, vd)


def dsa_prompt_branch(q, qi, sm, k, v):
    b, t = q.shape[:2]
    kc = min(KC, t)
    nc = t // kc
    kit = sm[:, :, :IDX_DIM].astype(bf16).reshape(b, nc, kc, IDX_DIM).transpose(0, 1, 3, 2)
    kit2 = jnp.concatenate([kit, kit], axis=2)
    kt = k.astype(bf16).reshape(b, nc, kc, N_KV_HEADS, HEAD_DIM).transpose(0, 3, 1, 4, 2)
    kt2 = jnp.concatenate([kt, kt], axis=3)
    vb = v.astype(bf16).reshape(b, nc, kc, N_KV_HEADS, HEAD_DIM).transpose(0, 3, 1, 2, 4)
    vd = jnp.concatenate([vb, jnp.ones_like(vb)], axis=4)
    return dsa_prompt(q, qi, sm, kit2, kt2, vd)


def _idx_scores(qi, w, kpage):
    s = lax.dot_general(qi, kpage.astype(bf16), (((1,), (1,)), ((), ())), preferred_element_type=f32)
    r = (w * jnp.maximum(s, 0.0)).reshape(IDX_HEADS, SUBLANES, s.shape[1])
    sc = r[0]
    for h in range(1, IDX_HEADS):
        sc = sc + r[h]
    return sc


def _ds_score_kernel(pt_ref, qi_ref, w_ref, kin_ref, *rest, g):
    pages = rest[:g]
    keys_ref = rest[g]
    s = pl.program_id(1)
    last = pl.num_programs(1) - 1
    qi = qi_ref[0]
    w = w_ref[0]

    @pl.when(s < last)
    def _():
        for j in range(g):
            keys_ref[0, :, j * PAGE_SIZE:(j + 1) * PAGE_SIZE] = _sortable(_idx_scores(qi, w, pages[j][0]))

    @pl.when(s == last)
    def _():
        keys_ref[0] = jnp.full(keys_ref.shape[1:], INT_MIN, i32)
        sc = _idx_scores(qi, w, kin_ref[0])
        tok = lax.broadcasted_iota(i32, sc.shape, 0)
        j = lax.broadcasted_iota(i32, sc.shape, 1)
        keys_ref[0, :, 0:PAGE_SIZE] = jnp.where(j <= tok, _sortable(sc), INT_MIN)


def _ds_thr_kernel(keys_ref, thr_ref, jm_ref, *, n_sel, idx_bits, chunk_w):
    rows, width = keys_ref.shape

    def get_chunk(c):
        return keys_ref[:, c * chunk_w:(c + 1) * chunk_w], c * chunk_w

    thr, jm = _select_threshold(get_chunk, width // chunk_w, chunk_w, rows, n_sel, idx_bits, False)
    thr_ref[...] = jnp.broadcast_to(thr, thr_ref.shape)
    jm_ref[...] = jnp.broadcast_to(jm, jm_ref.shape)


def _ds_attn_kernel(pt_ref, q_ref, keys_ref, thr_ref, jm_ref, knew_ref, vnew_ref, *rest, past, g):
    kpages = rest[:g]
    vpages = rest[g:2 * g]
    o_ref, m_ref, l_ref, acc_ref = rest[2 * g:]
    s = pl.program_id(1)
    last = pl.num_programs(1) - 1
    q = q_ref[0]
    thr = thr_ref[0][:, 0:1]
    jm = jm_ref[0][:, 0:1]

    def process(keys, base, kps, vps):
        n = len(kps)
        idx = base + lax.broadcasted_iota(i32, keys.shape, 1)
        sel = _selected(keys, idx, thr, jm)
        lg = jnp.concatenate(
            [lax.dot_general(q, kp.astype(bf16), (((1,), (1,)), ((), ())), preferred_element_type=f32) for kp in kps],
            axis=1)
        lg = jnp.where(sel[None], lg.reshape(N_HEADS, SUBLANES, n * PAGE_SIZE), NEG_BIG)
        m_old = m_ref[...]
        m_new = jnp.maximum(m_old, jnp.max(lg, axis=-1, keepdims=True))
        p = jnp.exp(lg - m_new)
        alpha = jnp.exp(m_old - m_new)
        l_ref[...] = alpha * l_ref[...] + jnp.sum(p, axis=-1, keepdims=True)
        pb = p.reshape(N_HEADS * SUBLANES, n * PAGE_SIZE).astype(bf16)
        pv = jnp.dot(pb[:, 0:PAGE_SIZE], vps[0].astype(bf16), preferred_element_type=f32)
        for j in range(1, n):
            pv = pv + jnp.dot(pb[:, j * PAGE_SIZE:(j + 1) * PAGE_SIZE], vps[j].astype(bf16),
                              preferred_element_type=f32)
        acc_ref[...] = alpha * acc_ref[...] + pv.reshape(N_HEADS, SUBLANES, LANES)
        m_ref[...] = m_new

    @pl.when(s == 0)
    def _():
        m_ref[...] = jnp.full(m_ref.shape, NEG_BIG, f32)
        l_ref[...] = jnp.zeros(l_ref.shape, f32)
        acc_ref[...] = jnp.zeros(acc_ref.shape, f32)

    @pl.when(s < last)
    def _():
        process(keys_ref[0], s * (g * PAGE_SIZE), [kp[0] for kp in kpages], [vp[0] for vp in vpages])

    @pl.when(s == last)
    def _():
        process(keys_ref[0, :, 0:PAGE_SIZE], past, [knew_ref[0]], [vnew_ref[0]])
        o_ref[0] = (acc_ref[...] / l_ref[...]).reshape(N_HEADS * SUBLANES, LANES)


def dsa_sample(q, qi, sm, k, v, cache_k, cache_v, cache_idx_k, page_table):
    b, t = q.shape[:2]
    assert t <= SUBLANES
    n_pages = page_table.shape[1]
    g = min(PAGES_PER_STEP, n_pages)
    assert n_pages % g == 0
    ns = n_pages // g
    past = n_pages * PAGE_SIZE
    n_sel = min(TOPK_MAX, (past + t) // 4)
    idx_bits = (past + t - 1).bit_length()
    pt = page_table.reshape(-1).astype(i32)
    padt = lambda a: jnp.pad(a, ((0, 0), (0, SUBLANES - t)) + ((0, 0),) * (a.ndim - 2))

    qi_r = padt(qi.reshape(b, t, IDX_HEADS, IDX_DIM)).transpose(0, 2, 1, 3).reshape(b, IDX_HEADS * SUBLANES, IDX_DIM)
    w_r = padt(sm[:, :, SM_WI:SM_WI + IDX_HEADS]).transpose(0, 2, 1).reshape(b, IDX_HEADS * SUBLANES, 1)
    padk = lambda a: jnp.pad(a, ((0, 0), (0, PAGE_SIZE - t), (0, 0)))
    ki_new = padk(sm[:, :, :IDX_DIM])
    k_new = padk(k)
    v_new = padk(v)
    qh = padt(q.reshape(b, t, N_HEADS, HEAD_DIM)).transpose(0, 2, 1, 3) * (HEAD_DIM ** -0.5)
    grp = (jnp.arange(N_HEADS) // GROUP)[None, :, None, None]
    q_r = jnp.concatenate([jnp.where(grp == 0, qh, 0.0), jnp.where(grp == 1, qh, 0.0)], axis=-1)
    q_r = q_r.reshape(b, N_HEADS * SUBLANES, LANES).astype(bf16)
    ck = cache_k.reshape(cache_k.shape[0], PAGE_SIZE, N_KV_HEADS * HEAD_DIM)
    cv = cache_v.reshape(cache_v.shape[0], PAGE_SIZE, N_KV_HEADS * HEAD_DIM)

    def page_spec(j, w):
        return pl.BlockSpec((1, PAGE_SIZE, w),
                            lambda bi, s, ptr: (ptr[bi * n_pages + jnp.minimum(s, ns - 1) * g + j], 0, 0))

    per_b = lambda a: pl.BlockSpec((1,) + a.shape[1:], lambda bi, s, ptr: (bi,) + (0,) * (a.ndim - 1))
    blk_w = g * PAGE_SIZE
    keys_spec = pl.BlockSpec((1, SUBLANES, blk_w), lambda bi, s, ptr: (bi, 0, s))
    width = past + blk_w

    keys = pl.pallas_call(
        functools.partial(_ds_score_kernel, g=g),
        grid_spec=pltpu.PrefetchScalarGridSpec(
            num_scalar_prefetch=1, grid=(b, ns + 1),
            in_specs=[per_b(qi_r), per_b(w_r), per_b(ki_new)] + [page_spec(j, IDX_DIM) for j in range(g)],
            out_specs=keys_spec),
        out_shape=jax.ShapeDtypeStruct((b, SUBLANES, width), i32),
        compiler_params=_cparams(("parallel", "arbitrary")),
        name="dsa_sample_scores",
    )(pt, qi_r.astype(bf16), w_r, ki_new, *([cache_idx_k] * g))

    rows = b * SUBLANES
    thr, jm = pl.pallas_call(
        functools.partial(_ds_thr_kernel, n_sel=n_sel, idx_bits=idx_bits, chunk_w=blk_w),
        out_shape=[jax.ShapeDtypeStruct((rows, LANES), i32)] * 2,
        compiler_params=pltpu.CompilerParams(vmem_limit_bytes=VMEM_LIMIT),
        name="dsa_sample_threshold",
    )(keys.reshape(rows, width))
    thr = thr.reshape(b, SUBLANES, LANES)
    jm = jm.reshape(b, SUBLANES, LANES)

    out = pl.pallas_call(
        functools.partial(_ds_attn_kernel, past=past, g=g),
        grid_spec=pltpu.PrefetchScalarGridSpec(
            num_scalar_prefetch=1, grid=(b, ns + 1),
            in_specs=[per_b(q_r), keys_spec, per_b(thr), per_b(jm), per_b(k_new), per_b(v_new)]
            + [page_spec(j, LANES) for j in range(g)] * 2,
            out_specs=per_b(q_r),
            scratch_shapes=[pltpu.VMEM((N_HEADS, SUBLANES, 1), f32), pltpu.VMEM((N_HEADS, SUBLANES, 1), f32),
                            pltpu.VMEM((N_HEADS, SUBLANES, LANES), f32)]),
        out_shape=jax.ShapeDtypeStruct((b, N_HEADS * SUBLANES, LANES), f32),
        compiler_params=_cparams(("parallel", "arbitrary")),
        name="dsa_sample_attention",
    )(pt, q_r, keys, thr, jm, k_new, v_new, *([ck] * g), *([cv] * g))
    out = out.reshape(b, N_HEADS, SUBLANES, N_KV_HEADS, HEAD_DIM)[:, :, :t]
    out = jnp.concatenate([out[:, :GROUP, :, 0], out[:, GROUP:, :, 1]], axis=1)
    return out.transpose(0, 2, 1, 3).reshape(b, t, A_WIDTH)


def _mid_kernel(x_ref, oa_ref, ob_ref, ga_ref, gb_ref, wa_ref, wb_ref, wo_ref, gf_ref, wrh_ref, wrl_ref, br_ref,
                x1_ref, h2_ref, te_ref, tg_ref):
    a = jnp.dot(oa_ref[...].astype(bf16), wa_ref[...], preferred_element_type=f32)
    b = jnp.dot(ob_ref[...].astype(bf16), wb_ref[...], preferred_element_type=f32)
    merged = ga_ref[...] * a + gb_ref[...] * b
    x1 = x_ref[...] + jnp.dot(merged.astype(bf16), wo_ref[...], preferred_element_type=f32)
    x1_ref[...] = x1
    h2 = _rms(x1, gf_ref[...])
    h2_ref[...] = h2
    hi = h2.astype(bf16)
    lo = (h2 - hi.astype(f32)).astype(bf16)
    lg = (jnp.dot(hi, wrh_ref[...], preferred_element_type=f32) + jnp.dot(lo, wrh_ref[...], preferred_element_type=f32)
          + jnp.dot(hi, wrl_ref[...], preferred_element_type=f32)) + br_ref[...]
    lane = lax.broadcasted_iota(i32, lg.shape, 1)
    lane_f = lane.astype(f32)
    vals, ids = [], []
    for _ in range(TOP_K):
        m = jnp.max(lg, axis=1, keepdims=True)
        idx = jnp.min(jnp.where(lg == m, lane_f, float(LANES)), axis=1, keepdims=True).astype(i32)
        vals.append(m)
        ids.append(idx)
        lg = jnp.where(lane == idx, -jnp.inf, lg)
    ex = [jnp.exp(v - vals[0]) for v in vals]
    tot = ex[0] + ex[1] + ex[2] + ex[3]
    te = jnp.zeros(lg.shape, i32)
    tg = jnp.zeros(lg.shape, f32)
    for j in range(TOP_K):
        te = jnp.where(lane == j, ids[j], te)
        tg = jnp.where(lane == j, ex[j] / tot, tg)
    te_ref[...] = te
    tg_ref[...] = tg


def mid(x, oa, ob, ga, gb, wa, wb, wo, g_ffn, wr_hi, wr_lo, br):
    n = x.shape[0]
    tm = min(256, n)
    assert n % tm == 0
    row = lambda w: pl.BlockSpec((tm, w), lambda i: (i, 0))
    full = lambda a: pl.BlockSpec(a.shape, lambda i: (0,) * a.ndim)
    return pl.pallas_call(
        _mid_kernel,
        grid=(n // tm,),
        in_specs=[row(D_MODEL), row(A_WIDTH), row(M_WIDTH), row(D_MODEL), row(D_MODEL),
                  full(wa), full(wb), full(wo), full(g_ffn), full(wr_hi), full(wr_lo), full(br)],
        out_specs=[row(D_MODEL), row(D_MODEL), row(LANES), row(LANES)],
        out_shape=[jax.ShapeDtypeStruct((n, D_MODEL), f32), jax.ShapeDtypeStruct((n, D_MODEL), f32),
                   jax.ShapeDtypeStruct((n, LANES), i32), jax.ShapeDtypeStruct((n, LANES), f32)],
        compiler_params=_cparams(("parallel",)),
        name="mid",
    )(x, oa, ob, ga, gb, wa, wb, wo, g_ffn, wr_hi, wr_lo, br)


def _rank_kernel(te_ref, dest_ref, meta_ref, cnt_ref, carry_ref, *, blk):
    ph = pl.program_id(0)
    i = pl.program_id(1)
    tm = te_ref.shape[0]
    lane = lax.broadcasted_iota(i32, (tm, LANES), 1)
    te = te_ref[...]
    oh = jnp.zeros((tm, LANES), f32)
    for j in range(TOP_K):
        oh = oh + jnp.where(lane == te[:, j:j + 1], 1.0, 0.0)
    tile_cnt = jnp.sum(oh, axis=0, keepdims=True)

    @pl.when((ph == 0) & (i == 0))
    def _():
        cnt_ref[...] = jnp.zeros(cnt_ref.shape, f32)

    @pl.when(ph == 0)
    def _():
        cnt_ref[...] += tile_cnt

    @pl.when((ph == 1) & (i == 0))
    def _():
        cnt = cnt_ref[...]
        padded = jnp.floor((cnt + (blk - 1)) / blk) * blk
        r = lax.broadcasted_iota(i32, (LANES, LANES), 0)
        c = lax.broadcasted_iota(i32, (LANES, LANES), 1)
        col = jnp.sum(jnp.where(r == c, padded, 0.0), axis=1, keepdims=True)
        start = jnp.sum(jnp.where(r < c, col, 0.0), axis=0, keepdims=True)
        carry_ref[...] = start
        pad_end = start + padded
        nbp = meta_ref.shape[0]
        jb = (lax.broadcasted_iota(i32, (nbp, LANES), 0) * blk).astype(f32)
        lane2 = lax.broadcasted_iota(i32, (nbp, LANES), 1)
        be = jnp.sum(jnp.where((pad_end <= jb) & (lane2 < N_EXPERTS), 1.0, 0.0), axis=1, keepdims=True)
        be = jnp.minimum(be, float(N_EXPERTS - 1))
        n_act = jnp.sum(jnp.where(lane2 == N_EXPERTS - 1, pad_end, 0.0), axis=1, keepdims=True) / blk
        meta_ref[...] = jnp.where(lane2 == 0, be, jnp.where(lane2 == 1, n_act, 0.0)).astype(i32)

    @pl.when(ph == 1)
    def _():
        r = lax.broadcasted_iota(i32, (tm, tm), 0)
        c = lax.broadcasted_iota(i32, (tm, tm), 1)
        before = jnp.where(c < r, 1.0, 0.0).astype(bf16)
        pos = carry_ref[...] + jnp.dot(before, oh.astype(bf16), preferred_element_type=f32)
        d = jnp.zeros((tm, LANES), f32)
        for j in range(TOP_K):
            dj = jnp.sum(jnp.where(lane == te[:, j:j + 1], pos, 0.0), axis=1, keepdims=True)
            d = jnp.where(lane == j, dj, d)
        dest_ref[...] = d.astype(i32)
        carry_ref[...] += tile_cnt


def rank(te, blk, n_blocks):
    n = te.shape[0]
    tm = min(256, n)
    assert n % tm == 0
    nbp = -(-n_blocks // SUBLANES) * SUBLANES
    return pl.pallas_call(
        functools.partial(_rank_kernel, blk=blk),
        grid=(2, n // tm),
        in_specs=[pl.BlockSpec((tm, LANES), lambda ph, i: (i, 0))],
        out_specs=[pl.BlockSpec((tm, LANES), lambda ph, i: (i * ph, 0)),
                   pl.BlockSpec((nbp, LANES), lambda ph, i: (0, 0))],
        out_shape=[jax.ShapeDtypeStruct((n, LANES), i32), jax.ShapeDtypeStruct((nbp, LANES), i32)],
        scratch_shapes=[pltpu.VMEM((1, LANES), f32), pltpu.VMEM((1, LANES), f32)],
        compiler_params=_cparams(("arbitrary", "arbitrary")),
        name="rank",
    )(te)


def _ffn_kernel(be_ref, nact_ref, rt_ref, x_hbm, wgu_ref, bgu_ref, wdn_ref, bdn_ref, y_ref,
                xbuf, sem, wgu_bf, wdn_bf):
    i = pl.program_id(0)
    n_act = nact_ref[0]
    blk = xbuf.shape[1]

    def row_copy(b, r, slot):
        tok = rt_ref[b * blk + r]
        return pltpu.make_async_copy(x_hbm.at[pl.ds(tok, 1)], xbuf.at[slot, pl.ds(r, 1)], sem.at[slot])

    def wait_slot(slot):
        pltpu.make_async_copy(xbuf.at[slot], xbuf.at[slot], sem.at[slot]).wait()

    @pl.when(i == 0)
    def _():
        def body(r, c):
            row_copy(0, r, 0).start()
            return c
        lax.fori_loop(0, blk, body, 0)

    @pl.when(i == n_act)
    def _():
        wait_slot(i % 2)

    @pl.when(i >= n_act)
    def _():
        y_ref[...] = jnp.zeros(y_ref.shape, f32)

    @pl.when(i < n_act)
    def _():
        slot = i % 2
        wait_slot(slot)

        @pl.when((i == 0) | (be_ref[i] != be_ref[jnp.maximum(i - 1, 0)]))
        def _():
            wgu_bf[...] = wgu_ref[0].astype(bf16)
            wdn_bf[...] = wdn_ref[0].astype(bf16)

        for r in range(blk):
            row_copy(i + 1, r, 1 - slot).start()
        x = xbuf[slot].astype(bf16)
        gu = jnp.dot(x, wgu_bf[...], preferred_element_type=f32) + bgu_ref[0]
        gate = jnp.minimum(gu[:, :EXPERT_FF], SWIGLU_LIMIT)
        up = jnp.clip(gu[:, EXPERT_FF:], -SWIGLU_LIMIT, SWIGLU_LIMIT)
        act = (up + 1.0) * (gate * _sigmoid(SWIGLU_ALPHA * gate))
        y_ref[...] = jnp.dot(act.astype(bf16), wdn_bf[...], preferred_element_type=f32) + bdn_ref[0]


def ffn(blk_e, n_act, row_tok, h2, w_gu, b_gu, w_dn, b_dn, blk, n_blocks):
    d = h2.shape[1]
    ff2 = w_gu.shape[2]
    return pl.pallas_call(
        _ffn_kernel,
        grid_spec=pltpu.PrefetchScalarGridSpec(
            num_scalar_prefetch=3, grid=(n_blocks,),
            in_specs=[pl.BlockSpec(memory_space=pl.ANY),
                      pl.BlockSpec((1, d, ff2), lambda i, be, na, rt: (be[i], 0, 0)),
                      pl.BlockSpec((1, 1, ff2), lambda i, be, na, rt: (be[i], 0, 0)),
                      pl.BlockSpec((1, ff2 // 2, d), lambda i, be, na, rt: (be[i], 0, 0)),
                      pl.BlockSpec((1, 1, d), lambda i, be, na, rt: (be[i], 0, 0))],
            out_specs=pl.BlockSpec((blk, d), lambda i, be, na, rt: (i, 0)),
            scratch_shapes=[pltpu.VMEM((2, blk, d), f32), pltpu.SemaphoreType.DMA((2,)),
                            pltpu.VMEM((d, ff2), bf16), pltpu.VMEM((ff2 // 2, d), bf16)]),
        out_shape=jax.ShapeDtypeStruct((n_blocks * blk, d), f32),
        compiler_params=_cparams(("arbitrary",)),
        name="ffn",
    )(blk_e, n_act, row_tok, h2, w_gu, b_gu.reshape(b_gu.shape[0], 1, ff2), w_dn, b_dn.reshape(b_dn.shape[0], 1, d))


def _out_kernel(dest_ref, x1_ref, tg_ref, p_ref, ys_hbm, gp_ref, wg_ref, wp_ref, gfin_ref, y_ref, buf, sem):
    i = pl.program_id(0)
    n = pl.num_programs(0)
    tm = x1_ref.shape[0]

    def row_copy(t, r, j, slot):
        d = dest_ref[(t * tm + r) * TOP_K + j]
        return pltpu.make_async_copy(ys_hbm.at[pl.ds(d, 1)], buf.at[slot, j, pl.ds(r, 1)], sem.at[slot])

    def wait_slot(slot):
        pltpu.make_async_copy(buf.at[slot], buf.at[slot], sem.at[slot]).wait()

    @pl.when(i == 0)
    def _():
        def body(r, c):
            for j in range(TOP_K):
                row_copy(0, r, j, 0).start()
            return c
        lax.fori_loop(0, tm, body, 0)

    slot = i % 2
    wait_slot(slot)
    nxt = jnp.minimum(i + 1, n - 1)
    for r in range(tm):
        for j in range(TOP_K):
            row_copy(nxt, r, j, 1 - slot).start()
    tg = tg_ref[...]
    x2 = x1_ref[...]
    for j in range(TOP_K):
        x2 = x2 + tg[:, j:j + 1] * buf[slot, j]
    hn = _rms(x2, gp_ref[...]).astype(bf16)
    gate = _sigmoid(jnp.dot(hn, wg_ref[...], preferred_element_type=f32))
    x3 = x2 + gate * jnp.dot(p_ref[...].astype(bf16), wp_ref[...], preferred_element_type=f32)
    y_ref[...] = _rms(x3, gfin_ref[...])

    @pl.when(i == n - 1)
    def _():
        wait_slot(1 - slot)


def out_stage(dest_flat, x1, tg, p, ys, g_ple, wg, wp, g_final):
    n, d = x1.shape
    tm = min(OUT_TM, n)
    assert n % tm == 0
    row = lambda w: pl.BlockSpec((tm, w), lambda i, ds: (i, 0))
    full = lambda a: pl.BlockSpec(a.shape, lambda i, ds: (0,) * a.ndim)
    return pl.pallas_call(
        _out_kernel,
        grid_spec=pltpu.PrefetchScalarGridSpec(
            num_scalar_prefetch=1, grid=(n // tm,),
            in_specs=[row(d), row(LANES), row(p.shape[1]), pl.BlockSpec(memory_space=pl.ANY),
                      full(g_ple), full(wg), full(wp), full(g_final)],
            out_specs=row(d),
            scratch_shapes=[pltpu.VMEM((2, TOP_K, tm, d), f32), pltpu.SemaphoreType.DMA((2,))]),
        out_shape=jax.ShapeDtypeStruct((n, d), f32),
        compiler_params=_cparams(("arbitrary",)),
        name="out",
    )(dest_flat, x1, tg, p, ys, g_ple, wg, wp, g_final)


def tail(x, oa, ob, ga, gb, p, w):
    n = x.shape[0]
    x1, h2, te, tg = mid(x, oa, ob, ga, gb, w["wa"], w["wb"], w["wo"], w["g_ffn"], w["wr_hi"], w["wr_lo"], w["br"])
    blk = FFN_BLK
    n_blocks = -(-(n * TOP_K) // blk) + N_EXPERTS
    dest, meta = rank(te, blk, n_blocks)
    dest_flat = dest[:, :TOP_K].reshape(-1)
    blk_e = meta[:n_blocks, 0]
    n_act = meta[0:1, 1]
    tok = jnp.repeat(jnp.arange(n, dtype=i32), TOP_K)
    row_tok = jnp.zeros((n_blocks * blk,), i32).at[dest_flat].set(tok)
    ys = ffn(blk_e, n_act, row_tok, h2, w["w_gu"], w["b_gu"], w["w_dn"], w["b_dn"], blk, n_blocks)
    return out_stage(dest_flat, x1, tg, p, ys, w["g_ple"], w["wg"], w["wp"], w["g_final"])


def _prep_weights(g_mix, w_in, b_gate, g_mnorm, w_up_a, w_up_b, w_out, g_ffn, w_router, b_router,
                  w_gu, b_gu, w_dn, b_dn, g_ple, w_ple_gate, w_ple_proj, g_final):
    wr = jnp.pad(w_router.astype(f32), ((0, 0), (0, LANES - N_EXPERTS)))
    wr_hi = wr.astype(bf16)
    wr_lo = (wr - wr_hi.astype(f32)).astype(bf16)
    br = jnp.full((1, LANES), -jnp.inf, f32).at[0, :N_EXPERTS].set(b_router.astype(f32))
    bias = jnp.zeros((1, LANES), f32).at[0, SM_IG:SM_IG + 2 * M_HEADS].set(b_gate.astype(f32))
    return dict(g_mix=g_mix.reshape(1, -1), w_in=_pack_w_in(w_in), bias=bias, g_mnorm=g_mnorm,
                wa=w_up_a.astype(bf16), wb=w_up_b.astype(bf16), wo=w_out.astype(bf16), g_ffn=g_ffn.reshape(1, -1),
                wr_hi=wr_hi, wr_lo=wr_lo, br=br, w_gu=w_gu, b_gu=b_gu, w_dn=w_dn, b_dn=b_dn,
                g_ple=g_ple.reshape(1, -1), wg=w_ple_gate.astype(bf16), wp=w_ple_proj.astype(bf16),
                g_final=g_final.reshape(1, -1))


def _layer(x, p, pos, attn_fn, c0, n0, m0, w):
    b, t = x.shape[:2]
    n = b * t
    cos, sin = _rope_tables(pos)
    cos = jnp.tile(cos, (b, 1))
    sin = jnp.tile(sin, (b, 1))
    x2d = x.reshape(n, D_MODEL)
    q, k, v, qi, sm, mq, mk, mv, og, ga, gb = in_proj(x2d, cos, sin, w["g_mix"], w["w_in"], w["bias"])
    r3 = lambda a: a.reshape(b, t, a.shape[-1])
    o_a = attn_fn(r3(q), r3(qi), r3(sm), r3(k), r3(v))
    o_b, c, nn, m = mlstm_branch(r3(mq), r3(mk), r3(mv), r3(og), r3(sm), c0, n0, m0, w["g_mnorm"])
    y = tail(x2d, o_a.reshape(n, A_WIDTH), o_b.reshape(n, M_WIDTH), ga, gb, p.reshape(n, PLE_DIM), w)
    state = (k.reshape(b, t, N_KV_HEADS, HEAD_DIM), v.reshape(b, t, N_KV_HEADS, HEAD_DIM),
             sm[:, :IDX_DIM].reshape(b, t, IDX_DIM), c, nn, m)
    return y.reshape(b, t, D_MODEL), state


def kernel(x_prompt, x_sample, cache_k, cache_v, cache_idx_k, state_C, state_n, state_m, page_table,
           p_prompt, p_sample, g_mix, w_in, b_gate, g_mnorm, w_up_a, w_up_b, w_out, g_ffn,
           w_router, b_router, w_gu, b_gu, w_dn, b_dn, g_ple, w_ple_gate, w_ple_proj, g_final):
    assert x_prompt.shape[-1] == D_MODEL and w_in.shape[0] == 1, "single-layer model of width D_MODEL"
    bp, tp = x_prompt.shape[:2]
    ts = x_sample.shape[1]
    past = page_table.shape[1] * PAGE_SIZE
    w = _prep_weights(g_mix[0], w_in[0], b_gate[0], g_mnorm[0], w_up_a[0], w_up_b[0], w_out[0], g_ffn[0],
                      w_router[0], b_router[0], w_gu[0], b_gu[0], w_dn[0], b_dn[0], g_ple[0],
                      w_ple_gate[0], w_ple_proj[0], g_final)
    zeros = lambda *s: jnp.zeros(s, f32)
    yp, sp = _layer(x_prompt, p_prompt[0], jnp.arange(tp, dtype=i32), dsa_prompt_branch,
                    zeros(bp, M_HEADS, M_V, M_QK), zeros(bp, M_HEADS, M_QK), zeros(bp, M_HEADS), w)
    attn_s = functools.partial(dsa_sample, cache_k=cache_k[0], cache_v=cache_v[0], cache_idx_k=cache_idx_k[0],
                               page_table=page_table)
    ys, ss = _layer(x_sample, p_sample[0], past + jnp.arange(ts, dtype=i32), attn_s,
                    state_C[0], state_n[0], state_m[0], w)
    return (yp, ys) + tuple(s[None] for s in sp) + tuple(s[None] for s in ss)
```

```python
import functools

import jax
import jax.numpy as jnp
from jax import lax
from jax.experimental import pallas as pl
from jax.experimental.pallas import tpu as pltpu

f32 = jnp.float32
bf16 = jnp.bfloat16
i32 = jnp.int32

D_MODEL = 1024
PAGE_SIZE = 128
N_HEADS = 8
N_KV_HEADS = 2
HEAD_DIM = 64
GROUP = N_HEADS // N_KV_HEADS
IDX_HEADS = 4
IDX_DIM = 64
TOPK_MAX = 256
ROPE_THETA = 10000.0
M_HEADS = 4
M_QK = 64
M_V = 128
N_EXPERTS = 32
TOP_K = 4
EXPERT_FF = D_MODEL
SWIGLU_LIMIT = 7.0
SWIGLU_ALPHA = 1.702
PLE_DIM = 256
EPS = 1e-6
A_WIDTH = N_HEADS * HEAD_DIM
M_WIDTH = M_HEADS * M_V

LANES = 128
SUBLANES = 8
VMEM_LIMIT = 56 * 1024 * 1024

M_CHUNK = 128
TQ = 128
KC = 1024
PAGES_PER_STEP = 16
FFN_BLK = 256
W_SPLIT = 4
OUT_TM = 128

INT_MIN = -2147483648
INT_MAX = 2147483647
NEG_BIG = -1e30
LOG2E = 1.4426950408889634
ROW_SLICE = 16

SM_KI = 0
SM_WI = IDX_DIM
SM_IG = SM_WI + IDX_HEADS
SM_LF = SM_IG + M_HEADS


def _cparams(sem):
    return pltpu.CompilerParams(dimension_semantics=sem, vmem_limit_bytes=VMEM_LIMIT)


def _rms(x, g):
    return x * lax.rsqrt(jnp.mean(x * x, axis=-1, keepdims=True) + EPS) * g


def _sigmoid(x):
    return 1.0 / (1.0 + jnp.exp(-x))


def _log_sigmoid(x):
    return jnp.minimum(x, 0.0) - jnp.log1p(jnp.exp(-jnp.abs(x)))


def _sortable(x):
    bits = lax.bitcast_convert_type(x, i32)
    key = bits ^ ((bits >> 31) & INT_MAX)
    return jnp.where(x == 0.0, 0, key)


_G_Q = (0, 512)
_G_K = (512, 640)
_G_V = (640, 768)
_G_QI = (768, 1024)
_G_SM = (1024, 1152)
_G_MQ = (1152, 1408)
_G_MK = (1408, 1664)
_G_MV = (1664, 2176)
_G_MO = (2176, 2688)
_G_GA = (2688, 3712)
_G_GB = (3712, 4736)
_W_COLS = 4736


def _pack_w_in(w_in):
    o = [0]
    for s in (A_WIDTH, 128, 128, 256, 64, 4, 256, 256, 512, 4, 4, 512, 1024, 1024):
        o.append(o[-1] + s)
    aq, ak, av, iq, ik, iw, mq, mk, mv, mi, mf, mo, ga, gb = [w_in[:, o[i]:o[i + 1]] for i in range(14)]
    pad = jnp.zeros((w_in.shape[0], LANES - (IDX_DIM + IDX_HEADS + 2 * M_HEADS)), w_in.dtype)
    small = jnp.concatenate([ik, iw, mi, mf, pad], axis=1)
    w = jnp.concatenate([aq, ak, av, iq, small, mq, mk, mv, mo, ga, gb], axis=1)
    return w.astype(bf16)


def _in_kernel(x_ref, cos_ref, sin_ref, g_ref, w_ref, bias_ref,
               q_ref, k_ref, v_ref, qi_ref, sm_ref, mq_ref, mk_ref, mv_ref, og_ref, ga_ref, gb_ref):
    x = x_ref[...]
    hb = _rms(x, g_ref[...]).astype(bf16)
    cos = cos_ref[...]
    sin = sin_ref[...]
    tm = x.shape[0]
    lane = lax.broadcasted_iota(i32, (tm, LANES), 1)
    first_half = (lane % HEAD_DIM) < (HEAD_DIM // 2)

    def rope(z):
        rot = jnp.where(first_half, pltpu.roll(z, LANES - HEAD_DIM // 2, 1), pltpu.roll(z, HEAD_DIM // 2, 1))
        return z * cos + rot * sin

    def proj(grp):
        return jnp.dot(hb, w_ref[:, grp[0]:grp[1]], preferred_element_type=f32)

    z = proj(_G_Q)
    for j in range(4):
        q_ref[:, j * LANES:(j + 1) * LANES] = rope(z[:, j * LANES:(j + 1) * LANES])
    k_ref[...] = rope(proj(_G_K))
    v_ref[...] = proj(_G_V)
    z = proj(_G_QI)
    for j in range(2):
        qi_ref[:, j * LANES:(j + 1) * LANES] = rope(z[:, j * LANES:(j + 1) * LANES])
    z = proj(_G_SM)
    zb = z + bias_ref[...]
    sm = jnp.where(lane < SM_WI, rope(z),
                   jnp.where(lane < SM_IG, z * (IDX_HEADS ** -0.5 * IDX_DIM ** -0.5),
                             jnp.where(lane < SM_LF, zb,
                                       jnp.where(lane < SM_LF + M_HEADS, _log_sigmoid(zb), 0.0))))
    sm_ref[...] = sm
    mq_ref[...] = proj(_G_MQ)
    mk_ref[...] = proj(_G_MK) * (M_QK ** -0.5)
    mv_ref[...] = proj(_G_MV)
    og_ref[...] = _sigmoid(proj(_G_MO))
    ga_ref[...] = _sigmoid(proj(_G_GA))
    gb_ref[...] = _sigmoid(proj(_G_GB))


def _rope_tables(pos):
    half = HEAD_DIM // 2
    inv = ROPE_THETA ** (-jnp.arange(half, dtype=f32) / half)
    ang = pos.astype(f32)[:, None] * inv[None, :]
    cos = jnp.cos(ang)
    sin = jnp.sin(ang)
    cos128 = jnp.tile(cos, (1, 4))
    sin128 = jnp.tile(jnp.concatenate([-sin, sin], axis=1), (1, 2))
    return cos128, sin128


def in_proj(x2d, cos128, sin128, g_mix, w_packed, bias128):
    n = x2d.shape[0]
    tm = min(256, n)
    assert n % tm == 0
    row = lambda w: pl.BlockSpec((tm, w), lambda i: (i, 0))
    full = lambda a: pl.BlockSpec(a.shape, lambda i: (0,) * a.ndim)
    widths = (512, 128, 128, 256, 128, 256, 256, 512, 512, 1024, 1024)
    return pl.pallas_call(
        _in_kernel,
        grid=(n // tm,),
        in_specs=[row(D_MODEL), row(LANES), row(LANES), full(g_mix), full(w_packed), full(bias128)],
        out_specs=[row(w) for w in widths],
        out_shape=[jax.ShapeDtypeStruct((n, w), f32) for w in widths],
        compiler_params=_cparams(("parallel",)),
        name="in_proj",
    )(x2d, cos128, sin128, g_mix, w_packed, bias128)


def _mlstm_kernel(mq_ref, mk_ref, mv_ref, og_ref, sm_ref, gr_ref, c0_ref, n0_ref, m0_ref, gn_ref,
                  ob_ref, c_ref, n_ref, m_ref):
    ci = pl.program_id(1)
    L = mq_ref.shape[1]

    @pl.when(ci == 0)
    def _():
        c_ref[...] = c0_ref[...]
        n_ref[...] = n0_ref[...]
        m_ref[...] = m0_ref[...]

    sm = sm_ref[0]
    gr = gr_ref[0]
    row = lax.broadcasted_iota(i32, (L, L), 0)
    col = lax.broadcasted_iota(i32, (L, L), 1)
    tril = row >= col
    for hd in range(M_HEADS):
        q = mq_ref[0, :, hd * M_QK:(hd + 1) * M_QK]
        k = mk_ref[0, :, hd * M_QK:(hd + 1) * M_QK]
        v = mv_ref[0, :, hd * M_V:(hd + 1) * M_V]
        ig_r = gr[hd:hd + 1, :]
        lf_r = gr[M_HEADS + hd:M_HEADS + hd + 1, :]
        ig_c = sm[:, SM_IG + hd:SM_IG + hd + 1]
        lf_c = sm[:, SM_LF + hd:SM_LF + hd + 1]
        C = c_ref[0, hd]
        nrow = n_ref[0, hd]
        m_prev = m_ref[0, hd]
        b_c = jnp.sum(jnp.where(tril, lf_r, 0.0), axis=1, keepdims=True)
        b_r = jnp.sum(jnp.where(tril, 0.0, lf_c) + jnp.where(row == col, lf_c, 0.0), axis=0, keepdims=True)
        dmat = jnp.where(tril, b_c - b_r + ig_r, -jnp.inf)
        inter = b_c + m_prev
        m_t = jnp.maximum(inter, jnp.max(dmat, axis=1, keepdims=True))
        qb = q.astype(bf16)
        kb = k.astype(bf16)
        qk = lax.dot_general(qb, kb, (((1,), (1,)), ((), ())), preferred_element_type=f32)
        s = qk * jnp.exp(dmat - m_t)
        w_inter = jnp.exp(inter - m_t)
        qc = lax.dot_general(qb, C.astype(bf16), (((1,), (1,)), ((), ())), preferred_element_type=f32)
        num = jnp.dot(s.astype(bf16), v.astype(bf16), preferred_element_type=f32) + w_inter * qc
        den = jnp.sum(s, axis=1, keepdims=True) + w_inter * jnp.sum(q * nrow, axis=1, keepdims=True)
        h = num / jnp.maximum(jnp.abs(den), jnp.exp(-m_t))
        b_last = b_c[L - 1:L, :]
        g_c = b_last - b_c + ig_c
        m_new = jnp.maximum(b_last + m_prev, jnp.max(g_c, axis=0, keepdims=True))
        w_k = jnp.exp(g_c - m_new)
        decay = jnp.exp(b_last + m_prev - m_new)
        wv = (w_k * v).astype(bf16)
        c_ref[0, hd] = decay * C + lax.dot_general(wv, kb, (((0,), (0,)), ((), ())), preferred_element_type=f32)
        n_ref[0, hd] = decay * nrow + jnp.sum(w_k * k, axis=0, keepdims=True)
        m_ref[0, hd] = m_new
        gn = gn_ref[:, hd * M_V:(hd + 1) * M_V]
        ob_ref[0, :, hd * M_V:(hd + 1) * M_V] = og_ref[0, :, hd * M_V:(hd + 1) * M_V] * _rms(h, gn)


def mlstm(mq, mk, mv, og, sm, grow, c0, n0, m0, g_mnorm):
    b, t = mq.shape[:2]
    L = M_CHUNK
    assert t % L == 0
    tok = lambda w: pl.BlockSpec((1, L, w), lambda bi, ci: (bi, ci, 0))
    st = lambda a: pl.BlockSpec((1,) + a.shape[1:], lambda bi, ci: (bi,) + (0,) * (a.ndim - 1))
    return pl.pallas_call(
        _mlstm_kernel,
        grid=(b, t // L),
        in_specs=[tok(256), tok(256), tok(512), tok(512), tok(LANES),
                  pl.BlockSpec((1, 2 * M_HEADS, L), lambda bi, ci: (bi, 0, ci)),
                  st(c0), st(n0), st(m0), pl.BlockSpec(g_mnorm.shape, lambda bi, ci: (0, 0))],
        out_specs=[tok(512), st(c0), st(n0), st(m0)],
        out_shape=[jax.ShapeDtypeStruct((b, t, M_WIDTH), f32), jax.ShapeDtypeStruct(c0.shape, f32),
                   jax.ShapeDtypeStruct(n0.shape, f32), jax.ShapeDtypeStruct(m0.shape, f32)],
        compiler_params=_cparams(("parallel", "arbitrary")),
        name="mlstm",
    )(mq, mk, mv, og, sm, grow, c0, n0, m0, g_mnorm)


def mlstm_branch(mq, mk, mv, og, sm, c0, n0, m0, g_mnorm):
    b, t = mq.shape[:2]
    tp = -(-t // M_CHUNK) * M_CHUNK
    if tp != t:
        pad = lambda a: jnp.pad(a, ((0, 0), (0, tp - t), (0, 0)))
        mq, mk, mv, og = pad(mq), pad(mk), pad(mv), pad(og)
        sm_pad = jnp.zeros((b, tp - t, LANES), f32).at[:, :, SM_IG:SM_IG + M_HEADS].set(NEG_BIG)
        sm = jnp.concatenate([sm, sm_pad], axis=1)
    grow = sm[:, :, SM_IG:SM_IG + 2 * M_HEADS].transpose(0, 2, 1)
    ob, c, n, m = mlstm(mq, mk, mv, og, sm, grow, c0.astype(f32), n0.astype(f32).reshape(b, M_HEADS, 1, M_QK),
                        m0.astype(f32).reshape(b, M_HEADS, 1, 1), g_mnorm.reshape(1, M_WIDTH))
    return ob[:, :t], c, n.reshape(b, M_HEADS, M_QK), m.reshape(b, M_HEADS)


def _select_threshold(get_chunk, n_chunks, chunk_w, rows, n_sel, dynamic):
    def count(pred):
        def body(c, acc):
            hit = jnp.where(pred(get_chunk(c)), 1.0, 0.0)
            for j in range(chunk_w // LANES):
                acc = acc + hit[:, j * LANES:(j + 1) * LANES]
            return acc
        acc0 = jnp.zeros((rows, LANES), f32)
        if dynamic:
            acc = lax.fori_loop(0, n_chunks, body, acc0)
        else:
            acc = acc0
            for c in range(n_chunks):
                acc = body(c, acc)
        return jnp.sum(acc, axis=1, keepdims=True)

    kf = float(n_sel)

    total = jnp.zeros((rows, 1), f32) + (n_chunks * chunk_w).astype(f32) if dynamic else \
        jnp.full((rows, 1), float(n_chunks * chunk_w), f32)
    c_nonneg = count(lambda keys: keys >= 0)
    c_pos = count(lambda keys: keys >= 1)
    take0 = c_nonneg >= kf
    zero_tie = take0 & (c_pos < kf)

    def unsettled(st):
        it, _, cnt = st
        return (it < 32) & (jnp.max(jnp.where(zero_tie, 0.0, jnp.abs(cnt - kf))) > 0.0)

    def bit_step(st):
        it, thr_u, cnt = st
        cand_u = thr_u | lax.shift_left(jnp.int32(1), 31 - it)
        cand_s = cand_u ^ INT_MIN
        c = count(lambda keys: keys >= cand_s)
        take = c >= kf
        return it + 1, jnp.where(take, cand_u, thr_u), jnp.where(take, c, cnt)

    state0 = (jnp.int32(1), jnp.where(take0, INT_MIN, 0).astype(i32), jnp.where(take0, c_nonneg, total))
    _, thr_u, _ = lax.while_loop(unsettled, bit_step, state0)
    thr = jnp.maximum(thr_u ^ INT_MIN, INT_MIN + 1)
    need = kf - count(lambda keys: keys > thr)
    return thr, need


def _tie_prefix_matrix():
    r = lax.broadcasted_iota(i32, (LANES, LANES), 0)
    c = lax.broadcasted_iota(i32, (LANES, LANES), 1)
    return jnp.where(r <= c, 1.0, 0.0).astype(bf16)


def _selected_block(keys, thr, need, seen, tri):
    eq = keys == thr
    rank = seen + jnp.dot(jnp.where(eq, 1.0, 0.0).astype(bf16), tri, preferred_element_type=f32)
    sel = (keys > thr) | (eq & (rank <= need))
    return sel, rank[:, LANES - 1:LANES]


def _dsa_prompt_kernel(q_ref, qi_ref, sm_ref, kit_ref, kt_ref, vd_ref, o_ref,
                       keys_ref, bias_ref, lg_ref, p_ref, m_ref, acc_ref, *, n_sel):
    qb = pl.program_id(1)
    tq = q_ref.shape[1]
    kc = kit_ref.shape[3]
    n_chunks = (qb * tq + tq - 1) // kc + 1
    lane = lax.broadcasted_iota(i32, (tq, LANES), 1)
    lo = lane < HEAD_DIM
    t_col = qb * tq + lax.broadcasted_iota(i32, (tq, 1), 0)
    sm = sm_ref[0]

    qi = qi_ref[0]
    qi_h = []
    for h in range(IDX_HEADS):
        blk = qi[:, (h // 2) * LANES:(h // 2 + 1) * LANES]
        qi_h.append(jnp.where(lo if h % 2 == 0 else ~lo, blk, 0.0).astype(bf16))
    w_h = [sm[:, SM_WI + h:SM_WI + h + 1] for h in range(IDX_HEADS)]

    def score_chunk(c, carry):
        kt = kit_ref[0, c]
        sc = jnp.zeros((tq, kc), f32)
        for h in range(IDX_HEADS):
            s = jnp.dot(qi_h[h], kt, preferred_element_type=f32)
            sc = sc + w_h[h] * jnp.maximum(s, 0.0)
        idx = c * kc + lax.broadcasted_iota(i32, (tq, kc), 1)
        keys_ref[c] = jnp.where(idx <= t_col, _sortable(sc), INT_MIN)
        return carry

    lax.fori_loop(0, n_chunks, score_chunk, 0)

    thr, need = _select_threshold(lambda c: keys_ref[c], n_chunks, kc, tq, n_sel, True)
    tri = _tie_prefix_matrix()

    q = q_ref[0] * (HEAD_DIM ** -0.5 * LOG2E)
    q_g = []
    for g in range(N_KV_HEADS):
        parts = []
        for j in range(GROUP):
            h = g * GROUP + j
            blk = q[:, (h // 2) * LANES:(h // 2 + 1) * LANES]
            parts.append(jnp.where(lo if h % 2 == 0 else ~lo, blk, 0.0).astype(bf16))
        q_g.append(jnp.concatenate(parts, axis=0))
    m_ref[...] = jnp.full(m_ref.shape, NEG_BIG, f32)
    acc_ref[...] = jnp.zeros(acc_ref.shape, f32)
    rs = min(ROW_SLICE, tq)

    def attend_chunk(c, seen):
        for j in range(kc // LANES):
            sel, seen = _selected_block(keys_ref[c, :, j * LANES:(j + 1) * LANES], thr, need, seen, tri)
            bias_ref[:, j * LANES:(j + 1) * LANES] = jnp.where(sel, 0.0, NEG_BIG)
        for g in range(N_KV_HEADS):
            lg_ref[...] = jnp.dot(q_g[g], kt_ref[0, g, c], preferred_element_type=f32)
            for r0 in range(0, GROUP * tq, rs):
                x = lg_ref[r0:r0 + rs] + bias_ref[r0 % tq:r0 % tq + rs]
                m_old = m_ref[g, r0:r0 + rs]
                m_new = jnp.maximum(m_old, jnp.max(x, axis=-1, keepdims=True))
                p_ref[r0:r0 + rs] = jnp.exp2(x - m_new).astype(bf16)
                acc_ref[g, r0:r0 + rs] = jnp.exp2(m_old - m_new) * acc_ref[g, r0:r0 + rs]
                m_ref[g, r0:r0 + rs] = m_new
            acc_ref[g] += jnp.dot(p_ref[...], vd_ref[0, g, c], preferred_element_type=f32)
        return seen

    lax.fori_loop(0, n_chunks, attend_chunk, jnp.zeros((tq, 1), f32))

    for g in range(N_KV_HEADS):
        acc = acc_ref[g]
        out = acc / acc[:, HEAD_DIM:HEAD_DIM + 1]
        for jp in range(GROUP // 2):
            even = out[(2 * jp) * tq:(2 * jp + 1) * tq]
            odd = pltpu.roll(out[(2 * jp + 1) * tq:(2 * jp + 2) * tq], HEAD_DIM, 1)
            o_ref[0, :, (g * 2 + jp) * LANES:(g * 2 + jp + 1) * LANES] = jnp.where(lo, even, odd)


def dsa_prompt(q, qi, sm, kit2, kt2, vd):
    b, t = q.shape[:2]
    nc, kc = kit2.shape[1], kit2.shape[3]
    tq = min(TQ, t)
    n_sel = min(TOPK_MAX, t // 4)
    tok = lambda w: pl.BlockSpec((1, tq, w), lambda bi, qb: (bi, qb, 0))
    return pl.pallas_call(
        functools.partial(_dsa_prompt_kernel, n_sel=n_sel),
        grid=(b, t // tq),
        in_specs=[tok(A_WIDTH), tok(IDX_HEADS * IDX_DIM), tok(LANES),
                  pl.BlockSpec((1, nc, LANES, kc), lambda bi, qb: (bi, 0, 0, 0)),
                  pl.BlockSpec((1, N_KV_HEADS, nc, LANES, kc), lambda bi, qb: (bi, 0, 0, 0, 0)),
                  pl.BlockSpec((1, N_KV_HEADS, nc, kc, LANES), lambda bi, qb: (bi, 0, 0, 0, 0))],
        out_specs=tok(A_WIDTH),
        out_shape=jax.ShapeDtypeStruct((b, t, A_WIDTH), f32),
        scratch_shapes=[pltpu.VMEM((nc, tq, kc), i32),
                        pltpu.VMEM((tq, kc), f32),
                        pltpu.VMEM((GROUP * tq, kc), f32),
                        pltpu.VMEM((GROUP * tq, kc), bf16),
                        pltpu.VMEM((N_KV_HEADS, GROUP * tq, 1), f32),
                        pltpu.VMEM((N_KV_HEADS, GROUP * tq, LANES), f32)],
        compiler_params=_cparams(("parallel", "arbitrary")),
        name="dsa_prompt",
    )(q, qi, sm, kit2, kt2, vd)


def dsa_prompt_branch(q, qi, sm, k, v):
    b, t = q.shape[:2]
    kc = min(KC, t)
    nc = t // kc
    kit = sm[:, :, :IDX_DIM].astype(bf16).reshape(b, nc, kc, IDX_DIM).transpose(0, 1, 3, 2)
    kit2 = jnp.concatenate([kit, kit], axis=2)
    kt = k.astype(bf16).reshape(b, nc, kc, N_KV_HEADS, HEAD_DIM).transpose(0, 3, 1, 4, 2)
    kt2 = jnp.concatenate([kt, kt], axis=3)
    vb = v.astype(bf16).reshape(b, nc, kc, N_KV_HEADS, HEAD_DIM).transpose(0, 3, 1, 2, 4)
    vd = jnp.concatenate([vb, jnp.ones_like(vb)], axis=4)
    return dsa_prompt(q, qi, sm, kit2, kt2, vd)


def _idx_scores(qi, w, ktpage):
    s = jnp.dot(qi, ktpage.astype(bf16), preferred_element_type=f32)
    r = (w * jnp.maximum(s, 0.0)).reshape(IDX_HEADS, SUBLANES, s.shape[1])
    sc = r[0]
    for h in range(1, IDX_HEADS):
        sc = sc + r[h]
    return sc


def _ds_score_kernel(pt_ref, qi_ref, w_ref, kin_ref, *rest, g):
    pages = rest[:g]
    keys_ref = rest[g]
    s = pl.program_id(1)
    last = pl.num_programs(1) - 1
    qi = qi_ref[0]
    w = w_ref[0]

    @pl.when(s < last)
    def _():
        for j in range(g):
            keys_ref[0, :, j * PAGE_SIZE:(j + 1) * PAGE_SIZE] = _sortable(_idx_scores(qi, w, pages[j][0]))

    @pl.when(s == last)
    def _():
        keys_ref[0] = jnp.full(keys_ref.shape[1:], INT_MIN, i32)
        sc = _idx_scores(qi, w, kin_ref[0])
        tok = lax.broadcasted_iota(i32, sc.shape, 0)
        j = lax.broadcasted_iota(i32, sc.shape, 1)
        keys_ref[0, :, 0:PAGE_SIZE] = jnp.where(j <= tok, _sortable(sc), INT_MIN)


def _ds_thr_kernel(keys_ref, thr_ref, need_ref, *, n_sel, chunk_w):
    rows, width = keys_ref.shape

    def get_chunk(c):
        return keys_ref[:, c * chunk_w:(c + 1) * chunk_w]

    thr, need = _select_threshold(get_chunk, width // chunk_w, chunk_w, rows, n_sel, False)
    thr_ref[...] = jnp.broadcast_to(thr, thr_ref.shape)
    need_ref[...] = jnp.broadcast_to(need, need_ref.shape)


def _ds_attn_kernel(pt_ref, q_ref, keys_ref, thr_ref, need_ref, knew_ref, vnew_ref, *rest, g):
    kpages = rest[:g]
    vpages = rest[g:2 * g]
    o_ref, m_ref, l_ref, acc_ref, seen_ref = rest[2 * g:]
    s = pl.program_id(1)
    last = pl.num_programs(1) - 1
    q = q_ref[0]
    thr = thr_ref[0][:, 0:1]
    need = need_ref[0][:, 0:1]
    tri = _tie_prefix_matrix()

    def process(keys, kps, vps):
        n = len(kps)
        seen = seen_ref[...]
        bias = []
        for j in range(n):
            sel, seen = _selected_block(keys[:, j * PAGE_SIZE:(j + 1) * PAGE_SIZE], thr, need, seen, tri)
            bias.append(jnp.where(sel, 0.0, NEG_BIG))
        seen_ref[...] = seen
        bias = jnp.concatenate(bias, axis=1) if n > 1 else bias[0]
        lg = jnp.concatenate(
            [jnp.dot(q, kp.astype(bf16), preferred_element_type=f32) for kp in kps],
            axis=1)
        lg = lg.reshape(N_HEADS, SUBLANES, n * PAGE_SIZE) + bias[None]
        m_old = m_ref[...]
        m_new = jnp.maximum(m_old, jnp.max(lg, axis=-1, keepdims=True))
        p = jnp.exp(lg - m_new)
        alpha = jnp.exp(m_old - m_new)
        l_ref[...] = alpha * l_ref[...] + jnp.sum(p, axis=-1, keepdims=True)
        pb = p.reshape(N_HEADS * SUBLANES, n * PAGE_SIZE).astype(bf16)
        nt = (((1,), (1,)), ((), ()))
        pv = lax.dot_general(pb[:, 0:PAGE_SIZE], vps[0].astype(bf16), nt, preferred_element_type=f32)
        for j in range(1, n):
            pv = pv + lax.dot_general(pb[:, j * PAGE_SIZE:(j + 1) * PAGE_SIZE], vps[j].astype(bf16), nt,
                                      preferred_element_type=f32)
        acc_ref[...] = alpha * acc_ref[...] + pv.reshape(N_HEADS, SUBLANES, LANES)
        m_ref[...] = m_new

    @pl.when(s == 0)
    def _():
        m_ref[...] = jnp.full(m_ref.shape, NEG_BIG, f32)
        l_ref[...] = jnp.zeros(l_ref.shape, f32)
        acc_ref[...] = jnp.zeros(acc_ref.shape, f32)
        seen_ref[...] = jnp.zeros(seen_ref.shape, f32)

    @pl.when(s < last)
    def _():
        process(keys_ref[0], [kp[0] for kp in kpages], [vp[0] for vp in vpages])

    @pl.when(s == last)
    def _():
        process(keys_ref[0, :, 0:PAGE_SIZE], [knew_ref[0]], [vnew_ref[0]])
        o_ref[0] = (acc_ref[...] / l_ref[...]).reshape(N_HEADS * SUBLANES, LANES)


def dsa_sample(q, qi, sm, k, v, cache_k, cache_v, cache_idx_k, page_table):
    b, t = q.shape[:2]
    assert t <= SUBLANES
    n_pages = page_table.shape[1]
    g = min(PAGES_PER_STEP, n_pages)
    assert n_pages % g == 0
    ns = n_pages // g
    past = n_pages * PAGE_SIZE
    n_sel = min(TOPK_MAX, (past + t) // 4)
    pt = page_table.reshape(-1).astype(i32)
    padt = lambda a: jnp.pad(a, ((0, 0), (0, SUBLANES - t)) + ((0, 0),) * (a.ndim - 2))

    qi_r = padt(qi.reshape(b, t, IDX_HEADS, IDX_DIM)).transpose(0, 2, 1, 3).reshape(b, IDX_HEADS * SUBLANES, IDX_DIM)
    w_r = padt(sm[:, :, SM_WI:SM_WI + IDX_HEADS]).transpose(0, 2, 1).reshape(b, IDX_HEADS * SUBLANES, 1)
    padk = lambda a: jnp.pad(a, ((0, 0), (0, PAGE_SIZE - t), (0, 0))).transpose(0, 2, 1)
    ki_new = padk(sm[:, :, :IDX_DIM])
    k_new = padk(k)
    v_new = padk(v)
    qh = padt(q.reshape(b, t, N_HEADS, HEAD_DIM)).transpose(0, 2, 1, 3) * (HEAD_DIM ** -0.5)
    grp = (jnp.arange(N_HEADS) // GROUP)[None, :, None, None]
    q_r = jnp.concatenate([jnp.where(grp == 0, qh, 0.0), jnp.where(grp == 1, qh, 0.0)], axis=-1)
    q_r = q_r.reshape(b, N_HEADS * SUBLANES, LANES).astype(bf16)
    n_pool = cache_k.shape[0]
    ck = cache_k.transpose(0, 2, 3, 1).reshape(n_pool, N_KV_HEADS * HEAD_DIM, PAGE_SIZE)
    cv = cache_v.transpose(0, 2, 3, 1).reshape(n_pool, N_KV_HEADS * HEAD_DIM, PAGE_SIZE)
    cki = cache_idx_k.transpose(0, 2, 1)

    def page_spec(j, w):
        return pl.BlockSpec((1, w, PAGE_SIZE),
                            lambda bi, s, ptr: (ptr[bi * n_pages + jnp.minimum(s, ns - 1) * g + j], 0, 0))

    per_b = lambda a: pl.BlockSpec((1,) + a.shape[1:], lambda bi, s, ptr: (bi,) + (0,) * (a.ndim - 1))
    blk_w = g * PAGE_SIZE
    keys_spec = pl.BlockSpec((1, SUBLANES, blk_w), lambda bi, s, ptr: (bi, 0, s))
    width = past + blk_w

    keys = pl.pallas_call(
        functools.partial(_ds_score_kernel, g=g),
        grid_spec=pltpu.PrefetchScalarGridSpec(
            num_scalar_prefetch=1, grid=(b, ns + 1),
            in_specs=[per_b(qi_r), per_b(w_r), per_b(ki_new)] + [page_spec(j, IDX_DIM) for j in range(g)],
            out_specs=keys_spec),
        out_shape=jax.ShapeDtypeStruct((b, SUBLANES, width), i32),
        compiler_params=_cparams(("parallel", "arbitrary")),
        name="dsa_sample_scores",
    )(pt, qi_r.astype(bf16), w_r, ki_new, *([cki] * g))

    rows = b * SUBLANES
    thr, need = pl.pallas_call(
        functools.partial(_ds_thr_kernel, n_sel=n_sel, chunk_w=blk_w),
        out_shape=[jax.ShapeDtypeStruct((rows, LANES), i32), jax.ShapeDtypeStruct((rows, LANES), f32)],
        compiler_params=pltpu.CompilerParams(vmem_limit_bytes=VMEM_LIMIT),
        name="dsa_sample_threshold",
    )(keys.reshape(rows, width))
    thr = thr.reshape(b, SUBLANES, LANES)
    need = need.reshape(b, SUBLANES, LANES)

    out = pl.pallas_call(
        functools.partial(_ds_attn_kernel, g=g),
        grid_spec=pltpu.PrefetchScalarGridSpec(
            num_scalar_prefetch=1, grid=(b, ns + 1),
            in_specs=[per_b(q_r), keys_spec, per_b(thr), per_b(need), per_b(k_new), per_b(v_new)]
            + [page_spec(j, LANES) for j in range(g)] * 2,
            out_specs=per_b(q_r),
            scratch_shapes=[pltpu.VMEM((N_HEADS, SUBLANES, 1), f32), pltpu.VMEM((N_HEADS, SUBLANES, 1), f32),
                            pltpu.VMEM((N_HEADS, SUBLANES, LANES), f32), pltpu.VMEM((SUBLANES, 1), f32)]),
        out_shape=jax.ShapeDtypeStruct((b, N_HEADS * SUBLANES, LANES), f32),
        compiler_params=_cparams(("parallel", "arbitrary")),
        name="dsa_sample_attention",
    )(pt, q_r, keys, thr, need, k_new, v_new, *([ck] * g), *([cv] * g))
    out = out.reshape(b, N_HEADS, SUBLANES, N_KV_HEADS, HEAD_DIM)[:, :, :t]
    out = jnp.concatenate([out[:, :GROUP, :, 0], out[:, GROUP:, :, 1]], axis=1)
    return out.transpose(0, 2, 1, 3).reshape(b, t, A_WIDTH)


def _mid_kernel(x_ref, oa_ref, ob_ref, ga_ref, gb_ref, wa_ref, wb_ref, wo_ref, gf_ref, wrh_ref, wrl_ref, br_ref,
                x1_ref, h2_ref, te_ref, tg_ref):
    a = jnp.dot(oa_ref[...].astype(bf16), wa_ref[...], preferred_element_type=f32)
    b = jnp.dot(ob_ref[...].astype(bf16), wb_ref[...], preferred_element_type=f32)
    merged = ga_ref[...] * a + gb_ref[...] * b
    x1 = x_ref[...] + jnp.dot(merged.astype(bf16), wo_ref[...], preferred_element_type=f32)
    x1_ref[...] = x1
    h2 = _rms(x1, gf_ref[...])
    h2_ref[...] = h2
    hi = h2.astype(bf16)
    lo = (h2 - hi.astype(f32)).astype(bf16)
    lg = (jnp.dot(hi, wrh_ref[...], preferred_element_type=f32) + jnp.dot(lo, wrh_ref[...], preferred_element_type=f32)
          + jnp.dot(hi, wrl_ref[...], preferred_element_type=f32)) + br_ref[...]
    lane = lax.broadcasted_iota(i32, lg.shape, 1)
    lane_f = lane.astype(f32)
    vals, ids = [], []
    for _ in range(TOP_K):
        m = jnp.max(lg, axis=1, keepdims=True)
        idx = jnp.min(jnp.where(lg == m, lane_f, float(LANES)), axis=1, keepdims=True).astype(i32)
        vals.append(m)
        ids.append(idx)
        lg = jnp.where(lane == idx, -jnp.inf, lg)
    ex = [jnp.exp(v - vals[0]) for v in vals]
    tot = ex[0] + ex[1] + ex[2] + ex[3]
    te = jnp.zeros(lg.shape, i32)
    tg = jnp.zeros(lg.shape, f32)
    for j in range(TOP_K):
        te = jnp.where(lane == j, ids[j], te)
        tg = jnp.where(lane == j, ex[j] / tot, tg)
    te_ref[...] = te
    tg_ref[...] = tg


def mid(x, oa, ob, ga, gb, wa, wb, wo, g_ffn, wr_hi, wr_lo, br):
    n = x.shape[0]
    tm = min(256, n)
    assert n % tm == 0
    row = lambda w: pl.BlockSpec((tm, w), lambda i: (i, 0))
    full = lambda a: pl.BlockSpec(a.shape, lambda i: (0,) * a.ndim)
    return pl.pallas_call(
        _mid_kernel,
        grid=(n // tm,),
        in_specs=[row(D_MODEL), row(A_WIDTH), row(M_WIDTH), row(D_MODEL), row(D_MODEL),
                  full(wa), full(wb), full(wo), full(g_ffn), full(wr_hi), full(wr_lo), full(br)],
        out_specs=[row(D_MODEL), row(D_MODEL), row(LANES), row(LANES)],
        out_shape=[jax.ShapeDtypeStruct((n, D_MODEL), f32), jax.ShapeDtypeStruct((n, D_MODEL), f32),
                   jax.ShapeDtypeStruct((n, LANES), i32), jax.ShapeDtypeStruct((n, LANES), f32)],
        compiler_params=_cparams(("parallel",)),
        name="mid",
    )(x, oa, ob, ga, gb, wa, wb, wo, g_ffn, wr_hi, wr_lo, br)


def _rank_kernel(te_ref, dest_ref, meta_ref, cnt_ref, carry_ref, *, blk):
    ph = pl.program_id(0)
    i = pl.program_id(1)
    tm = te_ref.shape[0]
    lane = lax.broadcasted_iota(i32, (tm, LANES), 1)
    te = te_ref[...]
    oh = jnp.zeros((tm, LANES), f32)
    for j in range(TOP_K):
        oh = oh + jnp.where(lane == te[:, j:j + 1], 1.0, 0.0)
    tile_cnt = jnp.sum(oh, axis=0, keepdims=True)

    @pl.when((ph == 0) & (i == 0))
    def _():
        cnt_ref[...] = jnp.zeros(cnt_ref.shape, f32)

    @pl.when(ph == 0)
    def _():
        cnt_ref[...] += tile_cnt

    @pl.when((ph == 1) & (i == 0))
    def _():
        cnt = cnt_ref[...]
        padded = jnp.floor((cnt + (blk - 1)) / blk) * blk
        r = lax.broadcasted_iota(i32, (LANES, LANES), 0)
        c = lax.broadcasted_iota(i32, (LANES, LANES), 1)
        col = jnp.sum(jnp.where(r == c, padded, 0.0), axis=1, keepdims=True)
        start = jnp.sum(jnp.where(r < c, col, 0.0), axis=0, keepdims=True)
        carry_ref[...] = start
        pad_end = start + padded
        nbp = meta_ref.shape[0]
        jb = (lax.broadcasted_iota(i32, (nbp, LANES), 0) * blk).astype(f32)
        lane2 = lax.broadcasted_iota(i32, (nbp, LANES), 1)
        be = jnp.sum(jnp.where((pad_end <= jb) & (lane2 < N_EXPERTS), 1.0, 0.0), axis=1, keepdims=True)
        be = jnp.minimum(be, float(N_EXPERTS - 1))
        n_act = jnp.sum(jnp.where(lane2 == N_EXPERTS - 1, pad_end, 0.0), axis=1, keepdims=True) / blk
        meta_ref[...] = jnp.where(lane2 == 0, be, jnp.where(lane2 == 1, n_act, 0.0)).astype(i32)

    @pl.when(ph == 1)
    def _():
        r = lax.broadcasted_iota(i32, (tm, tm), 0)
        c = lax.broadcasted_iota(i32, (tm, tm), 1)
        before = jnp.where(c < r, 1.0, 0.0).astype(bf16)
        pos = carry_ref[...] + jnp.dot(before, oh.astype(bf16), preferred_element_type=f32)
        d = jnp.zeros((tm, LANES), f32)
        for j in range(TOP_K):
            dj = jnp.sum(jnp.where(lane == te[:, j:j + 1], pos, 0.0), axis=1, keepdims=True)
            d = jnp.where(lane == j, dj, d)
        dest_ref[...] = d.astype(i32)
        carry_ref[...] += tile_cnt


def rank(te, blk, n_blocks):
    n = te.shape[0]
    tm = min(256, n)
    assert n % tm == 0
    nbp = -(-n_blocks // SUBLANES) * SUBLANES
    return pl.pallas_call(
        functools.partial(_rank_kernel, blk=blk),
        grid=(2, n // tm),
        in_specs=[pl.BlockSpec((tm, LANES), lambda ph, i: (i, 0))],
        out_specs=[pl.BlockSpec((tm, LANES), lambda ph, i: (i * ph, 0)),
                   pl.BlockSpec((nbp, LANES), lambda ph, i: (0, 0))],
        out_shape=[jax.ShapeDtypeStruct((n, LANES), i32), jax.ShapeDtypeStruct((nbp, LANES), i32)],
        scratch_shapes=[pltpu.VMEM((1, LANES), f32), pltpu.VMEM((1, LANES), f32)],
        compiler_params=_cparams(("arbitrary", "arbitrary")),
        name="rank",
    )(te)


def _ffn_kernel(be_ref, nact_ref, rt_ref, x_hbm, *rest):
    wgu_refs = rest[:W_SPLIT]
    bgu_ref = rest[W_SPLIT]
    wdn_refs = rest[W_SPLIT + 1:2 * W_SPLIT + 1]
    bdn_ref, y_ref, xbuf, sem, wgu_bf, wdn_bf = rest[2 * W_SPLIT + 1:]
    i = pl.program_id(0)
    n_act = nact_ref[0]
    blk = xbuf.shape[1]

    def row_copy(b, r, slot):
        tok = rt_ref[b * blk + r]
        return pltpu.make_async_copy(x_hbm.at[pl.ds(tok, 1)], xbuf.at[slot, pl.ds(r, 1)], sem.at[slot])

    def wait_slot(slot):
        pltpu.make_async_copy(xbuf.at[slot], xbuf.at[slot], sem.at[slot]).wait()

    @pl.when(i == 0)
    def _():
        def body(r, c):
            row_copy(0, r, 0).start()
            return c
        lax.fori_loop(0, blk, body, 0)

    @pl.when(i == n_act)
    def _():
        wait_slot(i % 2)

    @pl.when(i >= n_act)
    def _():
        y_ref[...] = jnp.zeros(y_ref.shape, f32)

    @pl.when(i < n_act)
    def _():
        slot = i % 2
        wait_slot(slot)

        @pl.when((i == 0) | (be_ref[i] != be_ref[jnp.maximum(i - 1, 0)]))
        def _():
            wg = wgu_bf.shape[1] // W_SPLIT
            wd = wdn_bf.shape[1] // W_SPLIT
            for c in range(W_SPLIT):
                wgu_bf[:, c * wg:(c + 1) * wg] = wgu_refs[c][0].astype(bf16)
                wdn_bf[:, c * wd:(c + 1) * wd] = wdn_refs[c][0].astype(bf16)

        for r in range(blk):
            row_copy(i + 1, r, 1 - slot).start()
        x = xbuf[slot].astype(bf16)
        gu = jnp.dot(x, wgu_bf[...], preferred_element_type=f32) + bgu_ref[0]
        gate = jnp.minimum(gu[:, :EXPERT_FF], SWIGLU_LIMIT)
        up = jnp.clip(gu[:, EXPERT_FF:], -SWIGLU_LIMIT, SWIGLU_LIMIT)
        act = (up + 1.0) * (gate * _sigmoid(SWIGLU_ALPHA * gate))
        y_ref[...] = jnp.dot(act.astype(bf16), wdn_bf[...], preferred_element_type=f32) + bdn_ref[0]


def ffn(blk_e, n_act, row_tok, h2, w_gu, b_gu, w_dn, b_dn, blk, n_blocks):
    d = h2.shape[1]
    ff2 = w_gu.shape[2]
    return pl.pallas_call(
        _ffn_kernel,
        grid_spec=pltpu.PrefetchScalarGridSpec(
            num_scalar_prefetch=3, grid=(n_blocks,),
            in_specs=[pl.BlockSpec(memory_space=pl.ANY)]
            + [pl.BlockSpec((1, d, ff2 // W_SPLIT), functools.partial(lambda i, be, na, rt, c: (be[i], 0, c), c=c))
               for c in range(W_SPLIT)]
            + [pl.BlockSpec((1, 1, ff2), lambda i, be, na, rt: (be[i], 0, 0))]
            + [pl.BlockSpec((1, ff2 // 2, d // W_SPLIT), functools.partial(lambda i, be, na, rt, c: (be[i], 0, c), c=c))
               for c in range(W_SPLIT)]
            + [pl.BlockSpec((1, 1, d), lambda i, be, na, rt: (be[i], 0, 0))],
            out_specs=pl.BlockSpec((blk, d), lambda i, be, na, rt: (i, 0)),
            scratch_shapes=[pltpu.VMEM((2, blk, d), f32), pltpu.SemaphoreType.DMA((2,)),
                            pltpu.VMEM((d, ff2), bf16), pltpu.VMEM((ff2 // 2, d), bf16)]),
        out_shape=jax.ShapeDtypeStruct((n_blocks * blk, d), f32),
        compiler_params=_cparams(("arbitrary",)),
        name="ffn",
    )(blk_e, n_act, row_tok, h2, *([w_gu] * W_SPLIT), b_gu.reshape(b_gu.shape[0], 1, ff2),
      *([w_dn] * W_SPLIT), b_dn.reshape(b_dn.shape[0], 1, d))


def _out_kernel(dest_ref, x1_ref, tg_ref, p_ref, ys_hbm, gp_ref, wg_ref, wp_ref, gfin_ref, y_ref, buf, sem):
    i = pl.program_id(0)
    n = pl.num_programs(0)
    tm = x1_ref.shape[0]

    def row_copy(t, r, j, slot):
        d = dest_ref[(t * tm + r) * TOP_K + j]
        return pltpu.make_async_copy(ys_hbm.at[pl.ds(d, 1)], buf.at[slot, j, pl.ds(r, 1)], sem.at[slot])

    def wait_slot(slot):
        pltpu.make_async_copy(buf.at[slot], buf.at[slot], sem.at[slot]).wait()

    @pl.when(i == 0)
    def _():
        def body(r, c):
            for j in range(TOP_K):
                row_copy(0, r, j, 0).start()
            return c
        lax.fori_loop(0, tm, body, 0)

    slot = i % 2
    wait_slot(slot)
    nxt = jnp.minimum(i + 1, n - 1)
    for r in range(tm):
        for j in range(TOP_K):
            row_copy(nxt, r, j, 1 - slot).start()
    tg = tg_ref[...]
    x2 = x1_ref[...]
    for j in range(TOP_K):
        x2 = x2 + tg[:, j:j + 1] * buf[slot, j]
    hn = _rms(x2, gp_ref[...]).astype(bf16)
    gate = _sigmoid(jnp.dot(hn, wg_ref[...], preferred_element_type=f32))
    x3 = x2 + gate * jnp.dot(p_ref[...].astype(bf16), wp_ref[...], preferred_element_type=f32)
    y_ref[...] = _rms(x3, gfin_ref[...])

    @pl.when(i == n - 1)
    def _():
        wait_slot(1 - slot)


def out_stage(dest_flat, x1, tg, p, ys, g_ple, wg, wp, g_final):
    n, d = x1.shape
    tm = min(OUT_TM, n)
    assert n % tm == 0
    row = lambda w: pl.BlockSpec((tm, w), lambda i, ds: (i, 0))
    full = lambda a: pl.BlockSpec(a.shape, lambda i, ds: (0,) * a.ndim)
    return pl.pallas_call(
        _out_kernel,
        grid_spec=pltpu.PrefetchScalarGridSpec(
            num_scalar_prefetch=1, grid=(n // tm,),
            in_specs=[row(d), row(LANES), row(p.shape[1]), pl.BlockSpec(memory_space=pl.ANY),
                      full(g_ple), full(wg), full(wp), full(g_final)],
            out_specs=row(d),
            scratch_shapes=[pltpu.VMEM((2, TOP_K, tm, d), f32), pltpu.SemaphoreType.DMA((2,))]),
        out_shape=jax.ShapeDtypeStruct((n, d), f32),
        compiler_params=_cparams(("arbitrary",)),
        name="out",
    )(dest_flat, x1, tg, p, ys, g_ple, wg, wp, g_final)


def tail(x, oa, ob, ga, gb, p, w):
    n = x.shape[0]
    x1, h2, te, tg = mid(x, oa, ob, ga, gb, w["wa"], w["wb"], w["wo"], w["g_ffn"], w["wr_hi"], w["wr_lo"], w["br"])
    blk = FFN_BLK
    n_blocks = -(-(n * TOP_K) // blk) + N_EXPERTS
    dest, meta = rank(te, blk, n_blocks)
    dest_flat = dest[:, :TOP_K].reshape(-1)
    blk_e = meta[:n_blocks, 0]
    n_act = meta[0:1, 1]
    tok = jnp.repeat(jnp.arange(n, dtype=i32), TOP_K)
    row_tok = jnp.zeros((n_blocks * blk,), i32).at[dest_flat].set(tok)
    ys = ffn(blk_e, n_act, row_tok, h2, w["w_gu"], w["b_gu"], w["w_dn"], w["b_dn"], blk, n_blocks)
    return out_stage(dest_flat, x1, tg, p, ys, w["g_ple"], w["wg"], w["wp"], w["g_final"])


def _prep_weights(g_mix, w_in, b_gate, g_mnorm, w_up_a, w_up_b, w_out, g_ffn, w_router, b_router,
                  w_gu, b_gu, w_dn, b_dn, g_ple, w_ple_gate, w_ple_proj, g_final):
    wr = jnp.pad(w_router.astype(f32), ((0, 0), (0, LANES - N_EXPERTS)))
    wr_hi = wr.astype(bf16)
    wr_lo = (wr - wr_hi.astype(f32)).astype(bf16)
    br = jnp.full((1, LANES), -jnp.inf, f32).at[0, :N_EXPERTS].set(b_router.astype(f32))
    bias = jnp.zeros((1, LANES), f32).at[0, SM_IG:SM_IG + 2 * M_HEADS].set(b_gate.astype(f32))
    return dict(g_mix=g_mix.reshape(1, -1), w_in=_pack_w_in(w_in), bias=bias, g_mnorm=g_mnorm,
                wa=w_up_a.astype(bf16), wb=w_up_b.astype(bf16), wo=w_out.astype(bf16), g_ffn=g_ffn.reshape(1, -1),
                wr_hi=wr_hi, wr_lo=wr_lo, br=br, w_gu=w_gu, b_gu=b_gu, w_dn=w_dn, b_dn=b_dn,
                g_ple=g_ple.reshape(1, -1), wg=w_ple_gate.astype(bf16), wp=w_ple_proj.astype(bf16),
                g_final=g_final.reshape(1, -1))


def _layer(x, p, pos, attn_fn, c0, n0, m0, w):
    b, t = x.shape[:2]
    n = b * t
    cos, sin = _rope_tables(pos)
    cos = jnp.tile(cos, (b, 1))
    sin = jnp.tile(sin, (b, 1))
    x2d = x.reshape(n, D_MODEL)
    q, k, v, qi, sm, mq, mk, mv, og, ga, gb = in_proj(x2d, cos, sin, w["g_mix"], w["w_in"], w["bias"])
    r3 = lambda a: a.reshape(b, t, a.shape[-1])
    o_a = attn_fn(r3(q), r3(qi), r3(sm), r3(k), r3(v))
    o_b, c, nn, m = mlstm_branch(r3(mq), r3(mk), r3(mv), r3(og), r3(sm), c0, n0, m0, w["g_mnorm"])
    y = tail(x2d, o_a.reshape(n, A_WIDTH), o_b.reshape(n, M_WIDTH), ga, gb, p.reshape(n, PLE_DIM), w)
    state = (k.reshape(b, t, N_KV_HEADS, HEAD_DIM), v.reshape(b, t, N_KV_HEADS, HEAD_DIM),
             sm[:, :IDX_DIM].reshape(b, t, IDX_DIM), c, nn, m)
    return y.reshape(b, t, D_MODEL), state


def kernel(x_prompt, x_sample, cache_k, cache_v, cache_idx_k, state_C, state_n, state_m, page_table,
           p_prompt, p_sample, g_mix, w_in, b_gate, g_mnorm, w_up_a, w_up_b, w_out, g_ffn,
           w_router, b_router, w_gu, b_gu, w_dn, b_dn, g_ple, w_ple_gate, w_ple_proj, g_final):
    assert x_prompt.shape[-1] == D_MODEL and w_in.shape[0] == 1, "single-layer model of width D_MODEL"
    bp, tp = x_prompt.shape[:2]
    ts = x_sample.shape[1]
    past = page_table.shape[1] * PAGE_SIZE
    w = _prep_weights(g_mix[0], w_in[0], b_gate[0], g_mnorm[0], w_up_a[0], w_up_b[0], w_out[0], g_ffn[0],
                      w_router[0], b_router[0], w_gu[0], b_gu[0], w_dn[0], b_dn[0], g_ple[0],
                      w_ple_gate[0], w_ple_proj[0], g_final)
    zeros = lambda *s: jnp.zeros(s, f32)
    yp, sp = _layer(x_prompt, p_prompt[0], jnp.arange(tp, dtype=i32), dsa_prompt_branch,
                    zeros(bp, M_HEADS, M_V, M_QK), zeros(bp, M_HEADS, M_QK), zeros(bp, M_HEADS), w)
    attn_s = functools.partial(dsa_sample, cache_k=cache_k[0], cache_v=cache_v[0], cache_idx_k=cache_idx_k[0],
                               page_table=page_table)
    ys, ss = _layer(x_sample, p_sample[0], past + jnp.arange(ts, dtype=i32), attn_s,
                    state_C[0], state_n[0], state_m[0], w)
    return (yp, ys) + tuple(s[None] for s in sp) + tuple(s[None] for s in ss)
```

```python
import functools

import jax
import jax.numpy as jnp
from jax import lax
from jax.experimental import pallas as pl
from jax.experimental.pallas import tpu as pltpu

f32 = jnp.float32
bf16 = jnp.bfloat16
i32 = jnp.int32

D_MODEL = 1024
PAGE_SIZE = 128
N_HEADS = 8
N_KV_HEADS = 2
HEAD_DIM = 64
GROUP = N_HEADS // N_KV_HEADS
IDX_HEADS = 4
IDX_DIM = 64
TOPK_MAX = 256
ROPE_THETA = 10000.0
M_HEADS = 4
M_QK = 64
M_V = 128
N_EXPERTS = 32
TOP_K = 4
EXPERT_FF = D_MODEL
SWIGLU_LIMIT = 7.0
SWIGLU_ALPHA = 1.702
PLE_DIM = 256
EPS = 1e-6
A_WIDTH = N_HEADS * HEAD_DIM
M_WIDTH = M_HEADS * M_V

LANES = 128
SUBLANES = 8
VMEM_LIMIT = 56 * 1024 * 1024

M_CHUNK = 128
MLSTM_BATCH = 1
TQ = 128
KC = 1024
PAGES_PER_STEP = 16
FFN_BLK = 256
W_SPLIT = 4
OUT_TM = 128

INT_MIN = -2147483648
INT_MAX = 2147483647
NEG_BIG = -1e30
LOG2E = 1.4426950408889634
ROW_SLICE = 16

SM_KI = 0
SM_WI = IDX_DIM
SM_IG = SM_WI + IDX_HEADS
SM_LF = SM_IG + M_HEADS


def _cparams(sem):
    return pltpu.CompilerParams(dimension_semantics=sem, vmem_limit_bytes=VMEM_LIMIT)


def _rms(x, g):
    return x * lax.rsqrt(jnp.mean(x * x, axis=-1, keepdims=True) + EPS) * g


def _sigmoid(x):
    return 1.0 / (1.0 + jnp.exp(-x))


def _log_sigmoid(x):
    return jnp.minimum(x, 0.0) - jnp.log1p(jnp.exp(-jnp.abs(x)))


def _sortable(x):
    bits = lax.bitcast_convert_type(x, i32)
    key = bits ^ ((bits >> 31) & INT_MAX)
    return jnp.where(x == 0.0, 0, key)


_G_Q = (0, 512)
_G_K = (512, 640)
_G_V = (640, 768)
_G_QI = (768, 1024)
_G_SM = (1024, 1152)
_G_MQ = (1152, 1408)
_G_MK = (1408, 1664)
_G_MV = (1664, 2176)
_G_MO = (2176, 2688)
_G_GA = (2688, 3712)
_G_GB = (3712, 4736)
_W_COLS = 4736


def _pack_w_in(w_in):
    o = [0]
    for s in (A_WIDTH, 128, 128, 256, 64, 4, 256, 256, 512, 4, 4, 512, 1024, 1024):
        o.append(o[-1] + s)
    aq, ak, av, iq, ik, iw, mq, mk, mv, mi, mf, mo, ga, gb = [w_in[:, o[i]:o[i + 1]] for i in range(14)]
    pad = jnp.zeros((w_in.shape[0], LANES - (IDX_DIM + IDX_HEADS + 2 * M_HEADS)), w_in.dtype)
    small = jnp.concatenate([ik, iw, mi, mf, pad], axis=1)
    w = jnp.concatenate([aq, ak, av, iq, small, mq, mk, mv, mo, ga, gb], axis=1)
    return w.astype(bf16)


def _in_kernel(x_ref, cos_ref, sin_ref, g_ref, w_ref, bias_ref,
               q_ref, k_ref, v_ref, qi_ref, sm_ref, mq_ref, mk_ref, mv_ref, og_ref, ga_ref, gb_ref):
    x = x_ref[...]
    hb = _rms(x, g_ref[...]).astype(bf16)
    cos = cos_ref[...]
    sin = sin_ref[...]
    tm = x.shape[0]
    lane = lax.broadcasted_iota(i32, (tm, LANES), 1)
    first_half = (lane % HEAD_DIM) < (HEAD_DIM // 2)

    def rope(z):
        rot = jnp.where(first_half, pltpu.roll(z, LANES - HEAD_DIM // 2, 1), pltpu.roll(z, HEAD_DIM // 2, 1))
        return z * cos + rot * sin

    def proj(grp):
        return jnp.dot(hb, w_ref[:, grp[0]:grp[1]], preferred_element_type=f32)

    z = proj(_G_Q)
    for j in range(4):
        q_ref[:, j * LANES:(j + 1) * LANES] = rope(z[:, j * LANES:(j + 1) * LANES])
    k_ref[...] = rope(proj(_G_K))
    v_ref[...] = proj(_G_V)
    z = proj(_G_QI)
    for j in range(2):
        qi_ref[:, j * LANES:(j + 1) * LANES] = rope(z[:, j * LANES:(j + 1) * LANES])
    z = proj(_G_SM)
    zb = z + bias_ref[...]
    sm = jnp.where(lane < SM_WI, rope(z),
                   jnp.where(lane < SM_IG, z * (IDX_HEADS ** -0.5 * IDX_DIM ** -0.5),
                             jnp.where(lane < SM_LF, zb,
                                       jnp.where(lane < SM_LF + M_HEADS, _log_sigmoid(zb), 0.0))))
    sm_ref[...] = sm
    mq_ref[...] = proj(_G_MQ)
    mk_ref[...] = proj(_G_MK) * (M_QK ** -0.5)
    mv_ref[...] = proj(_G_MV)
    og_ref[...] = _sigmoid(proj(_G_MO))
    ga_ref[...] = _sigmoid(proj(_G_GA))
    gb_ref[...] = _sigmoid(proj(_G_GB))


def _rope_tables(pos):
    half = HEAD_DIM // 2
    inv = ROPE_THETA ** (-jnp.arange(half, dtype=f32) / half)
    ang = pos.astype(f32)[:, None] * inv[None, :]
    cos = jnp.cos(ang)
    sin = jnp.sin(ang)
    cos128 = jnp.tile(cos, (1, 4))
    sin128 = jnp.tile(jnp.concatenate([-sin, sin], axis=1), (1, 2))
    return cos128, sin128


def in_proj(x2d, cos128, sin128, g_mix, w_packed, bias128):
    n = x2d.shape[0]
    tm = min(256, n)
    assert n % tm == 0
    row = lambda w: pl.BlockSpec((tm, w), lambda i: (i, 0))
    full = lambda a: pl.BlockSpec(a.shape, lambda i: (0,) * a.ndim)
    widths = (512, 128, 128, 256, 128, 256, 256, 512, 512, 1024, 1024)
    return pl.pallas_call(
        _in_kernel,
        grid=(n // tm,),
        in_specs=[row(D_MODEL), row(LANES), row(LANES), full(g_mix), full(w_packed), full(bias128)],
        out_specs=[row(w) for w in widths],
        out_shape=[jax.ShapeDtypeStruct((n, w), f32) for w in widths],
        compiler_params=_cparams(("parallel",)),
        name="in_proj",
    )(x2d, cos128, sin128, g_mix, w_packed, bias128)


def _mlstm_kernel(mq_ref, mk_ref, mv_ref, og_ref, sm_ref, gr_ref, c0_ref, n0_ref, m0_ref, gn_ref,
                  ob_ref, c_ref, n_ref, m_ref):
    ci = pl.program_id(1)
    L = mq_ref.shape[1]

    @pl.when(ci == 0)
    def _():
        c_ref[...] = c0_ref[...]
        n_ref[...] = n0_ref[...]
        m_ref[...] = m0_ref[...]

    row = lax.broadcasted_iota(i32, (L, L), 0)
    col = lax.broadcasted_iota(i32, (L, L), 1)
    tril = row >= col
    for bi, hd in [(bi, hd) for bi in range(mq_ref.shape[0]) for hd in range(M_HEADS)]:
        sm = sm_ref[bi]
        gr = gr_ref[bi]
        q = mq_ref[bi, :, hd * M_QK:(hd + 1) * M_QK]
        k = mk_ref[bi, :, hd * M_QK:(hd + 1) * M_QK]
        v = mv_ref[bi, :, hd * M_V:(hd + 1) * M_V]
        ig_r = gr[hd:hd + 1, :]
        lf_r = gr[M_HEADS + hd:M_HEADS + hd + 1, :]
        ig_c = sm[:, SM_IG + hd:SM_IG + hd + 1]
        lf_c = sm[:, SM_LF + hd:SM_LF + hd + 1]
        C = c_ref[bi, hd]
        nrow = n_ref[bi, hd]
        m_prev = m_ref[bi, hd]
        b_c = jnp.sum(jnp.where(tril, lf_r, 0.0), axis=1, keepdims=True)
        b_r = jnp.sum(jnp.where(tril, 0.0, lf_c) + jnp.where(row == col, lf_c, 0.0), axis=0, keepdims=True)
        dmat = jnp.where(tril, b_c - b_r + ig_r, -jnp.inf)
        inter = b_c + m_prev
        m_t = jnp.maximum(inter, jnp.max(dmat, axis=1, keepdims=True))
        qb = q.astype(bf16)
        kb = k.astype(bf16)
        qk = lax.dot_general(qb, kb, (((1,), (1,)), ((), ())), preferred_element_type=f32)
        s = qk * jnp.exp(dmat - m_t)
        w_inter = jnp.exp(inter - m_t)
        qc = lax.dot_general(qb, C.astype(bf16), (((1,), (1,)), ((), ())), preferred_element_type=f32)
        num = jnp.dot(s.astype(bf16), v.astype(bf16), preferred_element_type=f32) + w_inter * qc
        den = jnp.sum(s, axis=1, keepdims=True) + w_inter * jnp.sum(q * nrow, axis=1, keepdims=True)
        h = num / jnp.maximum(jnp.abs(den), jnp.exp(-m_t))
        b_last = b_c[L - 1:L, :]
        g_c = b_last - b_c + ig_c
        m_new = jnp.maximum(b_last + m_prev, jnp.max(g_c, axis=0, keepdims=True))
        w_k = jnp.exp(g_c - m_new)
        decay = jnp.exp(b_last + m_prev - m_new)
        wv = (w_k * v).astype(bf16)
        c_ref[bi, hd] = decay * C + lax.dot_general(wv, kb, (((0,), (0,)), ((), ())), preferred_element_type=f32)
        n_ref[bi, hd] = decay * nrow + jnp.sum(w_k * k, axis=0, keepdims=True)
        m_ref[bi, hd] = m_new
        gn = gn_ref[:, hd * M_V:(hd + 1) * M_V]
        ob_ref[bi, :, hd * M_V:(hd + 1) * M_V] = og_ref[bi, :, hd * M_V:(hd + 1) * M_V] * _rms(h, gn)


def mlstm(mq, mk, mv, og, sm, grow, c0, n0, m0, g_mnorm):
    b, t = mq.shape[:2]
    L = M_CHUNK
    bb = MLSTM_BATCH if b % MLSTM_BATCH == 0 else 1
    assert t % L == 0
    tok = lambda w: pl.BlockSpec((bb, L, w), lambda bi, ci: (bi, ci, 0))
    st = lambda a: pl.BlockSpec((bb,) + a.shape[1:], lambda bi, ci: (bi,) + (0,) * (a.ndim - 1))
    return pl.pallas_call(
        _mlstm_kernel,
        grid=(b // bb, t // L),
        in_specs=[tok(256), tok(256), tok(512), tok(512), tok(LANES),
                  pl.BlockSpec((bb, 2 * M_HEADS, L), lambda bi, ci: (bi, 0, ci)),
                  st(c0), st(n0), st(m0), pl.BlockSpec(g_mnorm.shape, lambda bi, ci: (0, 0))],
        out_specs=[tok(512), st(c0), st(n0), st(m0)],
        out_shape=[jax.ShapeDtypeStruct((b, t, M_WIDTH), f32), jax.ShapeDtypeStruct(c0.shape, f32),
                   jax.ShapeDtypeStruct(n0.shape, f32), jax.ShapeDtypeStruct(m0.shape, f32)],
        compiler_params=_cparams(("parallel", "arbitrary")),
        name="mlstm",
    )(mq, mk, mv, og, sm, grow, c0, n0, m0, g_mnorm)


def mlstm_branch(mq, mk, mv, og, sm, c0, n0, m0, g_mnorm):
    b, t = mq.shape[:2]
    tp = -(-t // M_CHUNK) * M_CHUNK
    if tp != t:
        pad = lambda a: jnp.pad(a, ((0, 0), (0, tp - t), (0, 0)))
        mq, mk, mv, og = pad(mq), pad(mk), pad(mv), pad(og)
        sm_pad = jnp.zeros((b, tp - t, LANES), f32).at[:, :, SM_IG:SM_IG + M_HEADS].set(NEG_BIG)
        sm = jnp.concatenate([sm, sm_pad], axis=1)
    grow = sm[:, :, SM_IG:SM_IG + 2 * M_HEADS].transpose(0, 2, 1)
    ob, c, n, m = mlstm(mq, mk, mv, og, sm, grow, c0.astype(f32), n0.astype(f32).reshape(b, M_HEADS, 1, M_QK),
                        m0.astype(f32).reshape(b, M_HEADS, 1, 1), g_mnorm.reshape(1, M_WIDTH))
    return ob[:, :t], c, n.reshape(b, M_HEADS, M_QK), m.reshape(b, M_HEADS)


def _select_threshold(get_chunk, n_chunks, chunk_w, rows, n_sel, dynamic):
    def count(pred):
        def body(c, acc):
            hit = jnp.where(pred(get_chunk(c)), 1.0, 0.0)
            for j in range(chunk_w // LANES):
                acc = acc + hit[:, j * LANES:(j + 1) * LANES]
            return acc
        acc0 = jnp.zeros((rows, LANES), f32)
        if dynamic:
            acc = lax.fori_loop(0, n_chunks, body, acc0)
        else:
            acc = acc0
            for c in range(n_chunks):
                acc = body(c, acc)
        return jnp.sum(acc, axis=1, keepdims=True)

    kf = float(n_sel)

    total = jnp.zeros((rows, 1), f32) + (n_chunks * chunk_w).astype(f32) if dynamic else \
        jnp.full((rows, 1), float(n_chunks * chunk_w), f32)
    c_nonneg = count(lambda keys: keys >= 0)
    c_pos = count(lambda keys: keys >= 1)
    take0 = c_nonneg >= kf
    zero_tie = take0 & (c_pos < kf)

    def unsettled(st):
        it, _, cnt = st
        return (it < 32) & (jnp.max(jnp.where(zero_tie, 0.0, jnp.abs(cnt - kf))) > 0.0)

    def bit_step(st):
        it, thr_u, cnt = st
        cand_u = thr_u | lax.shift_left(jnp.int32(1), 31 - it)
        cand_s = cand_u ^ INT_MIN
        c = count(lambda keys: keys >= cand_s)
        take = c >= kf
        return it + 1, jnp.where(take, cand_u, thr_u), jnp.where(take, c, cnt)

    state0 = (jnp.int32(1), jnp.where(take0, INT_MIN, 0).astype(i32), jnp.where(take0, c_nonneg, total))
    _, thr_u, _ = lax.while_loop(unsettled, bit_step, state0)
    thr = jnp.maximum(thr_u ^ INT_MIN, INT_MIN + 1)
    need = kf - count(lambda keys: keys > thr)
    return thr, need


def _tie_prefix_matrix():
    r = lax.broadcasted_iota(i32, (LANES, LANES), 0)
    c = lax.broadcasted_iota(i32, (LANES, LANES), 1)
    return jnp.where(r <= c, 1.0, 0.0).astype(bf16)


def _selected_block(keys, thr, need, seen, tri):
    eq = keys == thr
    rank = seen + jnp.dot(jnp.where(eq, 1.0, 0.0).astype(bf16), tri, preferred_element_type=f32)
    sel = (keys > thr) | (eq & (rank <= need))
    return sel, rank[:, LANES - 1:LANES]


def _dsa_prompt_kernel(q_ref, qi_ref, sm_ref, kit_ref, kt_ref, vd_ref, o_ref,
                       keys_ref, bias_ref, lg_ref, p_ref, m_ref, acc_ref, *, n_sel):
    qb = pl.program_id(1)
    tq = q_ref.shape[1]
    kc = kit_ref.shape[3]
    n_chunks = (qb * tq + tq - 1) // kc + 1
    lane = lax.broadcasted_iota(i32, (tq, LANES), 1)
    lo = lane < HEAD_DIM
    t_col = qb * tq + lax.broadcasted_iota(i32, (tq, 1), 0)
    sm = sm_ref[0]

    qi = qi_ref[0]
    qi_h = []
    for h in range(IDX_HEADS):
        blk = qi[:, (h // 2) * LANES:(h // 2 + 1) * LANES]
        qi_h.append(jnp.where(lo if h % 2 == 0 else ~lo, blk, 0.0).astype(bf16))
    w_h = [sm[:, SM_WI + h:SM_WI + h + 1] for h in range(IDX_HEADS)]

    def score_chunk(c, carry):
        kt = kit_ref[0, c]
        sc = jnp.zeros((tq, kc), f32)
        for h in range(IDX_HEADS):
            s = jnp.dot(qi_h[h], kt, preferred_element_type=f32)
            sc = sc + w_h[h] * jnp.maximum(s, 0.0)
        idx = c * kc + lax.broadcasted_iota(i32, (tq, kc), 1)
        keys_ref[c] = jnp.where(idx <= t_col, _sortable(sc), INT_MIN)
        return carry

    lax.fori_loop(0, n_chunks, score_chunk, 0)

    thr, need = _select_threshold(lambda c: keys_ref[c], n_chunks, kc, tq, n_sel, True)
    tri = _tie_prefix_matrix()

    q = q_ref[0] * (HEAD_DIM ** -0.5 * LOG2E)
    q_g = []
    for g in range(N_KV_HEADS):
        parts = []
        for j in range(GROUP):
            h = g * GROUP + j
            blk = q[:, (h // 2) * LANES:(h // 2 + 1) * LANES]
            parts.append(jnp.where(lo if h % 2 == 0 else ~lo, blk, 0.0).astype(bf16))
        q_g.append(jnp.concatenate(parts, axis=0))
    m_ref[...] = jnp.full(m_ref.shape, NEG_BIG, f32)
    acc_ref[...] = jnp.zeros(acc_ref.shape, f32)
    rs = min(ROW_SLICE, tq)

    def attend_chunk(c, seen):
        for j in range(kc // LANES):
            sel, seen = _selected_block(keys_ref[c, :, j * LANES:(j + 1) * LANES], thr, need, seen, tri)
            bias_ref[:, j * LANES:(j + 1) * LANES] = jnp.where(sel, 0.0, NEG_BIG)
        for g in range(N_KV_HEADS):
            lg_ref[...] = jnp.dot(q_g[g], kt_ref[0, g, c], preferred_element_type=f32)
            for r0 in range(0, GROUP * tq, rs):
                x = lg_ref[r0:r0 + rs] + bias_ref[r0 % tq:r0 % tq + rs]
                m_old = m_ref[g, r0:r0 + rs]
                m_new = jnp.maximum(m_old, jnp.max(x, axis=-1, keepdims=True))
                p_ref[r0:r0 + rs] = jnp.exp2(x - m_new).astype(bf16)
                acc_ref[g, r0:r0 + rs] = jnp.exp2(m_old - m_new) * acc_ref[g, r0:r0 + rs]
                m_ref[g, r0:r0 + rs] = m_new
            acc_ref[g] += jnp.dot(p_ref[...], vd_ref[0, g, c], preferred_element_type=f32)
        return seen

    lax.fori_loop(0, n_chunks, attend_chunk, jnp.zeros((tq, 1), f32))

    for g in range(N_KV_HEADS):
        acc = acc_ref[g]
        out = acc / acc[:, HEAD_DIM:HEAD_DIM + 1]
        for jp in range(GROUP // 2):
            even = out[(2 * jp) * tq:(2 * jp + 1) * tq]
            odd = pltpu.roll(out[(2 * jp + 1) * tq:(2 * jp + 2) * tq], HEAD_DIM, 1)
            o_ref[0, :, (g * 2 + jp) * LANES:(g * 2 + jp + 1) * LANES] = jnp.where(lo, even, odd)


def dsa_prompt(q, qi, sm, kit2, kt2, vd):
    b, t = q.shape[:2]
    nc, kc = kit2.shape[1], kit2.shape[3]
    tq = min(TQ, t)
    n_sel = min(TOPK_MAX, t // 4)
    tok = lambda w: pl.BlockSpec((1, tq, w), lambda bi, qb: (bi, qb, 0))
    return pl.pallas_call(
        functools.partial(_dsa_prompt_kernel, n_sel=n_sel),
        grid=(b, t // tq),
        in_specs=[tok(A_WIDTH), tok(IDX_HEADS * IDX_DIM), tok(LANES),
                  pl.BlockSpec((1, nc, LANES, kc), lambda bi, qb: (bi, 0, 0, 0)),
                  pl.BlockSpec((1, N_KV_HEADS, nc, LANES, kc), lambda bi, qb: (bi, 0, 0, 0, 0)),
                  pl.BlockSpec((1, N_KV_HEADS, nc, kc, LANES), lambda bi, qb: (bi, 0, 0, 0, 0))],
        out_specs=tok(A_WIDTH),
        out_shape=jax.ShapeDtypeStruct((b, t, A_WIDTH), f32),
        scratch_shapes=[pltpu.VMEM((nc, tq, kc), i32),
                        pltpu.VMEM((tq, kc), f32),
                        pltpu.VMEM((GROUP * tq, kc), f32),
                        pltpu.VMEM((GROUP * tq, kc), bf16),
                        pltpu.VMEM((N_KV_HEADS, GROUP * tq, 1), f32),
                        pltpu.VMEM((N_KV_HEADS, GROUP * tq, LANES), f32)],
        compiler_params=_cparams(("parallel", "arbitrary")),
        name="dsa_prompt",
    )(q, qi, sm, kit2, kt2, vd)


def dsa_prompt_branch(q, qi, sm, k, v):
    b, t = q.shape[:2]
    kc = min(KC, t)
    nc = t // kc
    kit = sm[:, :, :IDX_DIM].astype(bf16).reshape(b, nc, kc, IDX_DIM).transpose(0, 1, 3, 2)
    kit2 = jnp.concatenate([kit, kit], axis=2)
    kt = k.astype(bf16).reshape(b, nc, kc, N_KV_HEADS, HEAD_DIM).transpose(0, 3, 1, 4, 2)
    kt2 = jnp.concatenate([kt, kt], axis=3)
    vb = v.astype(bf16).reshape(b, nc, kc, N_KV_HEADS, HEAD_DIM).transpose(0, 3, 1, 2, 4)
    vd = jnp.concatenate([vb, jnp.ones_like(vb)], axis=4)
    return dsa_prompt(q, qi, sm, kit2, kt2, vd)


def _idx_scores(qi, w, ktpage):
    s = jnp.dot(qi, ktpage.astype(bf16), preferred_element_type=f32)
    r = (w * jnp.maximum(s, 0.0)).reshape(IDX_HEADS, SUBLANES, s.shape[1])
    sc = r[0]
    for h in range(1, IDX_HEADS):
        sc = sc + r[h]
    return sc


def _ds_score_kernel(pt_ref, qi_ref, w_ref, kin_ref, *rest, g):
    pages = rest[:g]
    keys_ref = rest[g]
    s = pl.program_id(1)
    last = pl.num_programs(1) - 1
    qi = qi_ref[0]
    w = w_ref[0]

    @pl.when(s < last)
    def _():
        kcat = jnp.concatenate([pg[0].astype(bf16) for pg in pages], axis=1) if g > 1 else pages[0][0]
        keys_ref[0] = _sortable(_idx_scores(qi, w, kcat))

    @pl.when(s == last)
    def _():
        keys_ref[0] = jnp.full(keys_ref.shape[1:], INT_MIN, i32)
        sc = _idx_scores(qi, w, kin_ref[0])
        tok = lax.broadcasted_iota(i32, sc.shape, 0)
        j = lax.broadcasted_iota(i32, sc.shape, 1)
        keys_ref[0, :, 0:PAGE_SIZE] = jnp.where(j <= tok, _sortable(sc), INT_MIN)


def _ds_thr_kernel(keys_ref, thr_ref, need_ref, *, n_sel, chunk_w):
    rows, width = keys_ref.shape

    def get_chunk(c):
        return keys_ref[:, c * chunk_w:(c + 1) * chunk_w]

    thr, need = _select_threshold(get_chunk, width // chunk_w, chunk_w, rows, n_sel, False)
    thr_ref[...] = jnp.broadcast_to(thr, thr_ref.shape)
    need_ref[...] = jnp.broadcast_to(need, need_ref.shape)


def _ds_attn_kernel(pt_ref, q_ref, keys_ref, thr_ref, need_ref, knew_ref, vnew_ref, *rest, g):
    kpages = rest[:g]
    vpages = rest[g:2 * g]
    o_ref, m_ref, l_ref, acc_ref, seen_ref = rest[2 * g:]
    s = pl.program_id(1)
    last = pl.num_programs(1) - 1
    q = q_ref[0]
    thr = thr_ref[0][:, 0:1]
    need = need_ref[0][:, 0:1]
    tri = _tie_prefix_matrix()

    def process(keys, kps, vps):
        n = len(kps)
        seen = seen_ref[...]
        bias = []
        for j in range(n):
            sel, seen = _selected_block(keys[:, j * PAGE_SIZE:(j + 1) * PAGE_SIZE], thr, need, seen, tri)
            bias.append(jnp.where(sel, 0.0, NEG_BIG))
        seen_ref[...] = seen
        bias = jnp.concatenate(bias, axis=1) if n > 1 else bias[0]
        kcat = jnp.concatenate([kp.astype(bf16) for kp in kps], axis=1) if n > 1 else kps[0].astype(bf16)
        vcat = jnp.concatenate([vp.astype(bf16) for vp in vps], axis=1) if n > 1 else vps[0].astype(bf16)
        lg = jnp.dot(q, kcat, preferred_element_type=f32)
        lg = lg.reshape(N_HEADS, SUBLANES, n * PAGE_SIZE) + bias[None]
        m_old = m_ref[...]
        m_new = jnp.maximum(m_old, jnp.max(lg, axis=-1, keepdims=True))
        p = jnp.exp(lg - m_new)
        alpha = jnp.exp(m_old - m_new)
        l_ref[...] = alpha * l_ref[...] + jnp.sum(p, axis=-1, keepdims=True)
        pb = p.reshape(N_HEADS * SUBLANES, n * PAGE_SIZE).astype(bf16)
        pv = lax.dot_general(pb, vcat, (((1,), (1,)), ((), ())), preferred_element_type=f32)
        acc_ref[...] = alpha * acc_ref[...] + pv.reshape(N_HEADS, SUBLANES, LANES)
        m_ref[...] = m_new

    @pl.when(s == 0)
    def _():
        m_ref[...] = jnp.full(m_ref.shape, NEG_BIG, f32)
        l_ref[...] = jnp.zeros(l_ref.shape, f32)
        acc_ref[...] = jnp.zeros(acc_ref.shape, f32)
        seen_ref[...] = jnp.zeros(seen_ref.shape, f32)

    @pl.when(s < last)
    def _():
        process(keys_ref[0], [kp[0] for kp in kpages], [vp[0] for vp in vpages])

    @pl.when(s == last)
    def _():
        process(keys_ref[0, :, 0:PAGE_SIZE], [knew_ref[0]], [vnew_ref[0]])
        o_ref[0] = (acc_ref[...] / l_ref[...]).reshape(N_HEADS * SUBLANES, LANES)


def dsa_sample(q, qi, sm, k, v, cache_k, cache_v, cache_idx_k, page_table):
    b, t = q.shape[:2]
    assert t <= SUBLANES
    n_pages = page_table.shape[1]
    g = min(PAGES_PER_STEP, n_pages)
    assert n_pages % g == 0
    ns = n_pages // g
    past = n_pages * PAGE_SIZE
    n_sel = min(TOPK_MAX, (past + t) // 4)
    pt = page_table.reshape(-1).astype(i32)
    padt = lambda a: jnp.pad(a, ((0, 0), (0, SUBLANES - t)) + ((0, 0),) * (a.ndim - 2))

    qi_r = padt(qi.reshape(b, t, IDX_HEADS, IDX_DIM)).transpose(0, 2, 1, 3).reshape(b, IDX_HEADS * SUBLANES, IDX_DIM)
    w_r = padt(sm[:, :, SM_WI:SM_WI + IDX_HEADS]).transpose(0, 2, 1).reshape(b, IDX_HEADS * SUBLANES, 1)
    padk = lambda a: jnp.pad(a, ((0, 0), (0, PAGE_SIZE - t), (0, 0))).transpose(0, 2, 1)
    ki_new = padk(sm[:, :, :IDX_DIM])
    k_new = padk(k)
    v_new = padk(v)
    qh = padt(q.reshape(b, t, N_HEADS, HEAD_DIM)).transpose(0, 2, 1, 3) * (HEAD_DIM ** -0.5)
    grp = (jnp.arange(N_HEADS) // GROUP)[None, :, None, None]
    q_r = jnp.concatenate([jnp.where(grp == 0, qh, 0.0), jnp.where(grp == 1, qh, 0.0)], axis=-1)
    q_r = q_r.reshape(b, N_HEADS * SUBLANES, LANES).astype(bf16)
    n_pool = cache_k.shape[0]
    ck = cache_k.transpose(0, 2, 3, 1).reshape(n_pool, N_KV_HEADS * HEAD_DIM, PAGE_SIZE)
    cv = cache_v.transpose(0, 2, 3, 1).reshape(n_pool, N_KV_HEADS * HEAD_DIM, PAGE_SIZE)
    cki = cache_idx_k.transpose(0, 2, 1)

    def page_spec(j, w):
        return pl.BlockSpec((1, w, PAGE_SIZE),
                            lambda bi, s, ptr: (ptr[bi * n_pages + jnp.minimum(s, ns - 1) * g + j], 0, 0))

    per_b = lambda a: pl.BlockSpec((1,) + a.shape[1:], lambda bi, s, ptr: (bi,) + (0,) * (a.ndim - 1))
    blk_w = g * PAGE_SIZE
    keys_spec = pl.BlockSpec((1, SUBLANES, blk_w), lambda bi, s, ptr: (bi, 0, s))
    width = past + blk_w

    keys = pl.pallas_call(
        functools.partial(_ds_score_kernel, g=g),
        grid_spec=pltpu.PrefetchScalarGridSpec(
            num_scalar_prefetch=1, grid=(b, ns + 1),
            in_specs=[per_b(qi_r), per_b(w_r), per_b(ki_new)] + [page_spec(j, IDX_DIM) for j in range(g)],
            out_specs=keys_spec),
        out_shape=jax.ShapeDtypeStruct((b, SUBLANES, width), i32),
        compiler_params=_cparams(("parallel", "arbitrary")),
        name="dsa_sample_scores",
    )(pt, qi_r.astype(bf16), w_r, ki_new, *([cki] * g))

    rows = b * SUBLANES
    thr, need = pl.pallas_call(
        functools.partial(_ds_thr_kernel, n_sel=n_sel, chunk_w=blk_w),
        out_shape=[jax.ShapeDtypeStruct((rows, LANES), i32), jax.ShapeDtypeStruct((rows, LANES), f32)],
        compiler_params=pltpu.CompilerParams(vmem_limit_bytes=VMEM_LIMIT),
        name="dsa_sample_threshold",
    )(keys.reshape(rows, width))
    thr = thr.reshape(b, SUBLANES, LANES)
    need = need.reshape(b, SUBLANES, LANES)

    out = pl.pallas_call(
        functools.partial(_ds_attn_kernel, g=g),
        grid_spec=pltpu.PrefetchScalarGridSpec(
            num_scalar_prefetch=1, grid=(b, ns + 1),
            in_specs=[per_b(q_r), keys_spec, per_b(thr), per_b(need), per_b(k_new), per_b(v_new)]
            + [page_spec(j, LANES) for j in range(g)] * 2,
            out_specs=per_b(q_r),
            scratch_shapes=[pltpu.VMEM((N_HEADS, SUBLANES, 1), f32), pltpu.VMEM((N_HEADS, SUBLANES, 1), f32),
                            pltpu.VMEM((N_HEADS, SUBLANES, LANES), f32), pltpu.VMEM((SUBLANES, 1), f32)]),
        out_shape=jax.ShapeDtypeStruct((b, N_HEADS * SUBLANES, LANES), f32),
        compiler_params=_cparams(("parallel", "arbitrary")),
        name="dsa_sample_attention",
    )(pt, q_r, keys, thr, need, k_new, v_new, *([ck] * g), *([cv] * g))
    out = out.reshape(b, N_HEADS, SUBLANES, N_KV_HEADS, HEAD_DIM)[:, :, :t]
    out = jnp.concatenate([out[:, :GROUP, :, 0], out[:, GROUP:, :, 1]], axis=1)
    return out.transpose(0, 2, 1, 3).reshape(b, t, A_WIDTH)


def _mid_kernel(x_ref, oa_ref, ob_ref, ga_ref, gb_ref, wa_ref, wb_ref, wo_ref, gf_ref, wrh_ref, wrl_ref, br_ref,
                x1_ref, h2_ref, te_ref, tg_ref):
    a = jnp.dot(oa_ref[...].astype(bf16), wa_ref[...], preferred_element_type=f32)
    b = jnp.dot(ob_ref[...].astype(bf16), wb_ref[...], preferred_element_type=f32)
    merged = ga_ref[...] * a + gb_ref[...] * b
    x1 = x_ref[...] + jnp.dot(merged.astype(bf16), wo_ref[...], preferred_element_type=f32)
    x1_ref[...] = x1
    h2 = _rms(x1, gf_ref[...])
    h2_ref[...] = h2
    hi = h2.astype(bf16)
    lo = (h2 - hi.astype(f32)).astype(bf16)
    lg = (jnp.dot(hi, wrh_ref[...], preferred_element_type=f32) + jnp.dot(lo, wrh_ref[...], preferred_element_type=f32)
          + jnp.dot(hi, wrl_ref[...], preferred_element_type=f32)) + br_ref[...]
    lane = lax.broadcasted_iota(i32, lg.shape, 1)
    lane_f = lane.astype(f32)
    vals, ids = [], []
    for _ in range(TOP_K):
        m = jnp.max(lg, axis=1, keepdims=True)
        idx = jnp.min(jnp.where(lg == m, lane_f, float(LANES)), axis=1, keepdims=True).astype(i32)
        vals.append(m)
        ids.append(idx)
        lg = jnp.where(lane == idx, -jnp.inf, lg)
    ex = [jnp.exp(v - vals[0]) for v in vals]
    tot = ex[0] + ex[1] + ex[2] + ex[3]
    te = jnp.zeros(lg.shape, i32)
    tg = jnp.zeros(lg.shape, f32)
    for j in range(TOP_K):
        te = jnp.where(lane == j, ids[j], te)
        tg = jnp.where(lane == j, ex[j] / tot, tg)
    te_ref[...] = te
    tg_ref[...] = tg


def mid(x, oa, ob, ga, gb, wa, wb, wo, g_ffn, wr_hi, wr_lo, br):
    n = x.shape[0]
    tm = min(256, n)
    assert n % tm == 0
    row = lambda w: pl.BlockSpec((tm, w), lambda i: (i, 0))
    full = lambda a: pl.BlockSpec(a.shape, lambda i: (0,) * a.ndim)
    return pl.pallas_call(
        _mid_kernel,
        grid=(n // tm,),
        in_specs=[row(D_MODEL), row(A_WIDTH), row(M_WIDTH), row(D_MODEL), row(D_MODEL),
                  full(wa), full(wb), full(wo), full(g_ffn), full(wr_hi), full(wr_lo), full(br)],
        out_specs=[row(D_MODEL), row(D_MODEL), row(LANES), row(LANES)],
        out_shape=[jax.ShapeDtypeStruct((n, D_MODEL), f32), jax.ShapeDtypeStruct((n, D_MODEL), f32),
                   jax.ShapeDtypeStruct((n, LANES), i32), jax.ShapeDtypeStruct((n, LANES), f32)],
        compiler_params=_cparams(("parallel",)),
        name="mid",
    )(x, oa, ob, ga, gb, wa, wb, wo, g_ffn, wr_hi, wr_lo, br)


def _rank_kernel(te_ref, dest_ref, meta_ref, cnt_ref, carry_ref, *, blk):
    ph = pl.program_id(0)
    i = pl.program_id(1)
    tm = te_ref.shape[0]
    lane = lax.broadcasted_iota(i32, (tm, LANES), 1)
    te = te_ref[...]
    oh = jnp.zeros((tm, LANES), f32)
    for j in range(TOP_K):
        oh = oh + jnp.where(lane == te[:, j:j + 1], 1.0, 0.0)
    tile_cnt = jnp.sum(oh, axis=0, keepdims=True)

    @pl.when((ph == 0) & (i == 0))
    def _():
        cnt_ref[...] = jnp.zeros(cnt_ref.shape, f32)

    @pl.when(ph == 0)
    def _():
        cnt_ref[...] += tile_cnt

    @pl.when((ph == 1) & (i == 0))
    def _():
        cnt = cnt_ref[...]
        padded = jnp.floor((cnt + (blk - 1)) / blk) * blk
        r = lax.broadcasted_iota(i32, (LANES, LANES), 0)
        c = lax.broadcasted_iota(i32, (LANES, LANES), 1)
        col = jnp.sum(jnp.where(r == c, padded, 0.0), axis=1, keepdims=True)
        start = jnp.sum(jnp.where(r < c, col, 0.0), axis=0, keepdims=True)
        carry_ref[...] = start
        pad_end = start + padded
        nbp = meta_ref.shape[0]
        jb = (lax.broadcasted_iota(i32, (nbp, LANES), 0) * blk).astype(f32)
        lane2 = lax.broadcasted_iota(i32, (nbp, LANES), 1)
        be = jnp.sum(jnp.where((pad_end <= jb) & (lane2 < N_EXPERTS), 1.0, 0.0), axis=1, keepdims=True)
        be = jnp.minimum(be, float(N_EXPERTS - 1))
        n_act = jnp.sum(jnp.where(lane2 == N_EXPERTS - 1, pad_end, 0.0), axis=1, keepdims=True) / blk
        meta_ref[...] = jnp.where(lane2 == 0, be, jnp.where(lane2 == 1, n_act, 0.0)).astype(i32)

    @pl.when(ph == 1)
    def _():
        r = lax.broadcasted_iota(i32, (tm, tm), 0)
        c = lax.broadcasted_iota(i32, (tm, tm), 1)
        before = jnp.where(c < r, 1.0, 0.0).astype(bf16)
        pos = carry_ref[...] + jnp.dot(before, oh.astype(bf16), preferred_element_type=f32)
        d = jnp.zeros((tm, LANES), f32)
        for j in range(TOP_K):
            dj = jnp.sum(jnp.where(lane == te[:, j:j + 1], pos, 0.0), axis=1, keepdims=True)
            d = jnp.where(lane == j, dj, d)
        dest_ref[...] = d.astype(i32)
        carry_ref[...] += tile_cnt


def rank(te, blk, n_blocks):
    n = te.shape[0]
    tm = next(c for c in (256, 128, 64, 32, 16, SUBLANES) if n % c == 0)
    nbp = -(-n_blocks // SUBLANES) * SUBLANES
    return pl.pallas_call(
        functools.partial(_rank_kernel, blk=blk),
        grid=(2, n // tm),
        in_specs=[pl.BlockSpec((tm, LANES), lambda ph, i: (i, 0))],
        out_specs=[pl.BlockSpec((tm, LANES), lambda ph, i: (i * ph, 0)),
                   pl.BlockSpec((nbp, LANES), lambda ph, i: (0, 0))],
        out_shape=[jax.ShapeDtypeStruct((n, LANES), i32), jax.ShapeDtypeStruct((nbp, LANES), i32)],
        scratch_shapes=[pltpu.VMEM((1, LANES), f32), pltpu.VMEM((1, LANES), f32)],
        compiler_params=_cparams(("arbitrary", "arbitrary")),
        name="rank",
    )(te)


def _ffn_kernel(be_ref, nact_ref, rt_ref, x_hbm, *rest):
    wgu_refs = rest[:W_SPLIT]
    bgu_ref = rest[W_SPLIT]
    wdn_refs = rest[W_SPLIT + 1:2 * W_SPLIT + 1]
    bdn_ref, y_ref, xbuf, sem, wgu_bf, wdn_bf = rest[2 * W_SPLIT + 1:]
    i = pl.program_id(0)
    n_act = nact_ref[0]
    blk = xbuf.shape[1]

    def row_copy(b, r, slot):
        tok = rt_ref[b * blk + r]
        return pltpu.make_async_copy(x_hbm.at[pl.ds(tok, 1)], xbuf.at[slot, pl.ds(r, 1)], sem.at[slot])

    def wait_slot(slot):
        pltpu.make_async_copy(xbuf.at[slot], xbuf.at[slot], sem.at[slot]).wait()

    @pl.when(i == 0)
    def _():
        def body(r, c):
            row_copy(0, r, 0).start()
            return c
        lax.fori_loop(0, blk, body, 0)

    @pl.when(i == n_act)
    def _():
        wait_slot(i % 2)

    @pl.when(i >= n_act)
    def _():
        y_ref[...] = jnp.zeros(y_ref.shape, f32)

    @pl.when(i < n_act)
    def _():
        slot = i % 2
        wait_slot(slot)

        @pl.when((i == 0) | (be_ref[i] != be_ref[jnp.maximum(i - 1, 0)]))
        def _():
            wg = wgu_bf.shape[1] // W_SPLIT
            wd = wdn_bf.shape[1] // W_SPLIT
            for c in range(W_SPLIT):
                wgu_bf[:, c * wg:(c + 1) * wg] = wgu_refs[c][0].astype(bf16)
                wdn_bf[:, c * wd:(c + 1) * wd] = wdn_refs[c][0].astype(bf16)

        for r in range(blk):
            row_copy(i + 1, r, 1 - slot).start()
        x = xbuf[slot].astype(bf16)
        gu = jnp.dot(x, wgu_bf[...], preferred_element_type=f32) + bgu_ref[0]
        gate = jnp.minimum(gu[:, :EXPERT_FF], SWIGLU_LIMIT)
        up = jnp.clip(gu[:, EXPERT_FF:], -SWIGLU_LIMIT, SWIGLU_LIMIT)
        act = (up + 1.0) * (gate * _sigmoid(SWIGLU_ALPHA * gate))
        y_ref[...] = jnp.dot(act.astype(bf16), wdn_bf[...], preferred_element_type=f32) + bdn_ref[0]


def ffn(blk_e, n_act, row_tok, h2, w_gu, b_gu, w_dn, b_dn, blk, n_blocks):
    d = h2.shape[1]
    ff2 = w_gu.shape[2]
    return pl.pallas_call(
        _ffn_kernel,
        grid_spec=pltpu.PrefetchScalarGridSpec(
            num_scalar_prefetch=3, grid=(n_blocks,),
            in_specs=[pl.BlockSpec(memory_space=pl.ANY)]
            + [pl.BlockSpec((1, d, ff2 // W_SPLIT), functools.partial(lambda i, be, na, rt, c: (be[i], 0, c), c=c))
               for c in range(W_SPLIT)]
            + [pl.BlockSpec((1, 1, ff2), lambda i, be, na, rt: (be[i], 0, 0))]
            + [pl.BlockSpec((1, ff2 // 2, d // W_SPLIT), functools.partial(lambda i, be, na, rt, c: (be[i], 0, c), c=c))
               for c in range(W_SPLIT)]
            + [pl.BlockSpec((1, 1, d), lambda i, be, na, rt: (be[i], 0, 0))],
            out_specs=pl.BlockSpec((blk, d), lambda i, be, na, rt: (i, 0)),
            scratch_shapes=[pltpu.VMEM((2, blk, d), f32), pltpu.SemaphoreType.DMA((2,)),
                            pltpu.VMEM((d, ff2), bf16), pltpu.VMEM((ff2 // 2, d), bf16)]),
        out_shape=jax.ShapeDtypeStruct((n_blocks * blk, d), f32),
        compiler_params=_cparams(("arbitrary",)),
        name="ffn",
    )(blk_e, n_act, row_tok, h2, *([w_gu] * W_SPLIT), b_gu.reshape(b_gu.shape[0], 1, ff2),
      *([w_dn] * W_SPLIT), b_dn.reshape(b_dn.shape[0], 1, d))


def _out_kernel(dest_ref, x1_ref, tg_ref, p_ref, ys_hbm, gp_ref, wg_ref, wp_ref, gfin_ref, y_ref, buf, sem):
    i = pl.program_id(0)
    n = pl.num_programs(0)
    tm = x1_ref.shape[0]

    def row_copy(t, r, j, slot):
        d = dest_ref[(t * tm + r) * TOP_K + j]
        return pltpu.make_async_copy(ys_hbm.at[pl.ds(d, 1)], buf.at[slot, j, pl.ds(r, 1)], sem.at[slot])

    def wait_slot(slot):
        pltpu.make_async_copy(buf.at[slot], buf.at[slot], sem.at[slot]).wait()

    @pl.when(i == 0)
    def _():
        def body(r, c):
            for j in range(TOP_K):
                row_copy(0, r, j, 0).start()
            return c
        lax.fori_loop(0, tm, body, 0)

    slot = i % 2
    wait_slot(slot)
    nxt = jnp.minimum(i + 1, n - 1)
    for r in range(tm):
        for j in range(TOP_K):
            row_copy(nxt, r, j, 1 - slot).start()
    tg = tg_ref[...]
    x2 = x1_ref[...]
    for j in range(TOP_K):
        x2 = x2 + tg[:, j:j + 1] * buf[slot, j]
    hn = _rms(x2, gp_ref[...]).astype(bf16)
    gate = _sigmoid(jnp.dot(hn, wg_ref[...], preferred_element_type=f32))
    x3 = x2 + gate * jnp.dot(p_ref[...].astype(bf16), wp_ref[...], preferred_element_type=f32)
    y_ref[...] = _rms(x3, gfin_ref[...])

    @pl.when(i == n - 1)
    def _():
        wait_slot(1 - slot)


def out_stage(dest_flat, x1, tg, p, ys, g_ple, wg, wp, g_final):
    n, d = x1.shape
    tm = min(OUT_TM, n)
    assert n % tm == 0
    row = lambda w: pl.BlockSpec((tm, w), lambda i, ds: (i, 0))
    full = lambda a: pl.BlockSpec(a.shape, lambda i, ds: (0,) * a.ndim)
    return pl.pallas_call(
        _out_kernel,
        grid_spec=pltpu.PrefetchScalarGridSpec(
            num_scalar_prefetch=1, grid=(n // tm,),
            in_specs=[row(d), row(LANES), row(p.shape[1]), pl.BlockSpec(memory_space=pl.ANY),
                      full(g_ple), full(wg), full(wp), full(g_final)],
            out_specs=row(d),
            scratch_shapes=[pltpu.VMEM((2, TOP_K, tm, d), f32), pltpu.SemaphoreType.DMA((2,))]),
        out_shape=jax.ShapeDtypeStruct((n, d), f32),
        compiler_params=_cparams(("arbitrary",)),
        name="out",
    )(dest_flat, x1, tg, p, ys, g_ple, wg, wp, g_final)


def mid_stage(x, oa, ob, ga, gb, w):
    return mid(x, oa, ob, ga, gb, w["wa"], w["wb"], w["wo"], w["g_ffn"], w["wr_hi"], w["wr_lo"], w["br"])


def moe_out(parts, w):
    h2 = jnp.concatenate([pt[1] for pt in parts], axis=0)
    te = jnp.concatenate([pt[2] for pt in parts], axis=0)
    n = h2.shape[0]
    blk = FFN_BLK
    n_blocks = -(-(n * TOP_K) // blk) + N_EXPERTS
    dest, meta = rank(te, blk, n_blocks)
    dest_flat = dest[:, :TOP_K].reshape(-1)
    blk_e = meta[:n_blocks, 0]
    n_act = meta[0:1, 1]
    tok = jnp.repeat(jnp.arange(n, dtype=i32), TOP_K)
    row_tok = jnp.zeros((n_blocks * blk,), i32).at[dest_flat].set(tok)
    ys = ffn(blk_e, n_act, row_tok, h2, w["w_gu"], w["b_gu"], w["w_dn"], w["b_dn"], blk, n_blocks)
    outs, off = [], 0
    for x1, _, _, tg, p in parts:
        ni = x1.shape[0]
        outs.append(out_stage(dest_flat[off * TOP_K:(off + ni) * TOP_K], x1, tg, p, ys,
                              w["g_ple"], w["wg"], w["wp"], w["g_final"]))
        off += ni
    return outs


def tail(x, oa, ob, ga, gb, p, w):
    return moe_out([mid_stage(x, oa, ob, ga, gb, w) + (p,)], w)[0]


def _prep_weights(g_mix, w_in, b_gate, g_mnorm, w_up_a, w_up_b, w_out, g_ffn, w_router, b_router,
                  w_gu, b_gu, w_dn, b_dn, g_ple, w_ple_gate, w_ple_proj, g_final):
    wr = jnp.pad(w_router.astype(f32), ((0, 0), (0, LANES - N_EXPERTS)))
    wr_hi = wr.astype(bf16)
    wr_lo = (wr - wr_hi.astype(f32)).astype(bf16)
    br = jnp.full((1, LANES), -jnp.inf, f32).at[0, :N_EXPERTS].set(b_router.astype(f32))
    bias = jnp.zeros((1, LANES), f32).at[0, SM_IG:SM_IG + 2 * M_HEADS].set(b_gate.astype(f32))
    return dict(g_mix=g_mix.reshape(1, -1), w_in=_pack_w_in(w_in), bias=bias, g_mnorm=g_mnorm,
                wa=w_up_a.astype(bf16), wb=w_up_b.astype(bf16), wo=w_out.astype(bf16), g_ffn=g_ffn.reshape(1, -1),
                wr_hi=wr_hi, wr_lo=wr_lo, br=br, w_gu=w_gu, b_gu=b_gu, w_dn=w_dn, b_dn=b_dn,
                g_ple=g_ple.reshape(1, -1), wg=w_ple_gate.astype(bf16), wp=w_ple_proj.astype(bf16),
                g_final=g_final.reshape(1, -1))


def _layer(x, p, pos, attn_fn, c0, n0, m0, w):
    b, t = x.shape[:2]
    n = b * t
    cos, sin = _rope_tables(pos)
    cos = jnp.tile(cos, (b, 1))
    sin = jnp.tile(sin, (b, 1))
    x2d = x.reshape(n, D_MODEL)
    q, k, v, qi, sm, mq, mk, mv, og, ga, gb = in_proj(x2d, cos, sin, w["g_mix"], w["w_in"], w["bias"])
    r3 = lambda a: a.reshape(b, t, a.shape[-1])
    o_a = attn_fn(r3(q), r3(qi), r3(sm), r3(k), r3(v))
    o_b, c, nn, m = mlstm_branch(r3(mq), r3(mk), r3(mv), r3(og), r3(sm), c0, n0, m0, w["g_mnorm"])
    part = mid_stage(x2d, o_a.reshape(n, A_WIDTH), o_b.reshape(n, M_WIDTH), ga, gb, w) + (p.reshape(n, PLE_DIM),)
    state = (k.reshape(b, t, N_KV_HEADS, HEAD_DIM), v.reshape(b, t, N_KV_HEADS, HEAD_DIM),
             sm[:, :IDX_DIM].reshape(b, t, IDX_DIM), c, nn, m)
    return part, state


def kernel(x_prompt, x_sample, cache_k, cache_v, cache_idx_k, state_C, state_n, state_m, page_table,
           p_prompt, p_sample, g_mix, w_in, b_gate, g_mnorm, w_up_a, w_up_b, w_out, g_ffn,
           w_router, b_router, w_gu, b_gu, w_dn, b_dn, g_ple, w_ple_gate, w_ple_proj, g_final):
    assert x_prompt.shape[-1] == D_MODEL and w_in.shape[0] == 1, "single-layer model of width D_MODEL"
    bp, tp = x_prompt.shape[:2]
    ts = x_sample.shape[1]
    past = page_table.shape[1] * PAGE_SIZE
    w = _prep_weights(g_mix[0], w_in[0], b_gate[0], g_mnorm[0], w_up_a[0], w_up_b[0], w_out[0], g_ffn[0],
                      w_router[0], b_router[0], w_gu[0], b_gu[0], w_dn[0], b_dn[0], g_ple[0],
                      w_ple_gate[0], w_ple_proj[0], g_final)
    zeros = lambda *s: jnp.zeros(s, f32)
    part_p, sp = _layer(x_prompt, p_prompt[0], jnp.arange(tp, dtype=i32), dsa_prompt_branch,
                        zeros(bp, M_HEADS, M_V, M_QK), zeros(bp, M_HEADS, M_QK), zeros(bp, M_HEADS), w)
    attn_s = functools.partial(dsa_sample, cache_k=cache_k[0], cache_v=cache_v[0], cache_idx_k=cache_idx_k[0],
                               page_table=page_table)
    part_s, ss = _layer(x_sample, p_sample[0], past + jnp.arange(ts, dtype=i32), attn_s,
                        state_C[0], state_n[0], state_m[0], w)
    yp, ys = moe_out([part_p, part_s], w)
    return ((yp.reshape(x_prompt.shape), ys.reshape(x_sample.shape))
            + tuple(s[None] for s in sp) + tuple(s[None] for s in ss))
```

```python
import functools

import jax
import jax.numpy as jnp
from jax import lax
from jax.experimental import pallas as pl
from jax.experimental.pallas import tpu as pltpu

f32 = jnp.float32
bf16 = jnp.bfloat16
i32 = jnp.int32

D_MODEL = 1024
PAGE_SIZE = 128
N_HEADS = 8
N_KV_HEADS = 2
HEAD_DIM = 64
GROUP = N_HEADS // N_KV_HEADS
IDX_HEADS = 4
IDX_DIM = 64
TOPK_MAX = 256
ROPE_THETA = 10000.0
M_HEADS = 4
M_QK = 64
M_V = 128
N_EXPERTS = 32
TOP_K = 4
EXPERT_FF = D_MODEL
SWIGLU_LIMIT = 7.0
SWIGLU_ALPHA = 1.702
PLE_DIM = 256
EPS = 1e-6
A_WIDTH = N_HEADS * HEAD_DIM
M_WIDTH = M_HEADS * M_V

LANES = 128
SUBLANES = 8
VMEM_LIMIT = 56 * 1024 * 1024

M_CHUNK = 128
MLSTM_BATCH = 1
TQ = 128
KC = 1024
PAGES_PER_STEP = 16
FFN_BLK = 256
W_SPLIT = 4
OUT_TM = 128

INT_MIN = -2147483648
INT_MAX = 2147483647
NEG_BIG = -1e30
LOG2E = 1.4426950408889634
ROW_SLICE = 16

SM_KI = 0
SM_WI = IDX_DIM
SM_IG = SM_WI + IDX_HEADS
SM_LF = SM_IG + M_HEADS


def _cparams(sem):
    return pltpu.CompilerParams(dimension_semantics=sem, vmem_limit_bytes=VMEM_LIMIT)


def _rms(x, g):
    return x * lax.rsqrt(jnp.mean(x * x, axis=-1, keepdims=True) + EPS) * g


def _sigmoid(x):
    return 1.0 / (1.0 + jnp.exp(-x))


def _log_sigmoid(x):
    return jnp.minimum(x, 0.0) - jnp.log1p(jnp.exp(-jnp.abs(x)))


def _sortable(x):
    bits = lax.bitcast_convert_type(x, i32)
    key = bits ^ ((bits >> 31) & INT_MAX)
    return jnp.where(x == 0.0, 0, key)


_G_Q = (0, 512)
_G_K = (512, 640)
_G_V = (640, 768)
_G_QI = (768, 1024)
_G_SM = (1024, 1152)
_G_MQ = (1152, 1408)
_G_MK = (1408, 1664)
_G_MV = (1664, 2176)
_G_MO = (2176, 2688)
_G_GA = (2688, 3712)
_G_GB = (3712, 4736)
_W_COLS = 4736


def _pack_w_in(w_in):
    o = [0]
    for s in (A_WIDTH, 128, 128, 256, 64, 4, 256, 256, 512, 4, 4, 512, 1024, 1024):
        o.append(o[-1] + s)
    aq, ak, av, iq, ik, iw, mq, mk, mv, mi, mf, mo, ga, gb = [w_in[:, o[i]:o[i + 1]] for i in range(14)]
    pad = jnp.zeros((w_in.shape[0], LANES - (IDX_DIM + IDX_HEADS + 2 * M_HEADS)), w_in.dtype)
    small = jnp.concatenate([ik, iw, mi, mf, pad], axis=1)
    w = jnp.concatenate([aq, ak, av, iq, small, mq, mk, mv, mo, ga, gb], axis=1)
    return w.astype(bf16)


def _in_kernel(x_ref, cos_ref, sin_ref, g_ref, w_ref, bias_ref,
               q_ref, k_ref, v_ref, qi_ref, sm_ref, mq_ref, mk_ref, mv_ref, og_ref, ga_ref, gb_ref):
    x = x_ref[...]
    hb = _rms(x, g_ref[...]).astype(bf16)
    cos = cos_ref[...]
    sin = sin_ref[...]
    tm = x.shape[0]
    lane = lax.broadcasted_iota(i32, (tm, LANES), 1)
    first_half = (lane % HEAD_DIM) < (HEAD_DIM // 2)

    def rope(z):
        rot = jnp.where(first_half, pltpu.roll(z, LANES - HEAD_DIM // 2, 1), pltpu.roll(z, HEAD_DIM // 2, 1))
        return z * cos + rot * sin

    def proj(grp):
        return jnp.dot(hb, w_ref[:, grp[0]:grp[1]], preferred_element_type=f32)

    z = proj(_G_Q)
    for j in range(4):
        q_ref[:, j * LANES:(j + 1) * LANES] = rope(z[:, j * LANES:(j + 1) * LANES])
    k_ref[...] = rope(proj(_G_K))
    v_ref[...] = proj(_G_V)
    z = proj(_G_QI)
    for j in range(2):
        qi_ref[:, j * LANES:(j + 1) * LANES] = rope(z[:, j * LANES:(j + 1) * LANES])
    z = proj(_G_SM)
    zb = z + bias_ref[...]
    sm = jnp.where(lane < SM_WI, rope(z),
                   jnp.where(lane < SM_IG, z * (IDX_HEADS ** -0.5 * IDX_DIM ** -0.5),
                             jnp.where(lane < SM_LF, zb,
                                       jnp.where(lane < SM_LF + M_HEADS, _log_sigmoid(zb), 0.0))))
    sm_ref[...] = sm
    mq_ref[...] = proj(_G_MQ)
    mk_ref[...] = proj(_G_MK) * (M_QK ** -0.5)
    mv_ref[...] = proj(_G_MV)
    og_ref[...] = _sigmoid(proj(_G_MO))
    ga_ref[...] = _sigmoid(proj(_G_GA))
    gb_ref[...] = _sigmoid(proj(_G_GB))


def _rope_tables(pos):
    half = HEAD_DIM // 2
    inv = ROPE_THETA ** (-jnp.arange(half, dtype=f32) / half)
    ang = pos.astype(f32)[:, None] * inv[None, :]
    cos = jnp.cos(ang)
    sin = jnp.sin(ang)
    cos128 = jnp.tile(cos, (1, 4))
    sin128 = jnp.tile(jnp.concatenate([-sin, sin], axis=1), (1, 2))
    return cos128, sin128


def in_proj(x2d, cos128, sin128, g_mix, w_packed, bias128):
    n = x2d.shape[0]
    tm = min(256, n)
    assert n % tm == 0
    row = lambda w: pl.BlockSpec((tm, w), lambda i: (i, 0))
    full = lambda a: pl.BlockSpec(a.shape, lambda i: (0,) * a.ndim)
    widths = (512, 128, 128, 256, 128, 256, 256, 512, 512, 1024, 1024)
    return pl.pallas_call(
        _in_kernel,
        grid=(n // tm,),
        in_specs=[row(D_MODEL), row(LANES), row(LANES), full(g_mix), full(w_packed), full(bias128)],
        out_specs=[row(w) for w in widths],
        out_shape=[jax.ShapeDtypeStruct((n, w), f32) for w in widths],
        compiler_params=_cparams(("parallel",)),
        name="in_proj",
    )(x2d, cos128, sin128, g_mix, w_packed, bias128)


def _mlstm_kernel(mq_ref, mk_ref, mv_ref, og_ref, sm_ref, gr_ref, c0_ref, n0_ref, m0_ref, gn_ref,
                  ob_ref, c_ref, n_ref, m_ref):
    ci = pl.program_id(1)
    L = mq_ref.shape[1]

    @pl.when(ci == 0)
    def _():
        c_ref[...] = c0_ref[...]
        n_ref[...] = n0_ref[...]
        m_ref[...] = m0_ref[...]

    row = lax.broadcasted_iota(i32, (L, L), 0)
    col = lax.broadcasted_iota(i32, (L, L), 1)
    tril = row >= col
    for bi, hd in [(bi, hd) for bi in range(mq_ref.shape[0]) for hd in range(M_HEADS)]:
        sm = sm_ref[bi]
        gr = gr_ref[bi]
        q = mq_ref[bi, :, hd * M_QK:(hd + 1) * M_QK]
        k = mk_ref[bi, :, hd * M_QK:(hd + 1) * M_QK]
        v = mv_ref[bi, :, hd * M_V:(hd + 1) * M_V]
        ig_r = gr[hd:hd + 1, :]
        lf_r = gr[M_HEADS + hd:M_HEADS + hd + 1, :]
        ig_c = sm[:, SM_IG + hd:SM_IG + hd + 1]
        lf_c = sm[:, SM_LF + hd:SM_LF + hd + 1]
        C = c_ref[bi, hd]
        nrow = n_ref[bi, hd]
        m_prev = m_ref[bi, hd]
        b_c = jnp.sum(jnp.where(tril, lf_r, 0.0), axis=1, keepdims=True)
        b_r = jnp.sum(jnp.where(tril, 0.0, lf_c) + jnp.where(row == col, lf_c, 0.0), axis=0, keepdims=True)
        dmat = jnp.where(tril, b_c - b_r + ig_r, -jnp.inf)
        inter = b_c + m_prev
        m_t = jnp.maximum(inter, jnp.max(dmat, axis=1, keepdims=True))
        qb = q.astype(bf16)
        kb = k.astype(bf16)
        qk = lax.dot_general(qb, kb, (((1,), (1,)), ((), ())), preferred_element_type=f32)
        s = qk * jnp.exp(dmat - m_t)
        w_inter = jnp.exp(inter - m_t)
        qc = lax.dot_general(qb, C.astype(bf16), (((1,), (1,)), ((), ())), preferred_element_type=f32)
        num = jnp.dot(s.astype(bf16), v.astype(bf16), preferred_element_type=f32) + w_inter * qc
        den = jnp.sum(s, axis=1, keepdims=True) + w_inter * jnp.sum(q * nrow, axis=1, keepdims=True)
        h = num / jnp.maximum(jnp.abs(den), jnp.exp(-m_t))
        b_last = b_c[L - 1:L, :]
        g_c = b_last - b_c + ig_c
        m_new = jnp.maximum(b_last + m_prev, jnp.max(g_c, axis=0, keepdims=True))
        w_k = jnp.exp(g_c - m_new)
        decay = jnp.exp(b_last + m_prev - m_new)
        wv = (w_k * v).astype(bf16)
        c_ref[bi, hd] = decay * C + lax.dot_general(wv, kb, (((0,), (0,)), ((), ())), preferred_element_type=f32)
        n_ref[bi, hd] = decay * nrow + jnp.sum(w_k * k, axis=0, keepdims=True)
        m_ref[bi, hd] = m_new
        gn = gn_ref[:, hd * M_V:(hd + 1) * M_V]
        ob_ref[bi, :, hd * M_V:(hd + 1) * M_V] = og_ref[bi, :, hd * M_V:(hd + 1) * M_V] * _rms(h, gn)


def mlstm(mq, mk, mv, og, sm, grow, c0, n0, m0, g_mnorm):
    b, t = mq.shape[:2]
    L = M_CHUNK
    bb = MLSTM_BATCH if b % MLSTM_BATCH == 0 else 1
    assert t % L == 0
    tok = lambda w: pl.BlockSpec((bb, L, w), lambda bi, ci: (bi, ci, 0))
    st = lambda a: pl.BlockSpec((bb,) + a.shape[1:], lambda bi, ci: (bi,) + (0,) * (a.ndim - 1))
    return pl.pallas_call(
        _mlstm_kernel,
        grid=(b // bb, t // L),
        in_specs=[tok(256), tok(256), tok(512), tok(512), tok(LANES),
                  pl.BlockSpec((bb, 2 * M_HEADS, L), lambda bi, ci: (bi, 0, ci)),
                  st(c0), st(n0), st(m0), pl.BlockSpec(g_mnorm.shape, lambda bi, ci: (0, 0))],
        out_specs=[tok(512), st(c0), st(n0), st(m0)],
        out_shape=[jax.ShapeDtypeStruct((b, t, M_WIDTH), f32), jax.ShapeDtypeStruct(c0.shape, f32),
                   jax.ShapeDtypeStruct(n0.shape, f32), jax.ShapeDtypeStruct(m0.shape, f32)],
        compiler_params=_cparams(("parallel", "arbitrary")),
        name="mlstm",
    )(mq, mk, mv, og, sm, grow, c0, n0, m0, g_mnorm)


def mlstm_branch(mq, mk, mv, og, sm, c0, n0, m0, g_mnorm):
    b, t = mq.shape[:2]
    tp = -(-t // M_CHUNK) * M_CHUNK
    if tp != t:
        pad = lambda a: jnp.pad(a, ((0, 0), (0, tp - t), (0, 0)))
        mq, mk, mv, og = pad(mq), pad(mk), pad(mv), pad(og)
        sm_pad = jnp.zeros((b, tp - t, LANES), f32).at[:, :, SM_IG:SM_IG + M_HEADS].set(NEG_BIG)
        sm = jnp.concatenate([sm, sm_pad], axis=1)
    grow = sm[:, :, SM_IG:SM_IG + 2 * M_HEADS].transpose(0, 2, 1)
    ob, c, n, m = mlstm(mq, mk, mv, og, sm, grow, c0.astype(f32), n0.astype(f32).reshape(b, M_HEADS, 1, M_QK),
                        m0.astype(f32).reshape(b, M_HEADS, 1, 1), g_mnorm.reshape(1, M_WIDTH))
    return ob[:, :t], c, n.reshape(b, M_HEADS, M_QK), m.reshape(b, M_HEADS)


def _select_threshold(get_chunk, n_chunks, chunk_w, rows, n_sel, dynamic):
    def count(pred):
        def body(c, acc):
            hit = jnp.where(pred(get_chunk(c)), 1.0, 0.0)
            for j in range(chunk_w // LANES):
                acc = acc + hit[:, j * LANES:(j + 1) * LANES]
            return acc
        acc0 = jnp.zeros((rows, LANES), f32)
        if dynamic:
            acc = lax.fori_loop(0, n_chunks, body, acc0)
        else:
            acc = acc0
            for c in range(n_chunks):
                acc = body(c, acc)
        return jnp.sum(acc, axis=1, keepdims=True)

    kf = float(n_sel)

    total = jnp.zeros((rows, 1), f32) + (n_chunks * chunk_w).astype(f32) if dynamic else \
        jnp.full((rows, 1), float(n_chunks * chunk_w), f32)
    c_nonneg = count(lambda keys: keys >= 0)
    c_pos = count(lambda keys: keys >= 1)
    take0 = c_nonneg >= kf
    zero_tie = take0 & (c_pos < kf)

    def unsettled(st):
        it, _, cnt = st
        return (it < 32) & (jnp.max(jnp.where(zero_tie, 0.0, jnp.abs(cnt - kf))) > 0.0)

    def bit_step(st):
        it, thr_u, cnt = st
        cand_u = thr_u | lax.shift_left(jnp.int32(1), 31 - it)
        cand_s = cand_u ^ INT_MIN
        c = count(lambda keys: keys >= cand_s)
        take = c >= kf
        return it + 1, jnp.where(take, cand_u, thr_u), jnp.where(take, c, cnt)

    state0 = (jnp.int32(1), jnp.where(take0, INT_MIN, 0).astype(i32), jnp.where(take0, c_nonneg, total))
    _, thr_u, _ = lax.while_loop(unsettled, bit_step, state0)
    thr = jnp.maximum(thr_u ^ INT_MIN, INT_MIN + 1)
    need = kf - count(lambda keys: keys > thr)
    return thr, need


def _tie_prefix_matrix():
    r = lax.broadcasted_iota(i32, (LANES, LANES), 0)
    c = lax.broadcasted_iota(i32, (LANES, LANES), 1)
    return jnp.where(r <= c, 1.0, 0.0).astype(bf16)


def _selected_blocks(key_blocks, thr, need, seen, tri):
    rows = key_blocks[0].shape[0]
    eqs = [kb == thr for kb in key_blocks]
    stack = jnp.concatenate([jnp.where(eq, 1.0, 0.0) for eq in eqs], axis=0).astype(bf16)
    pre = jnp.dot(stack, tri, preferred_element_type=f32)
    sels = []
    for j, (kb, eq) in enumerate(zip(key_blocks, eqs)):
        pj = pre[j * rows:(j + 1) * rows]
        sels.append((kb > thr) | (eq & ((seen + pj) <= need)))
        seen = seen + pj[:, LANES - 1:LANES]
    return sels, seen


def _dsa_prompt_kernel(q_ref, qi_ref, sm_ref, kit_ref, kt_ref, vd_ref, o_ref,
                       keys_ref, bias_ref, lg_ref, p_ref, m_ref, acc_ref, *, n_sel):
    qb = pl.program_id(1)
    tq = q_ref.shape[1]
    kc = kit_ref.shape[3]
    n_chunks = (qb * tq + tq - 1) // kc + 1
    lane = lax.broadcasted_iota(i32, (tq, LANES), 1)
    lo = lane < HEAD_DIM
    t_col = qb * tq + lax.broadcasted_iota(i32, (tq, 1), 0)
    sm = sm_ref[0]

    qi = qi_ref[0]
    qi_h = []
    for h in range(IDX_HEADS):
        blk = qi[:, (h // 2) * LANES:(h // 2 + 1) * LANES]
        qi_h.append(jnp.where(lo if h % 2 == 0 else ~lo, blk, 0.0).astype(bf16))
    w_h = [sm[:, SM_WI + h:SM_WI + h + 1] for h in range(IDX_HEADS)]

    def score_chunk(c, carry):
        kt = kit_ref[0, c]
        sc = jnp.zeros((tq, kc), f32)
        for h in range(IDX_HEADS):
            s = jnp.dot(qi_h[h], kt, preferred_element_type=f32)
            sc = sc + w_h[h] * jnp.maximum(s, 0.0)
        idx = c * kc + lax.broadcasted_iota(i32, (tq, kc), 1)
        keys_ref[c] = jnp.where(idx <= t_col, _sortable(sc), INT_MIN)
        return carry

    lax.fori_loop(0, n_chunks, score_chunk, 0)

    thr, need = _select_threshold(lambda c: keys_ref[c], n_chunks, kc, tq, n_sel, True)
    tri = _tie_prefix_matrix()

    q = q_ref[0] * (HEAD_DIM ** -0.5 * LOG2E)
    q_g = []
    for g in range(N_KV_HEADS):
        parts = []
        for j in range(GROUP):
            h = g * GROUP + j
            blk = q[:, (h // 2) * LANES:(h // 2 + 1) * LANES]
            parts.append(jnp.where(lo if h % 2 == 0 else ~lo, blk, 0.0).astype(bf16))
        q_g.append(jnp.concatenate(parts, axis=0))
    m_ref[...] = jnp.full(m_ref.shape, NEG_BIG, f32)
    acc_ref[...] = jnp.zeros(acc_ref.shape, f32)
    rs = min(ROW_SLICE, tq)

    def attend_chunk(c, seen):
        blocks = [keys_ref[c, :, j * LANES:(j + 1) * LANES] for j in range(kc // LANES)]
        sels, seen = _selected_blocks(blocks, thr, need, seen, tri)
        for j, sel in enumerate(sels):
            bias_ref[:, j * LANES:(j + 1) * LANES] = jnp.where(sel, 0.0, NEG_BIG)
        for g in range(N_KV_HEADS):
            lg_ref[...] = jnp.dot(q_g[g], kt_ref[0, g, c], preferred_element_type=f32)
            for r0 in range(0, GROUP * tq, rs):
                x = lg_ref[r0:r0 + rs] + bias_ref[r0 % tq:r0 % tq + rs]
                m_old = m_ref[g, r0:r0 + rs]
                m_new = jnp.maximum(m_old, jnp.max(x, axis=-1, keepdims=True))
                p_ref[r0:r0 + rs] = jnp.exp2(x - m_new).astype(bf16)
                acc_ref[g, r0:r0 + rs] = jnp.exp2(m_old - m_new) * acc_ref[g, r0:r0 + rs]
                m_ref[g, r0:r0 + rs] = m_new
            acc_ref[g] += jnp.dot(p_ref[...], vd_ref[0, g, c], preferred_element_type=f32)
        return seen

    lax.fori_loop(0, n_chunks, attend_chunk, jnp.zeros((tq, 1), f32))

    for g in range(N_KV_HEADS):
        acc = acc_ref[g]
        out = acc / acc[:, HEAD_DIM:HEAD_DIM + 1]
        for jp in range(GROUP // 2):
            even = out[(2 * jp) * tq:(2 * jp + 1) * tq]
            odd = pltpu.roll(out[(2 * jp + 1) * tq:(2 * jp + 2) * tq], HEAD_DIM, 1)
            o_ref[0, :, (g * 2 + jp) * LANES:(g * 2 + jp + 1) * LANES] = jnp.where(lo, even, odd)


def dsa_prompt(q, qi, sm, kit2, kt2, vd):
    b, t = q.shape[:2]
    nc, kc = kit2.shape[1], kit2.shape[3]
    tq = min(TQ, t)
    n_sel = min(TOPK_MAX, t // 4)
    tok = lambda w: pl.BlockSpec((1, tq, w), lambda bi, qb: (bi, qb, 0))
    return pl.pallas_call(
        functools.partial(_dsa_prompt_kernel, n_sel=n_sel),
        grid=(b, t // tq),
        in_specs=[tok(A_WIDTH), tok(IDX_HEADS * IDX_DIM), tok(LANES),
                  pl.BlockSpec((1, nc, LANES, kc), lambda bi, qb: (bi, 0, 0, 0)),
                  pl.BlockSpec((1, N_KV_HEADS, nc, LANES, kc), lambda bi, qb: (bi, 0, 0, 0, 0)),
                  pl.BlockSpec((1, N_KV_HEADS, nc, kc, LANES), lambda bi, qb: (bi, 0, 0, 0, 0))],
        out_specs=tok(A_WIDTH),
        out_shape=jax.ShapeDtypeStruct((b, t, A_WIDTH), f32),
        scratch_shapes=[pltpu.VMEM((nc, tq, kc), i32),
                        pltpu.VMEM((tq, kc), f32),
                        pltpu.VMEM((GROUP * tq, kc), f32),
                        pltpu.VMEM((GROUP * tq, kc), bf16),
                        pltpu.VMEM((N_KV_HEADS, GROUP * tq, 1), f32),
                        pltpu.VMEM((N_KV_HEADS, GROUP * tq, LANES), f32)],
        compiler_params=_cparams(("parallel", "arbitrary")),
        name="dsa_prompt",
    )(q, qi, sm, kit2, kt2, vd)


def dsa_prompt_branch(q, qi, sm, k, v):
    b, t = q.shape[:2]
    kc = min(KC, t)
    nc = t // kc
    kit = sm[:, :, :IDX_DIM].astype(bf16).reshape(b, nc, kc, IDX_DIM).transpose(0, 1, 3, 2)
    kit2 = jnp.concatenate([kit, kit], axis=2)
    kt = k.astype(bf16).reshape(b, nc, kc, N_KV_HEADS, HEAD_DIM).transpose(0, 3, 1, 4, 2)
    kt2 = jnp.concatenate([kt, kt], axis=3)
    vb = v.astype(bf16).reshape(b, nc, kc, N_KV_HEADS, HEAD_DIM).transpose(0, 3, 1, 2, 4)
    vd = jnp.concatenate([vb, jnp.ones_like(vb)], axis=4)
    return dsa_prompt(q, qi, sm, kit2, kt2, vd)


def _idx_scores(qi, w, ktpage):
    s = jnp.dot(qi, ktpage.astype(bf16), preferred_element_type=f32)
    r = (w * jnp.maximum(s, 0.0)).reshape(IDX_HEADS, SUBLANES, s.shape[1])
    sc = r[0]
    for h in range(1, IDX_HEADS):
        sc = sc + r[h]
    return sc


def _page_fetcher(pt_ref, srcs, bufs, sem, g):
    def start(step, slot):
        for j in range(g):
            pg = pt_ref[step * g + j]
            for i, (src, buf) in enumerate(zip(srcs, bufs)):
                pltpu.make_async_copy(src.at[pg], buf.at[slot, j], sem.at[i, slot]).start()

    def wait(slot):
        for i, buf in enumerate(bufs):
            pltpu.make_async_copy(buf.at[slot], buf.at[slot], sem.at[i, slot]).wait()

    return start, wait


def _ds_score_kernel(pt_ref, qi_ref, w_ref, kin_ref, cki_hbm, keys_ref, kbuf, sem, *, g):
    bi = pl.program_id(0)
    s = pl.program_id(1)
    last = pl.num_programs(1) - 1
    t = bi * last + s
    n_tot = pl.num_programs(0) * last
    qi = qi_ref[0]
    w = w_ref[0]
    start, wait = _page_fetcher(pt_ref, [cki_hbm], [kbuf], sem, g)

    @pl.when((bi == 0) & (s == 0))
    def _():
        start(0, 0)

    def pages(slot):
        wait(slot)

        @pl.when(t + 1 < n_tot)
        def _():
            start(t + 1, 1 - slot)

        kcat = jnp.concatenate([kbuf[slot, j].astype(bf16) for j in range(g)], axis=1)
        keys_ref[0] = _sortable(_idx_scores(qi, w, kcat))

    for slot in range(2):
        pl.when((s < last) & (t % 2 == slot))(functools.partial(pages, slot))

    @pl.when(s == last)
    def _():
        keys_ref[0] = jnp.full(keys_ref.shape[1:], INT_MIN, i32)
        sc = _idx_scores(qi, w, kin_ref[0])
        tok = lax.broadcasted_iota(i32, sc.shape, 0)
        j = lax.broadcasted_iota(i32, sc.shape, 1)
        keys_ref[0, :, 0:PAGE_SIZE] = jnp.where(j <= tok, _sortable(sc), INT_MIN)


def _ds_thr_kernel(keys_ref, thr_ref, need_ref, *, n_sel, chunk_w):
    rows, width = keys_ref.shape

    def get_chunk(c):
        return keys_ref[:, c * chunk_w:(c + 1) * chunk_w]

    thr, need = _select_threshold(get_chunk, width // chunk_w, chunk_w, rows, n_sel, False)
    thr_ref[...] = jnp.broadcast_to(thr, thr_ref.shape)
    need_ref[...] = jnp.broadcast_to(need, need_ref.shape)


def _ds_attn_kernel(pt_ref, q_ref, keys_ref, thr_ref, need_ref, knew_ref, vnew_ref, ck_hbm, cv_hbm,
                    o_ref, m_ref, l_ref, acc_ref, seen_ref, kbuf, vbuf, sem, *, g):
    bi = pl.program_id(0)
    s = pl.program_id(1)
    last = pl.num_programs(1) - 1
    t = bi * last + s
    n_tot = pl.num_programs(0) * last
    start, wait = _page_fetcher(pt_ref, [ck_hbm, cv_hbm], [kbuf, vbuf], sem, g)
    q = q_ref[0]
    thr = thr_ref[0][:, 0:1]
    need = need_ref[0][:, 0:1]
    tri = _tie_prefix_matrix()

    def process(keys, kps, vps):
        n = len(kps)
        seen = seen_ref[...]
        sels, seen = _selected_blocks([keys[:, j * PAGE_SIZE:(j + 1) * PAGE_SIZE] for j in range(n)],
                                      thr, need, seen, tri)
        bias = [jnp.where(sel, 0.0, NEG_BIG) for sel in sels]
        seen_ref[...] = seen
        bias = jnp.concatenate(bias, axis=1) if n > 1 else bias[0]
        kcat = jnp.concatenate([kp.astype(bf16) for kp in kps], axis=1) if n > 1 else kps[0].astype(bf16)
        vcat = jnp.concatenate([vp.astype(bf16) for vp in vps], axis=1) if n > 1 else vps[0].astype(bf16)
        lg = jnp.dot(q, kcat, preferred_element_type=f32)
        lg = lg.reshape(N_HEADS, SUBLANES, n * PAGE_SIZE) + bias[None]
        m_old = m_ref[...]
        m_new = jnp.maximum(m_old, jnp.max(lg, axis=-1, keepdims=True))
        p = jnp.exp(lg - m_new)
        alpha = jnp.exp(m_old - m_new)
        l_ref[...] = alpha * l_ref[...] + jnp.sum(p, axis=-1, keepdims=True)
        pb = p.reshape(N_HEADS * SUBLANES, n * PAGE_SIZE).astype(bf16)
        pv = lax.dot_general(pb, vcat, (((1,), (1,)), ((), ())), preferred_element_type=f32)
        acc_ref[...] = alpha * acc_ref[...] + pv.reshape(N_HEADS, SUBLANES, LANES)
        m_ref[...] = m_new

    @pl.when(s == 0)
    def _():
        m_ref[...] = jnp.full(m_ref.shape, NEG_BIG, f32)
        l_ref[...] = jnp.zeros(l_ref.shape, f32)
        acc_ref[...] = jnp.zeros(acc_ref.shape, f32)
        seen_ref[...] = jnp.zeros(seen_ref.shape, f32)

    @pl.when((bi == 0) & (s == 0))
    def _():
        start(0, 0)

    def pages(slot):
        wait(slot)

        @pl.when(t + 1 < n_tot)
        def _():
            start(t + 1, 1 - slot)

        process(keys_ref[0], [kbuf[slot, j] for j in range(g)], [vbuf[slot, j] for j in range(g)])

    for slot in range(2):
        pl.when((s < last) & (t % 2 == slot))(functools.partial(pages, slot))

    @pl.when(s == last)
    def _():
        process(keys_ref[0, :, 0:PAGE_SIZE], [knew_ref[0]], [vnew_ref[0]])
        o_ref[0] = (acc_ref[...] / l_ref[...]).reshape(N_HEADS * SUBLANES, LANES)


def dsa_sample(q, qi, sm, k, v, cache_k, cache_v, cache_idx_k, page_table):
    b, t = q.shape[:2]
    assert t <= SUBLANES
    n_pages = page_table.shape[1]
    g = min(PAGES_PER_STEP, n_pages)
    assert n_pages % g == 0
    ns = n_pages // g
    past = n_pages * PAGE_SIZE
    n_sel = min(TOPK_MAX, (past + t) // 4)
    pt = page_table.reshape(-1).astype(i32)
    padt = lambda a: jnp.pad(a, ((0, 0), (0, SUBLANES - t)) + ((0, 0),) * (a.ndim - 2))

    qi_r = padt(qi.reshape(b, t, IDX_HEADS, IDX_DIM)).transpose(0, 2, 1, 3).reshape(b, IDX_HEADS * SUBLANES, IDX_DIM)
    w_r = padt(sm[:, :, SM_WI:SM_WI + IDX_HEADS]).transpose(0, 2, 1).reshape(b, IDX_HEADS * SUBLANES, 1)
    padk = lambda a: jnp.pad(a, ((0, 0), (0, PAGE_SIZE - t), (0, 0))).transpose(0, 2, 1)
    ki_new = padk(sm[:, :, :IDX_DIM])
    k_new = padk(k)
    v_new = padk(v)
    qh = padt(q.reshape(b, t, N_HEADS, HEAD_DIM)).transpose(0, 2, 1, 3) * (HEAD_DIM ** -0.5)
    grp = (jnp.arange(N_HEADS) // GROUP)[None, :, None, None]
    q_r = jnp.concatenate([jnp.where(grp == 0, qh, 0.0), jnp.where(grp == 1, qh, 0.0)], axis=-1)
    q_r = q_r.reshape(b, N_HEADS * SUBLANES, LANES).astype(bf16)
    n_pool = cache_k.shape[0]
    ck = cache_k.transpose(0, 2, 3, 1).reshape(n_pool, N_KV_HEADS * HEAD_DIM, PAGE_SIZE)
    cv = cache_v.transpose(0, 2, 3, 1).reshape(n_pool, N_KV_HEADS * HEAD_DIM, PAGE_SIZE)
    cki = cache_idx_k.transpose(0, 2, 1)

    hbm = pl.BlockSpec(memory_space=pl.ANY)
    per_b = lambda a: pl.BlockSpec((1,) + a.shape[1:], lambda bi, s, ptr: (bi,) + (0,) * (a.ndim - 1))
    blk_w = g * PAGE_SIZE
    keys_spec = pl.BlockSpec((1, SUBLANES, blk_w), lambda bi, s, ptr: (bi, 0, s))
    width = past + blk_w

    keys = pl.pallas_call(
        functools.partial(_ds_score_kernel, g=g),
        grid_spec=pltpu.PrefetchScalarGridSpec(
            num_scalar_prefetch=1, grid=(b, ns + 1),
            in_specs=[per_b(qi_r), per_b(w_r), per_b(ki_new), hbm],
            out_specs=keys_spec,
            scratch_shapes=[pltpu.VMEM((2, g, IDX_DIM, PAGE_SIZE), f32), pltpu.SemaphoreType.DMA((1, 2))]),
        out_shape=jax.ShapeDtypeStruct((b, SUBLANES, width), i32),
        compiler_params=_cparams(("arbitrary", "arbitrary")),
        name="dsa_sample_scores",
    )(pt, qi_r.astype(bf16), w_r, ki_new, cki)

    rows = b * SUBLANES
    thr, need = pl.pallas_call(
        functools.partial(_ds_thr_kernel, n_sel=n_sel, chunk_w=blk_w),
        out_shape=[jax.ShapeDtypeStruct((rows, LANES), i32), jax.ShapeDtypeStruct((rows, LANES), f32)],
        compiler_params=pltpu.CompilerParams(vmem_limit_bytes=VMEM_LIMIT),
        name="dsa_sample_threshold",
    )(keys.reshape(rows, width))
    thr = thr.reshape(b, SUBLANES, LANES)
    need = need.reshape(b, SUBLANES, LANES)

    out = pl.pallas_call(
        functools.partial(_ds_attn_kernel, g=g),
        grid_spec=pltpu.PrefetchScalarGridSpec(
            num_scalar_prefetch=1, grid=(b, ns + 1),
            in_specs=[per_b(q_r), keys_spec, per_b(thr), per_b(need), per_b(k_new), per_b(v_new), hbm, hbm],
            out_specs=per_b(q_r),
            scratch_shapes=[pltpu.VMEM((N_HEADS, SUBLANES, 1), f32), pltpu.VMEM((N_HEADS, SUBLANES, 1), f32),
                            pltpu.VMEM((N_HEADS, SUBLANES, LANES), f32), pltpu.VMEM((SUBLANES, 1), f32),
                            pltpu.VMEM((2, g, LANES, PAGE_SIZE), f32), pltpu.VMEM((2, g, LANES, PAGE_SIZE), f32),
                            pltpu.SemaphoreType.DMA((2, 2))]),
        out_shape=jax.ShapeDtypeStruct((b, N_HEADS * SUBLANES, LANES), f32),
        compiler_params=_cparams(("arbitrary", "arbitrary")),
        name="dsa_sample_attention",
    )(pt, q_r, keys, thr, need, k_new, v_new, ck, cv)
    out = out.reshape(b, N_HEADS, SUBLANES, N_KV_HEADS, HEAD_DIM)[:, :, :t]
    out = jnp.concatenate([out[:, :GROUP, :, 0], out[:, GROUP:, :, 1]], axis=1)
    return out.transpose(0, 2, 1, 3).reshape(b, t, A_WIDTH)


def _mid_kernel(x_ref, oa_ref, ob_ref, ga_ref, gb_ref, wa_ref, wb_ref, wo_ref, gf_ref, wrh_ref, wrl_ref, br_ref,
                x1_ref, h2_ref, te_ref, tg_ref):
    a = jnp.dot(oa_ref[...].astype(bf16), wa_ref[...], preferred_element_type=f32)
    b = jnp.dot(ob_ref[...].astype(bf16), wb_ref[...], preferred_element_type=f32)
    merged = ga_ref[...] * a + gb_ref[...] * b
    x1 = x_ref[...] + jnp.dot(merged.astype(bf16), wo_ref[...], preferred_element_type=f32)
    x1_ref[...] = x1
    h2 = _rms(x1, gf_ref[...])
    h2_ref[...] = h2
    hi = h2.astype(bf16)
    lo = (h2 - hi.astype(f32)).astype(bf16)
    lg = (jnp.dot(hi, wrh_ref[...], preferred_element_type=f32) + jnp.dot(lo, wrh_ref[...], preferred_element_type=f32)
          + jnp.dot(hi, wrl_ref[...], preferred_element_type=f32)) + br_ref[...]
    lane = lax.broadcasted_iota(i32, lg.shape, 1)
    lane_f = lane.astype(f32)
    vals, ids = [], []
    for _ in range(TOP_K):
        m = jnp.max(lg, axis=1, keepdims=True)
        idx = jnp.min(jnp.where(lg == m, lane_f, float(LANES)), axis=1, keepdims=True).astype(i32)
        vals.append(m)
        ids.append(idx)
        lg = jnp.where(lane == idx, -jnp.inf, lg)
    ex = [jnp.exp(v - vals[0]) for v in vals]
    tot = ex[0] + ex[1] + ex[2] + ex[3]
    te = jnp.zeros(lg.shape, i32)
    tg = jnp.zeros(lg.shape, f32)
    for j in range(TOP_K):
        te = jnp.where(lane == j, ids[j], te)
        tg = jnp.where(lane == j, ex[j] / tot, tg)
    te_ref[...] = te
    tg_ref[...] = tg


def mid(x, oa, ob, ga, gb, wa, wb, wo, g_ffn, wr_hi, wr_lo, br):
    n = x.shape[0]
    tm = min(256, n)
    assert n % tm == 0
    row = lambda w: pl.BlockSpec((tm, w), lambda i: (i, 0))
    full = lambda a: pl.BlockSpec(a.shape, lambda i: (0,) * a.ndim)
    return pl.pallas_call(
        _mid_kernel,
        grid=(n // tm,),
        in_specs=[row(D_MODEL), row(A_WIDTH), row(M_WIDTH), row(D_MODEL), row(D_MODEL),
                  full(wa), full(wb), full(wo), full(g_ffn), full(wr_hi), full(wr_lo), full(br)],
        out_specs=[row(D_MODEL), row(D_MODEL), row(LANES), row(LANES)],
        out_shape=[jax.ShapeDtypeStruct((n, D_MODEL), f32), jax.ShapeDtypeStruct((n, D_MODEL), f32),
                   jax.ShapeDtypeStruct((n, LANES), i32), jax.ShapeDtypeStruct((n, LANES), f32)],
        compiler_params=_cparams(("parallel",)),
        name="mid",
    )(x, oa, ob, ga, gb, wa, wb, wo, g_ffn, wr_hi, wr_lo, br)


def _rank_kernel(te_ref, dest_ref, meta_ref, cnt_ref, carry_ref, *, blk):
    ph = pl.program_id(0)
    i = pl.program_id(1)
    tm = te_ref.shape[0]
    lane = lax.broadcasted_iota(i32, (tm, LANES), 1)
    te = te_ref[...]
    oh = jnp.zeros((tm, LANES), f32)
    for j in range(TOP_K):
        oh = oh + jnp.where(lane == te[:, j:j + 1], 1.0, 0.0)
    tile_cnt = jnp.sum(oh, axis=0, keepdims=True)

    @pl.when((ph == 0) & (i == 0))
    def _():
        cnt_ref[...] = jnp.zeros(cnt_ref.shape, f32)

    @pl.when(ph == 0)
    def _():
        cnt_ref[...] += tile_cnt

    @pl.when((ph == 1) & (i == 0))
    def _():
        cnt = cnt_ref[...]
        padded = jnp.floor((cnt + (blk - 1)) / blk) * blk
        r = lax.broadcasted_iota(i32, (LANES, LANES), 0)
        c = lax.broadcasted_iota(i32, (LANES, LANES), 1)
        col = jnp.sum(jnp.where(r == c, padded, 0.0), axis=1, keepdims=True)
        start = jnp.sum(jnp.where(r < c, col, 0.0), axis=0, keepdims=True)
        carry_ref[...] = start
        pad_end = start + padded
        nbp = meta_ref.shape[0]
        jb = (lax.broadcasted_iota(i32, (nbp, LANES), 0) * blk).astype(f32)
        lane2 = lax.broadcasted_iota(i32, (nbp, LANES), 1)
        be = jnp.sum(jnp.where((pad_end <= jb) & (lane2 < N_EXPERTS), 1.0, 0.0), axis=1, keepdims=True)
        be = jnp.minimum(be, float(N_EXPERTS - 1))
        n_act = jnp.sum(jnp.where(lane2 == N_EXPERTS - 1, pad_end, 0.0), axis=1, keepdims=True) / blk
        meta_ref[...] = jnp.where(lane2 == 0, be, jnp.where(lane2 == 1, n_act, 0.0)).astype(i32)

    @pl.when(ph == 1)
    def _():
        r = lax.broadcasted_iota(i32, (tm, tm), 0)
        c = lax.broadcasted_iota(i32, (tm, tm), 1)
        before = jnp.where(c < r, 1.0, 0.0).astype(bf16)
        pos = carry_ref[...] + jnp.dot(before, oh.astype(bf16), preferred_element_type=f32)
        d = jnp.zeros((tm, LANES), f32)
        for j in range(TOP_K):
            dj = jnp.sum(jnp.where(lane == te[:, j:j + 1], pos, 0.0), axis=1, keepdims=True)
            d = jnp.where(lane == j, dj, d)
        dest_ref[...] = d.astype(i32)
        carry_ref[...] += tile_cnt


def rank(te, blk, n_blocks):
    n = te.shape[0]
    tm = next(c for c in (512, 384, 256, 128, 64, 32, 16, SUBLANES) if n % c == 0)
    nbp = -(-n_blocks // SUBLANES) * SUBLANES
    return pl.pallas_call(
        functools.partial(_rank_kernel, blk=blk),
        grid=(2, n // tm),
        in_specs=[pl.BlockSpec((tm, LANES), lambda ph, i: (i, 0))],
        out_specs=[pl.BlockSpec((tm, LANES), lambda ph, i: (i * ph, 0)),
                   pl.BlockSpec((nbp, LANES), lambda ph, i: (0, 0))],
        out_shape=[jax.ShapeDtypeStruct((n, LANES), i32), jax.ShapeDtypeStruct((nbp, LANES), i32)],
        scratch_shapes=[pltpu.VMEM((1, LANES), f32), pltpu.VMEM((1, LANES), f32)],
        compiler_params=_cparams(("arbitrary", "arbitrary")),
        name="rank",
    )(te)


def _ffn_kernel(be_ref, nact_ref, rt_ref, x_hbm, *rest):
    wgu_refs = rest[:W_SPLIT]
    bgu_ref = rest[W_SPLIT]
    wdn_refs = rest[W_SPLIT + 1:2 * W_SPLIT + 1]
    bdn_ref, y_ref, xbuf, sem, wgu_bf, wdn_bf = rest[2 * W_SPLIT + 1:]
    i = pl.program_id(0)
    n_act = nact_ref[0]
    blk = xbuf.shape[1]

    def row_copy(b, r, slot):
        tok = rt_ref[b * blk + r]
        return pltpu.make_async_copy(x_hbm.at[pl.ds(tok, 1)], xbuf.at[slot, pl.ds(r, 1)], sem.at[slot])

    def wait_slot(slot):
        pltpu.make_async_copy(xbuf.at[slot], xbuf.at[slot], sem.at[slot]).wait()

    @pl.when(i == 0)
    def _():
        def body(r, c):
            row_copy(0, r, 0).start()
            return c
        lax.fori_loop(0, blk, body, 0)

    @pl.when(i == n_act)
    def _():
        wait_slot(i % 2)

    @pl.when(i >= n_act)
    def _():
        y_ref[...] = jnp.zeros(y_ref.shape, f32)

    @pl.when((i < n_act) & ((i == 0) | (be_ref[i] != be_ref[jnp.maximum(i - 1, 0)])))
    def _():
        wg = wgu_bf.shape[1] // W_SPLIT
        wd = wdn_bf.shape[1] // W_SPLIT
        for c in range(W_SPLIT):
            wgu_bf[:, c * wg:(c + 1) * wg] = wgu_refs[c][0].astype(bf16)
            wdn_bf[:, c * wd:(c + 1) * wd] = wdn_refs[c][0].astype(bf16)

    def block(slot):
        wait_slot(slot)
        for r in range(blk):
            row_copy(i + 1, r, 1 - slot).start()
        x = xbuf[slot].astype(bf16)
        gu = jnp.dot(x, wgu_bf[...], preferred_element_type=f32) + bgu_ref[0]
        gate = jnp.minimum(gu[:, :EXPERT_FF], SWIGLU_LIMIT)
        up = jnp.clip(gu[:, EXPERT_FF:], -SWIGLU_LIMIT, SWIGLU_LIMIT)
        act = (up + 1.0) * (gate * _sigmoid(SWIGLU_ALPHA * gate))
        y_ref[...] = jnp.dot(act.astype(bf16), wdn_bf[...], preferred_element_type=f32) + bdn_ref[0]

    for slot in range(2):
        pl.when((i < n_act) & (i % 2 == slot))(functools.partial(block, slot))


def ffn(blk_e, n_act, row_tok, h2, w_gu, b_gu, w_dn, b_dn, blk, n_blocks):
    d = h2.shape[1]
    ff2 = w_gu.shape[2]
    return pl.pallas_call(
        _ffn_kernel,
        grid_spec=pltpu.PrefetchScalarGridSpec(
            num_scalar_prefetch=3, grid=(n_blocks,),
            in_specs=[pl.BlockSpec(memory_space=pl.ANY)]
            + [pl.BlockSpec((1, d, ff2 // W_SPLIT), functools.partial(lambda i, be, na, rt, c: (be[i], 0, c), c=c))
               for c in range(W_SPLIT)]
            + [pl.BlockSpec((1, 1, ff2), lambda i, be, na, rt: (be[i], 0, 0))]
            + [pl.BlockSpec((1, ff2 // 2, d // W_SPLIT), functools.partial(lambda i, be, na, rt, c: (be[i], 0, c), c=c))
               for c in range(W_SPLIT)]
            + [pl.BlockSpec((1, 1, d), lambda i, be, na, rt: (be[i], 0, 0))],
            out_specs=pl.BlockSpec((blk, d), lambda i, be, na, rt: (i, 0)),
            scratch_shapes=[pltpu.VMEM((2, blk, d), f32), pltpu.SemaphoreType.DMA((2,)),
                            pltpu.VMEM((d, ff2), bf16), pltpu.VMEM((ff2 // 2, d), bf16)]),
        out_shape=jax.ShapeDtypeStruct((n_blocks * blk, d), f32),
        compiler_params=_cparams(("arbitrary",)),
        name="ffn",
    )(blk_e, n_act, row_tok, h2, *([w_gu] * W_SPLIT), b_gu.reshape(b_gu.shape[0], 1, ff2),
      *([w_dn] * W_SPLIT), b_dn.reshape(b_dn.shape[0], 1, d))


def _out_kernel(dest_ref, x1_ref, tg_ref, p_ref, ys_hbm, gp_ref, wg_ref, wp_ref, gfin_ref, y_ref, buf, sem):
    i = pl.program_id(0)
    n = pl.num_programs(0)
    tm = x1_ref.shape[0]

    def row_copy(t, r, j, slot):
        d = dest_ref[(t * tm + r) * TOP_K + j]
        return pltpu.make_async_copy(ys_hbm.at[pl.ds(d, 1)], buf.at[slot, j, pl.ds(r, 1)], sem.at[slot])

    def wait_slot(slot):
        pltpu.make_async_copy(buf.at[slot], buf.at[slot], sem.at[slot]).wait()

    @pl.when(i == 0)
    def _():
        def body(r, c):
            for j in range(TOP_K):
                row_copy(0, r, j, 0).start()
            return c
        lax.fori_loop(0, tm, body, 0)

    def tile(slot):
        wait_slot(slot)
        nxt = jnp.minimum(i + 1, n - 1)
        for r in range(tm):
            for j in range(TOP_K):
                row_copy(nxt, r, j, 1 - slot).start()
        tg = tg_ref[...]
        x2 = x1_ref[...]
        for j in range(TOP_K):
            x2 = x2 + tg[:, j:j + 1] * buf[slot, j]
        hn = _rms(x2, gp_ref[...]).astype(bf16)
        gate = _sigmoid(jnp.dot(hn, wg_ref[...], preferred_element_type=f32))
        x3 = x2 + gate * jnp.dot(p_ref[...].astype(bf16), wp_ref[...], preferred_element_type=f32)
        y_ref[...] = _rms(x3, gfin_ref[...])

    for slot in range(2):
        pl.when(i % 2 == slot)(functools.partial(tile, slot))

    @pl.when(i == n - 1)
    def _():
        wait_slot(1 - i % 2)


def out_stage(dest_flat, x1, tg, p, ys, g_ple, wg, wp, g_final):
    n, d = x1.shape
    tm = min(OUT_TM, n)
    assert n % tm == 0
    row = lambda w: pl.BlockSpec((tm, w), lambda i, ds: (i, 0))
    full = lambda a: pl.BlockSpec(a.shape, lambda i, ds: (0,) * a.ndim)
    return pl.pallas_call(
        _out_kernel,
        grid_spec=pltpu.PrefetchScalarGridSpec(
            num_scalar_prefetch=1, grid=(n // tm,),
            in_specs=[row(d), row(LANES), row(p.shape[1]), pl.BlockSpec(memory_space=pl.ANY),
                      full(g_ple), full(wg), full(wp), full(g_final)],
            out_specs=row(d),
            scratch_shapes=[pltpu.VMEM((2, TOP_K, tm, d), f32), pltpu.SemaphoreType.DMA((2,))]),
        out_shape=jax.ShapeDtypeStruct((n, d), f32),
        compiler_params=_cparams(("arbitrary",)),
        name="out",
    )(dest_flat, x1, tg, p, ys, g_ple, wg, wp, g_final)


def mid_stage(x, oa, ob, ga, gb, w):
    return mid(x, oa, ob, ga, gb, w["wa"], w["wb"], w["wo"], w["g_ffn"], w["wr_hi"], w["wr_lo"], w["br"])


def moe_out(parts, w):
    h2 = jnp.concatenate([pt[1] for pt in parts], axis=0)
    te = jnp.concatenate([pt[2] for pt in parts], axis=0)
    n = h2.shape[0]
    blk = FFN_BLK
    n_blocks = -(-(n * TOP_K) // blk) + N_EXPERTS
    dest, meta = rank(te, blk, n_blocks)
    dest_flat = dest[:, :TOP_K].reshape(-1)
    blk_e = meta[:n_blocks, 0]
    n_act = meta[0:1, 1]
    tok = jnp.repeat(jnp.arange(n, dtype=i32), TOP_K)
    row_tok = jnp.zeros((n_blocks * blk,), i32).at[dest_flat].set(tok)
    ys = ffn(blk_e, n_act, row_tok, h2, w["w_gu"], w["b_gu"], w["w_dn"], w["b_dn"], blk, n_blocks)
    outs, off = [], 0
    for x1, _, _, tg, p in parts:
        ni = x1.shape[0]
        outs.append(out_stage(dest_flat[off * TOP_K:(off + ni) * TOP_K], x1, tg, p, ys,
                              w["g_ple"], w["wg"], w["wp"], w["g_final"]))
        off += ni
    return outs


def tail(x, oa, ob, ga, gb, p, w):
    return moe_out([mid_stage(x, oa, ob, ga, gb, w) + (p,)], w)[0]


def _prep_weights(g_mix, w_in, b_gate, g_mnorm, w_up_a, w_up_b, w_out, g_ffn, w_router, b_router,
                  w_gu, b_gu, w_dn, b_dn, g_ple, w_ple_gate, w_ple_proj, g_final):
    wr = jnp.pad(w_router.astype(f32), ((0, 0), (0, LANES - N_EXPERTS)))
    wr_hi = wr.astype(bf16)
    wr_lo = (wr - wr_hi.astype(f32)).astype(bf16)
    br = jnp.full((1, LANES), -jnp.inf, f32).at[0, :N_EXPERTS].set(b_router.astype(f32))
    bias = jnp.zeros((1, LANES), f32).at[0, SM_IG:SM_IG + 2 * M_HEADS].set(b_gate.astype(f32))
    return dict(g_mix=g_mix.reshape(1, -1), w_in=_pack_w_in(w_in), bias=bias, g_mnorm=g_mnorm,
                wa=w_up_a.astype(bf16), wb=w_up_b.astype(bf16), wo=w_out.astype(bf16), g_ffn=g_ffn.reshape(1, -1),
                wr_hi=wr_hi, wr_lo=wr_lo, br=br, w_gu=w_gu, b_gu=b_gu, w_dn=w_dn, b_dn=b_dn,
                g_ple=g_ple.reshape(1, -1), wg=w_ple_gate.astype(bf16), wp=w_ple_proj.astype(bf16),
                g_final=g_final.reshape(1, -1))


def _layer(x, p, pos, attn_fn, c0, n0, m0, w):
    b, t = x.shape[:2]
    n = b * t
    cos, sin = _rope_tables(pos)
    cos = jnp.tile(cos, (b, 1))
    sin = jnp.tile(sin, (b, 1))
    x2d = x.reshape(n, D_MODEL)
    q, k, v, qi, sm, mq, mk, mv, og, ga, gb = in_proj(x2d, cos, sin, w["g_mix"], w["w_in"], w["bias"])
    r3 = lambda a: a.reshape(b, t, a.shape[-1])
    o_a = attn_fn(r3(q), r3(qi), r3(sm), r3(k), r3(v))
    o_b, c, nn, m = mlstm_branch(r3(mq), r3(mk), r3(mv), r3(og), r3(sm), c0, n0, m0, w["g_mnorm"])
    part = mid_stage(x2d, o_a.reshape(n, A_WIDTH), o_b.reshape(n, M_WIDTH), ga, gb, w) + (p.reshape(n, PLE_DIM),)
    state = (k.reshape(b, t, N_KV_HEADS, HEAD_DIM), v.reshape(b, t, N_KV_HEADS, HEAD_DIM),
             sm[:, :IDX_DIM].reshape(b, t, IDX_DIM), c, nn, m)
    return part, state


def kernel(x_prompt, x_sample, cache_k, cache_v, cache_idx_k, state_C, state_n, state_m, page_table,
           p_prompt, p_sample, g_mix, w_in, b_gate, g_mnorm, w_up_a, w_up_b, w_out, g_ffn,
           w_router, b_router, w_gu, b_gu, w_dn, b_dn, g_ple, w_ple_gate, w_ple_proj, g_final):
    assert x_prompt.shape[-1] == D_MODEL and w_in.shape[0] == 1, "single-layer model of width D_MODEL"
    bp, tp = x_prompt.shape[:2]
    ts = x_sample.shape[1]
    past = page_table.shape[1] * PAGE_SIZE
    w = _prep_weights(g_mix[0], w_in[0], b_gate[0], g_mnorm[0], w_up_a[0], w_up_b[0], w_out[0], g_ffn[0],
                      w_router[0], b_router[0], w_gu[0], b_gu[0], w_dn[0], b_dn[0], g_ple[0],
                      w_ple_gate[0], w_ple_proj[0], g_final)
    zeros = lambda *s: jnp.zeros(s, f32)
    part_p, sp = _layer(x_prompt, p_prompt[0], jnp.arange(tp, dtype=i32), dsa_prompt_branch,
                        zeros(bp, M_HEADS, M_V, M_QK), zeros(bp, M_HEADS, M_QK), zeros(bp, M_HEADS), w)
    attn_s = functools.partial(dsa_sample, cache_k=cache_k[0], cache_v=cache_v[0], cache_idx_k=cache_idx_k[0],
                               page_table=page_table)
    part_s, ss = _layer(x_sample, p_sample[0], past + jnp.arange(ts, dtype=i32), attn_s,
                        state_C[0], state_n[0], state_m[0], w)
    yp, ys = moe_out([part_p, part_s], w)
    return ((yp.reshape(x_prompt.shape), ys.reshape(x_sample.shape))
            + tuple(s[None] for s in sp) + tuple(s[None] for s in ss))
```

```python
import functools

import jax
import jax.numpy as jnp
from jax import lax
from jax.experimental import pallas as pl
from jax.experimental.pallas import tpu as pltpu

f32 = jnp.float32
bf16 = jnp.bfloat16
i32 = jnp.int32

D_MODEL = 1024
PAGE_SIZE = 128
N_HEADS = 8
N_KV_HEADS = 2
HEAD_DIM = 64
GROUP = N_HEADS // N_KV_HEADS
IDX_HEADS = 4
IDX_DIM = 64
TOPK_MAX = 256
ROPE_THETA = 10000.0
M_HEADS = 4
M_QK = 64
M_V = 128
N_EXPERTS = 32
TOP_K = 4
EXPERT_FF = D_MODEL
SWIGLU_LIMIT = 7.0
SWIGLU_ALPHA = 1.702
PLE_DIM = 256
EPS = 1e-6
A_WIDTH = N_HEADS * HEAD_DIM
M_WIDTH = M_HEADS * M_V

LANES = 128
SUBLANES = 8
VMEM_LIMIT = 56 * 1024 * 1024

M_CHUNK = 128
MLSTM_BATCH = 1
TQ = 128
KC = 1024
PAGE_SLOTS = 4
PAGES_PER_STEP = 16
FFN_BLK = 256
W_SPLIT = 4
OUT_TM = 128

INT_MIN = -2147483648
INT_MAX = 2147483647
NEG_BIG = -1e30
LOG2E = 1.4426950408889634
ROW_SLICE = 16

SM_KI = 0
SM_WI = IDX_DIM
SM_IG = SM_WI + IDX_HEADS
SM_LF = SM_IG + M_HEADS


def _cparams(sem):
    return pltpu.CompilerParams(dimension_semantics=sem, vmem_limit_bytes=VMEM_LIMIT)


def _rms(x, g):
    return x * lax.rsqrt(jnp.mean(x * x, axis=-1, keepdims=True) + EPS) * g


def _sigmoid(x):
    return 1.0 / (1.0 + jnp.exp(-x))


def _log_sigmoid(x):
    return jnp.minimum(x, 0.0) - jnp.log1p(jnp.exp(-jnp.abs(x)))


def _sortable(x):
    bits = lax.bitcast_convert_type(x, i32)
    key = bits ^ ((bits >> 31) & INT_MAX)
    return jnp.where(x == 0.0, 0, key)


_G_Q = (0, 512)
_G_K = (512, 640)
_G_V = (640, 768)
_G_QI = (768, 1024)
_G_SM = (1024, 1152)
_G_MQ = (1152, 1408)
_G_MK = (1408, 1664)
_G_MV = (1664, 2176)
_G_MO = (2176, 2688)
_G_GA = (2688, 3712)
_G_GB = (3712, 4736)
_W_COLS = 4736


def _pack_w_in(w_in):
    o = [0]
    for s in (A_WIDTH, 128, 128, 256, 64, 4, 256, 256, 512, 4, 4, 512, 1024, 1024):
        o.append(o[-1] + s)
    aq, ak, av, iq, ik, iw, mq, mk, mv, mi, mf, mo, ga, gb = [w_in[:, o[i]:o[i + 1]] for i in range(14)]
    pad = jnp.zeros((w_in.shape[0], LANES - (IDX_DIM + IDX_HEADS + 2 * M_HEADS)), w_in.dtype)
    small = jnp.concatenate([ik, iw, mi, mf, pad], axis=1)
    w = jnp.concatenate([aq, ak, av, iq, small, mq, mk, mv, mo, ga, gb], axis=1)
    return w.astype(bf16)


def _in_kernel(x_ref, cos_ref, sin_ref, g_ref, w_ref, bias_ref,
               q_ref, k_ref, v_ref, qi_ref, sm_ref, mq_ref, mk_ref, mv_ref, og_ref, ga_ref, gb_ref):
    x = x_ref[...]
    hb = _rms(x, g_ref[...]).astype(bf16)
    cos = cos_ref[...]
    sin = sin_ref[...]
    tm = x.shape[0]
    lane = lax.broadcasted_iota(i32, (tm, LANES), 1)
    first_half = (lane % HEAD_DIM) < (HEAD_DIM // 2)

    def rope(z):
        rot = jnp.where(first_half, pltpu.roll(z, LANES - HEAD_DIM // 2, 1), pltpu.roll(z, HEAD_DIM // 2, 1))
        return z * cos + rot * sin

    def proj(grp):
        return jnp.dot(hb, w_ref[:, grp[0]:grp[1]], preferred_element_type=f32)

    z = proj(_G_Q)
    for j in range(4):
        q_ref[:, j * LANES:(j + 1) * LANES] = rope(z[:, j * LANES:(j + 1) * LANES])
    k_ref[...] = rope(proj(_G_K))
    v_ref[...] = proj(_G_V)
    z = proj(_G_QI)
    for j in range(2):
        qi_ref[:, j * LANES:(j + 1) * LANES] = rope(z[:, j * LANES:(j + 1) * LANES])
    z = proj(_G_SM)
    zb = z + bias_ref[...]
    sm = jnp.where(lane < SM_WI, rope(z),
                   jnp.where(lane < SM_IG, z * (IDX_HEADS ** -0.5 * IDX_DIM ** -0.5),
                             jnp.where(lane < SM_LF, zb,
                                       jnp.where(lane < SM_LF + M_HEADS, _log_sigmoid(zb), 0.0))))
    sm_ref[...] = sm
    mq_ref[...] = proj(_G_MQ)
    mk_ref[...] = proj(_G_MK) * (M_QK ** -0.5)
    mv_ref[...] = proj(_G_MV)
    og_ref[...] = _sigmoid(proj(_G_MO))
    ga_ref[...] = _sigmoid(proj(_G_GA))
    gb_ref[...] = _sigmoid(proj(_G_GB))


def _rope_tables(pos):
    half = HEAD_DIM // 2
    inv = ROPE_THETA ** (-jnp.arange(half, dtype=f32) / half)
    ang = pos.astype(f32)[:, None] * inv[None, :]
    cos = jnp.cos(ang)
    sin = jnp.sin(ang)
    cos128 = jnp.tile(cos, (1, 4))
    sin128 = jnp.tile(jnp.concatenate([-sin, sin], axis=1), (1, 2))
    return cos128, sin128


def in_proj(x2d, cos128, sin128, g_mix, w_packed, bias128):
    n = x2d.shape[0]
    tm = min(256, n)
    assert n % tm == 0
    row = lambda w: pl.BlockSpec((tm, w), lambda i: (i, 0))
    full = lambda a: pl.BlockSpec(a.shape, lambda i: (0,) * a.ndim)
    widths = (512, 128, 128, 256, 128, 256, 256, 512, 512, 1024, 1024)
    return pl.pallas_call(
        _in_kernel,
        grid=(n // tm,),
        in_specs=[row(D_MODEL), row(LANES), row(LANES), full(g_mix), full(w_packed), full(bias128)],
        out_specs=[row(w) for w in widths],
        out_shape=[jax.ShapeDtypeStruct((n, w), f32) for w in widths],
        compiler_params=_cparams(("parallel",)),
        name="in_proj",
    )(x2d, cos128, sin128, g_mix, w_packed, bias128)


def _mlstm_kernel(mq_ref, mk_ref, mv_ref, og_ref, sm_ref, gr_ref, c0_ref, n0_ref, m0_ref, gn_ref,
                  ob_ref, c_ref, n_ref, m_ref):
    ci = pl.program_id(1)
    L = mq_ref.shape[1]

    @pl.when(ci == 0)
    def _():
        c_ref[...] = c0_ref[...]
        n_ref[...] = n0_ref[...]
        m_ref[...] = m0_ref[...]

    row = lax.broadcasted_iota(i32, (L, L), 0)
    col = lax.broadcasted_iota(i32, (L, L), 1)
    tril = row >= col
    for bi, hd in [(bi, hd) for bi in range(mq_ref.shape[0]) for hd in range(M_HEADS)]:
        sm = sm_ref[bi]
        gr = gr_ref[bi]
        q = mq_ref[bi, :, hd * M_QK:(hd + 1) * M_QK]
        k = mk_ref[bi, :, hd * M_QK:(hd + 1) * M_QK]
        v = mv_ref[bi, :, hd * M_V:(hd + 1) * M_V]
        ig_r = gr[hd:hd + 1, :]
        lf_r = gr[M_HEADS + hd:M_HEADS + hd + 1, :]
        ig_c = sm[:, SM_IG + hd:SM_IG + hd + 1]
        lf_c = sm[:, SM_LF + hd:SM_LF + hd + 1]
        C = c_ref[bi, hd]
        nrow = n_ref[bi, hd]
        m_prev = m_ref[bi, hd]
        b_c = jnp.sum(jnp.where(tril, lf_r, 0.0), axis=1, keepdims=True)
        b_r = jnp.sum(jnp.where(tril, 0.0, lf_c) + jnp.where(row == col, lf_c, 0.0), axis=0, keepdims=True)
        dmat = jnp.where(tril, b_c - b_r + ig_r, -jnp.inf)
        inter = b_c + m_prev
        m_t = jnp.maximum(inter, jnp.max(dmat, axis=1, keepdims=True))
        qb = q.astype(bf16)
        kb = k.astype(bf16)
        qk = lax.dot_general(qb, kb, (((1,), (1,)), ((), ())), preferred_element_type=f32)
        s = qk * jnp.exp(dmat - m_t)
        w_inter = jnp.exp(inter - m_t)
        qc = lax.dot_general(qb, C.astype(bf16), (((1,), (1,)), ((), ())), preferred_element_type=f32)
        num = jnp.dot(s.astype(bf16), v.astype(bf16), preferred_element_type=f32) + w_inter * qc
        den = jnp.sum(s, axis=1, keepdims=True) + w_inter * jnp.sum(q * nrow, axis=1, keepdims=True)
        h = num / jnp.maximum(jnp.abs(den), jnp.exp(-m_t))
        b_last = b_c[L - 1:L, :]
        g_c = b_last - b_c + ig_c
        m_new = jnp.maximum(b_last + m_prev, jnp.max(g_c, axis=0, keepdims=True))
        w_k = jnp.exp(g_c - m_new)
        decay = jnp.exp(b_last + m_prev - m_new)
        wv = (w_k * v).astype(bf16)
        c_ref[bi, hd] = decay * C + lax.dot_general(wv, kb, (((0,), (0,)), ((), ())), preferred_element_type=f32)
        n_ref[bi, hd] = decay * nrow + jnp.sum(w_k * k, axis=0, keepdims=True)
        m_ref[bi, hd] = m_new
        gn = gn_ref[:, hd * M_V:(hd + 1) * M_V]
        ob_ref[bi, :, hd * M_V:(hd + 1) * M_V] = og_ref[bi, :, hd * M_V:(hd + 1) * M_V] * _rms(h, gn)


def mlstm(mq, mk, mv, og, sm, grow, c0, n0, m0, g_mnorm):
    b, t = mq.shape[:2]
    L = M_CHUNK
    bb = MLSTM_BATCH if b % MLSTM_BATCH == 0 else 1
    assert t % L == 0
    tok = lambda w: pl.BlockSpec((bb, L, w), lambda bi, ci: (bi, ci, 0))
    st = lambda a: pl.BlockSpec((bb,) + a.shape[1:], lambda bi, ci: (bi,) + (0,) * (a.ndim - 1))
    return pl.pallas_call(
        _mlstm_kernel,
        grid=(b // bb, t // L),
        in_specs=[tok(256), tok(256), tok(512), tok(512), tok(LANES),
                  pl.BlockSpec((bb, 2 * M_HEADS, L), lambda bi, ci: (bi, 0, ci)),
                  st(c0), st(n0), st(m0), pl.BlockSpec(g_mnorm.shape, lambda bi, ci: (0, 0))],
        out_specs=[tok(512), st(c0), st(n0), st(m0)],
        out_shape=[jax.ShapeDtypeStruct((b, t, M_WIDTH), f32), jax.ShapeDtypeStruct(c0.shape, f32),
                   jax.ShapeDtypeStruct(n0.shape, f32), jax.ShapeDtypeStruct(m0.shape, f32)],
        compiler_params=_cparams(("parallel", "arbitrary")),
        name="mlstm",
    )(mq, mk, mv, og, sm, grow, c0, n0, m0, g_mnorm)


def mlstm_branch(mq, mk, mv, og, sm, c0, n0, m0, g_mnorm):
    b, t = mq.shape[:2]
    tp = -(-t // M_CHUNK) * M_CHUNK
    if tp != t:
        pad = lambda a: jnp.pad(a, ((0, 0), (0, tp - t), (0, 0)))
        mq, mk, mv, og = pad(mq), pad(mk), pad(mv), pad(og)
        sm_pad = jnp.zeros((b, tp - t, LANES), f32).at[:, :, SM_IG:SM_IG + M_HEADS].set(NEG_BIG)
        sm = jnp.concatenate([sm, sm_pad], axis=1)
    grow = sm[:, :, SM_IG:SM_IG + 2 * M_HEADS].transpose(0, 2, 1)
    ob, c, n, m = mlstm(mq, mk, mv, og, sm, grow, c0.astype(f32), n0.astype(f32).reshape(b, M_HEADS, 1, M_QK),
                        m0.astype(f32).reshape(b, M_HEADS, 1, 1), g_mnorm.reshape(1, M_WIDTH))
    return ob[:, :t], c, n.reshape(b, M_HEADS, M_QK), m.reshape(b, M_HEADS)


def _select_threshold(get_chunk, n_chunks, chunk_w, rows, n_sel, dynamic):
    def count(pred):
        def body(c, acc):
            hit = jnp.where(pred(get_chunk(c)), 1.0, 0.0)
            for j in range(chunk_w // LANES):
                acc = acc + hit[:, j * LANES:(j + 1) * LANES]
            return acc
        acc0 = jnp.zeros((rows, LANES), f32)
        if dynamic:
            acc = lax.fori_loop(0, n_chunks, body, acc0)
        else:
            acc = acc0
            for c in range(n_chunks):
                acc = body(c, acc)
        return jnp.sum(acc, axis=1, keepdims=True)

    kf = float(n_sel)

    total = jnp.zeros((rows, 1), f32) + (n_chunks * chunk_w).astype(f32) if dynamic else \
        jnp.full((rows, 1), float(n_chunks * chunk_w), f32)
    c_nonneg = count(lambda keys: keys >= 0)
    c_pos = count(lambda keys: keys >= 1)
    take0 = c_nonneg >= kf
    zero_tie = take0 & (c_pos < kf)

    def unsettled(st):
        it, _, cnt = st
        return (it < 32) & (jnp.max(jnp.where(zero_tie, 0.0, jnp.abs(cnt - kf))) > 0.0)

    def bit_step(st):
        it, thr_u, cnt = st
        cand_u = thr_u | lax.shift_left(jnp.int32(1), 31 - it)
        cand_s = cand_u ^ INT_MIN
        c = count(lambda keys: keys >= cand_s)
        take = c >= kf
        return it + 1, jnp.where(take, cand_u, thr_u), jnp.where(take, c, cnt)

    state0 = (jnp.int32(1), jnp.where(take0, INT_MIN, 0).astype(i32), jnp.where(take0, c_nonneg, total))
    _, thr_u, _ = lax.while_loop(unsettled, bit_step, state0)
    thr = jnp.maximum(thr_u ^ INT_MIN, INT_MIN + 1)
    need = kf - count(lambda keys: keys > thr)
    return thr, need


def _tie_prefix_matrix():
    r = lax.broadcasted_iota(i32, (LANES, LANES), 0)
    c = lax.broadcasted_iota(i32, (LANES, LANES), 1)
    return jnp.where(r <= c, 1.0, 0.0).astype(bf16)


def _selected_blocks(key_blocks, thr, need, seen, tri):
    rows = key_blocks[0].shape[0]
    eqs = [kb == thr for kb in key_blocks]
    stack = jnp.concatenate([jnp.where(eq, 1.0, 0.0) for eq in eqs], axis=0).astype(bf16)
    pre = jnp.dot(stack, tri, preferred_element_type=f32)
    sels = []
    for j, (kb, eq) in enumerate(zip(key_blocks, eqs)):
        pj = pre[j * rows:(j + 1) * rows]
        sels.append((kb > thr) | (eq & ((seen + pj) <= need)))
        seen = seen + pj[:, LANES - 1:LANES]
    return sels, seen


def _dsa_prompt_kernel(q_ref, qi_ref, sm_ref, kit_ref, kt_ref, vd_ref, o_ref,
                       keys_ref, bias_ref, lg_ref, p_ref, m_ref, acc_ref, *, n_sel):
    qb = pl.program_id(1)
    tq = q_ref.shape[1]
    kc = kit_ref.shape[3]
    n_chunks = (qb * tq + tq - 1) // kc + 1
    lane = lax.broadcasted_iota(i32, (tq, LANES), 1)
    lo = lane < HEAD_DIM
    t_col = qb * tq + lax.broadcasted_iota(i32, (tq, 1), 0)
    sm = sm_ref[0]

    qi = qi_ref[0]
    qi_h = []
    for h in range(IDX_HEADS):
        blk = qi[:, (h // 2) * LANES:(h // 2 + 1) * LANES]
        qi_h.append(jnp.where(lo if h % 2 == 0 else ~lo, blk, 0.0).astype(bf16))
    w_h = [sm[:, SM_WI + h:SM_WI + h + 1] for h in range(IDX_HEADS)]

    def score_chunk(c, carry):
        kt = kit_ref[0, c]
        sc = jnp.zeros((tq, kc), f32)
        for h in range(IDX_HEADS):
            s = jnp.dot(qi_h[h], kt, preferred_element_type=f32)
            sc = sc + w_h[h] * jnp.maximum(s, 0.0)
        idx = c * kc + lax.broadcasted_iota(i32, (tq, kc), 1)
        keys_ref[c] = jnp.where(idx <= t_col, _sortable(sc), INT_MIN)
        return carry

    lax.fori_loop(0, n_chunks, score_chunk, 0)

    thr, need = _select_threshold(lambda c: keys_ref[c], n_chunks, kc, tq, n_sel, True)
    tri = _tie_prefix_matrix()

    q = q_ref[0] * (HEAD_DIM ** -0.5 * LOG2E)
    q_g = []
    for g in range(N_KV_HEADS):
        parts = []
        for j in range(GROUP):
            h = g * GROUP + j
            blk = q[:, (h // 2) * LANES:(h // 2 + 1) * LANES]
            parts.append(jnp.where(lo if h % 2 == 0 else ~lo, blk, 0.0).astype(bf16))
        q_g.append(jnp.concatenate(parts, axis=0))
    m_ref[...] = jnp.full(m_ref.shape, NEG_BIG, f32)
    acc_ref[...] = jnp.zeros(acc_ref.shape, f32)
    rs = min(ROW_SLICE, tq)

    def attend_chunk(c, seen):
        blocks = [keys_ref[c, :, j * LANES:(j + 1) * LANES] for j in range(kc // LANES)]
        sels, seen = _selected_blocks(blocks, thr, need, seen, tri)
        for j, sel in enumerate(sels):
            bias_ref[:, j * LANES:(j + 1) * LANES] = jnp.where(sel, 0.0, NEG_BIG)
        for g in range(N_KV_HEADS):
            lg_ref[...] = jnp.dot(q_g[g], kt_ref[0, g, c], preferred_element_type=f32)
            for r0 in range(0, GROUP * tq, rs):
                x = lg_ref[r0:r0 + rs] + bias_ref[r0 % tq:r0 % tq + rs]
                m_old = m_ref[g, r0:r0 + rs]
                m_new = jnp.maximum(m_old, jnp.max(x, axis=-1, keepdims=True))
                p_ref[r0:r0 + rs] = jnp.exp2(x - m_new).astype(bf16)
                acc_ref[g, r0:r0 + rs] = jnp.exp2(m_old - m_new) * acc_ref[g, r0:r0 + rs]
                m_ref[g, r0:r0 + rs] = m_new
            acc_ref[g] += jnp.dot(p_ref[...], vd_ref[0, g, c], preferred_element_type=f32)
        return seen

    lax.fori_loop(0, n_chunks, attend_chunk, jnp.zeros((tq, 1), f32))

    for g in range(N_KV_HEADS):
        acc = acc_ref[g]
        out = acc / acc[:, HEAD_DIM:HEAD_DIM + 1]
        for jp in range(GROUP // 2):
            even = out[(2 * jp) * tq:(2 * jp + 1) * tq]
            odd = pltpu.roll(out[(2 * jp + 1) * tq:(2 * jp + 2) * tq], HEAD_DIM, 1)
            o_ref[0, :, (g * 2 + jp) * LANES:(g * 2 + jp + 1) * LANES] = jnp.where(lo, even, odd)


def dsa_prompt(q, qi, sm, kit2, kt2, vd):
    b, t = q.shape[:2]
    nc, kc = kit2.shape[1], kit2.shape[3]
    tq = min(TQ, t)
    n_sel = min(TOPK_MAX, t // 4)
    tok = lambda w: pl.BlockSpec((1, tq, w), lambda bi, qb: (bi, qb, 0))
    return pl.pallas_call(
        functools.partial(_dsa_prompt_kernel, n_sel=n_sel),
        grid=(b, t // tq),
        in_specs=[tok(A_WIDTH), tok(IDX_HEADS * IDX_DIM), tok(LANES),
                  pl.BlockSpec((1, nc, LANES, kc), lambda bi, qb: (bi, 0, 0, 0)),
                  pl.BlockSpec((1, N_KV_HEADS, nc, LANES, kc), lambda bi, qb: (bi, 0, 0, 0, 0)),
                  pl.BlockSpec((1, N_KV_HEADS, nc, kc, LANES), lambda bi, qb: (bi, 0, 0, 0, 0))],
        out_specs=tok(A_WIDTH),
        out_shape=jax.ShapeDtypeStruct((b, t, A_WIDTH), f32),
        scratch_shapes=[pltpu.VMEM((nc, tq, kc), i32),
                        pltpu.VMEM((tq, kc), f32),
                        pltpu.VMEM((GROUP * tq, kc), f32),
                        pltpu.VMEM((GROUP * tq, kc), bf16),
                        pltpu.VMEM((N_KV_HEADS, GROUP * tq, 1), f32),
                        pltpu.VMEM((N_KV_HEADS, GROUP * tq, LANES), f32)],
        compiler_params=_cparams(("parallel", "arbitrary")),
        name="dsa_prompt",
    )(q, qi, sm, kit2, kt2, vd)


def dsa_prompt_branch(q, qi, sm, k, v):
    b, t = q.shape[:2]
    kc = min(KC, t)
    nc = t // kc
    kit = sm[:, :, :IDX_DIM].astype(bf16).reshape(b, nc, kc, IDX_DIM).transpose(0, 1, 3, 2)
    kit2 = jnp.concatenate([kit, kit], axis=2)
    kt = k.astype(bf16).reshape(b, nc, kc, N_KV_HEADS, HEAD_DIM).transpose(0, 3, 1, 4, 2)
    kt2 = jnp.concatenate([kt, kt], axis=3)
    vb = v.astype(bf16).reshape(b, nc, kc, N_KV_HEADS, HEAD_DIM).transpose(0, 3, 1, 2, 4)
    vd = jnp.concatenate([vb, jnp.ones_like(vb)], axis=4)
    return dsa_prompt(q, qi, sm, kit2, kt2, vd)


def _idx_scores(qi, w, ktpage):
    s = jnp.dot(qi, ktpage.astype(bf16), preferred_element_type=f32)
    r = (w * jnp.maximum(s, 0.0)).reshape(IDX_HEADS, SUBLANES, s.shape[1])
    sc = r[0]
    for h in range(1, IDX_HEADS):
        sc = sc + r[h]
    return sc


def _page_fetcher(pt_ref, srcs, bufs, sem, g):
    def start(step, slot):
        for j in range(g):
            pg = pt_ref[step * g + j]
            for i, (src, buf) in enumerate(zip(srcs, bufs)):
                pltpu.make_async_copy(src.at[pg], buf.at[slot, j], sem.at[i, slot]).start()

    def wait(slot):
        for i, buf in enumerate(bufs):
            pltpu.make_async_copy(buf.at[slot], buf.at[slot], sem.at[i, slot]).wait()

    return start, wait


def _run_paged(t, n_tot, active, start, wait, nbuf, body):
    @pl.when(active & (t == 0))
    def _():
        for d in range(nbuf - 1):
            pl.when(d < n_tot)(functools.partial(start, d, d))

    def one(slot):
        wait(slot)
        pl.when(t + nbuf - 1 < n_tot)(functools.partial(start, t + nbuf - 1, (slot + nbuf - 1) % nbuf))
        body(slot)

    for slot in range(nbuf):
        pl.when(active & (t % nbuf == slot))(functools.partial(one, slot))


def _ds_score_kernel(pt_ref, qi_ref, w_ref, kin_ref, cki_hbm, keys_ref, kbuf, sem, *, g):
    bi = pl.program_id(0)
    s = pl.program_id(1)
    last = pl.num_programs(1) - 1
    t = bi * last + s
    n_tot = pl.num_programs(0) * last
    qi = qi_ref[0]
    w = w_ref[0]
    start, wait = _page_fetcher(pt_ref, [cki_hbm], [kbuf], sem, g)

    def pages(slot):
        kcat = jnp.concatenate([kbuf[slot, j].astype(bf16) for j in range(g)], axis=1)
        keys_ref[0] = _sortable(_idx_scores(qi, w, kcat))

    _run_paged(t, n_tot, s < last, start, wait, kbuf.shape[0], pages)

    @pl.when(s == last)
    def _():
        keys_ref[0] = jnp.full(keys_ref.shape[1:], INT_MIN, i32)
        sc = _idx_scores(qi, w, kin_ref[0])
        tok = lax.broadcasted_iota(i32, sc.shape, 0)
        j = lax.broadcasted_iota(i32, sc.shape, 1)
        keys_ref[0, :, 0:PAGE_SIZE] = jnp.where(j <= tok, _sortable(sc), INT_MIN)


def _ds_thr_kernel(keys_ref, thr_ref, need_ref, *, n_sel, chunk_w):
    rows, width = keys_ref.shape

    def get_chunk(c):
        return keys_ref[:, c * chunk_w:(c + 1) * chunk_w]

    thr, need = _select_threshold(get_chunk, width // chunk_w, chunk_w, rows, n_sel, False)
    thr_ref[...] = jnp.broadcast_to(thr, thr_ref.shape)
    need_ref[...] = jnp.broadcast_to(need, need_ref.shape)


def _ds_attn_kernel(pt_ref, q_ref, keys_ref, thr_ref, need_ref, knew_ref, vnew_ref, ck_hbm, cv_hbm,
                    o_ref, m_ref, l_ref, acc_ref, seen_ref, kbuf, vbuf, sem, *, g):
    bi = pl.program_id(0)
    s = pl.program_id(1)
    last = pl.num_programs(1) - 1
    t = bi * last + s
    n_tot = pl.num_programs(0) * last
    start, wait = _page_fetcher(pt_ref, [ck_hbm, cv_hbm], [kbuf, vbuf], sem, g)
    q = q_ref[0]
    thr = thr_ref[0][:, 0:1]
    need = need_ref[0][:, 0:1]
    tri = _tie_prefix_matrix()

    def process(keys, kps, vps):
        n = len(kps)
        seen = seen_ref[...]
        sels, seen = _selected_blocks([keys[:, j * PAGE_SIZE:(j + 1) * PAGE_SIZE] for j in range(n)],
                                      thr, need, seen, tri)
        bias = [jnp.where(sel, 0.0, NEG_BIG) for sel in sels]
        seen_ref[...] = seen
        bias = jnp.concatenate(bias, axis=1) if n > 1 else bias[0]
        kcat = jnp.concatenate([kp.astype(bf16) for kp in kps], axis=1) if n > 1 else kps[0].astype(bf16)
        vcat = jnp.concatenate([vp.astype(bf16) for vp in vps], axis=1) if n > 1 else vps[0].astype(bf16)
        lg = jnp.dot(q, kcat, preferred_element_type=f32)
        lg = lg.reshape(N_HEADS, SUBLANES, n * PAGE_SIZE) + bias[None]
        m_old = m_ref[...]
        m_new = jnp.maximum(m_old, jnp.max(lg, axis=-1, keepdims=True))
        p = jnp.exp(lg - m_new)
        alpha = jnp.exp(m_old - m_new)
        l_ref[...] = alpha * l_ref[...] + jnp.sum(p, axis=-1, keepdims=True)
        pb = p.reshape(N_HEADS * SUBLANES, n * PAGE_SIZE).astype(bf16)
        pv = lax.dot_general(pb, vcat, (((1,), (1,)), ((), ())), preferred_element_type=f32)
        acc_ref[...] = alpha * acc_ref[...] + pv.reshape(N_HEADS, SUBLANES, LANES)
        m_ref[...] = m_new

    @pl.when(s == 0)
    def _():
        m_ref[...] = jnp.full(m_ref.shape, NEG_BIG, f32)
        l_ref[...] = jnp.zeros(l_ref.shape, f32)
        acc_ref[...] = jnp.zeros(acc_ref.shape, f32)
        seen_ref[...] = jnp.zeros(seen_ref.shape, f32)

    def pages(slot):
        process(keys_ref[0], [kbuf[slot, j] for j in range(g)], [vbuf[slot, j] for j in range(g)])

    _run_paged(t, n_tot, s < last, start, wait, kbuf.shape[0], pages)

    @pl.when(s == last)
    def _():
        process(keys_ref[0, :, 0:PAGE_SIZE], [knew_ref[0]], [vnew_ref[0]])
        o_ref[0] = (acc_ref[...] / l_ref[...]).reshape(N_HEADS * SUBLANES, LANES)


def dsa_sample(q, qi, sm, k, v, cache_k, cache_v, cache_idx_k, page_table):
    b, t = q.shape[:2]
    assert t <= SUBLANES
    n_pages = page_table.shape[1]
    g = min(PAGES_PER_STEP, n_pages)
    assert n_pages % g == 0
    ns = n_pages // g
    past = n_pages * PAGE_SIZE
    n_sel = min(TOPK_MAX, (past + t) // 4)
    pt = page_table.reshape(-1).astype(i32)
    padt = lambda a: jnp.pad(a, ((0, 0), (0, SUBLANES - t)) + ((0, 0),) * (a.ndim - 2))

    qi_r = padt(qi.reshape(b, t, IDX_HEADS, IDX_DIM)).transpose(0, 2, 1, 3).reshape(b, IDX_HEADS * SUBLANES, IDX_DIM)
    w_r = padt(sm[:, :, SM_WI:SM_WI + IDX_HEADS]).transpose(0, 2, 1).reshape(b, IDX_HEADS * SUBLANES, 1)
    padk = lambda a: jnp.pad(a, ((0, 0), (0, PAGE_SIZE - t), (0, 0))).transpose(0, 2, 1)
    ki_new = padk(sm[:, :, :IDX_DIM])
    k_new = padk(k)
    v_new = padk(v)
    qh = padt(q.reshape(b, t, N_HEADS, HEAD_DIM)).transpose(0, 2, 1, 3) * (HEAD_DIM ** -0.5)
    grp = (jnp.arange(N_HEADS) // GROUP)[None, :, None, None]
    q_r = jnp.concatenate([jnp.where(grp == 0, qh, 0.0), jnp.where(grp == 1, qh, 0.0)], axis=-1)
    q_r = q_r.reshape(b, N_HEADS * SUBLANES, LANES).astype(bf16)
    n_pool = cache_k.shape[0]
    ck = cache_k.transpose(0, 2, 3, 1).reshape(n_pool, N_KV_HEADS * HEAD_DIM, PAGE_SIZE)
    cv = cache_v.transpose(0, 2, 3, 1).reshape(n_pool, N_KV_HEADS * HEAD_DIM, PAGE_SIZE)
    cki = cache_idx_k.transpose(0, 2, 1)

    hbm = pl.BlockSpec(memory_space=pl.ANY)
    per_b = lambda a: pl.BlockSpec((1,) + a.shape[1:], lambda bi, s, ptr: (bi,) + (0,) * (a.ndim - 1))
    blk_w = g * PAGE_SIZE
    keys_spec = pl.BlockSpec((1, SUBLANES, blk_w), lambda bi, s, ptr: (bi, 0, s))
    width = past + blk_w

    keys = pl.pallas_call(
        functools.partial(_ds_score_kernel, g=g),
        grid_spec=pltpu.PrefetchScalarGridSpec(
            num_scalar_prefetch=1, grid=(b, ns + 1),
            in_specs=[per_b(qi_r), per_b(w_r), per_b(ki_new), hbm],
            out_specs=keys_spec,
            scratch_shapes=[pltpu.VMEM((PAGE_SLOTS, g, IDX_DIM, PAGE_SIZE), f32),
                            pltpu.SemaphoreType.DMA((1, PAGE_SLOTS))]),
        out_shape=jax.ShapeDtypeStruct((b, SUBLANES, width), i32),
        compiler_params=_cparams(("arbitrary", "arbitrary")),
        name="dsa_sample_scores",
    )(pt, qi_r.astype(bf16), w_r, ki_new, cki)

    rows = b * SUBLANES
    thr, need = pl.pallas_call(
        functools.partial(_ds_thr_kernel, n_sel=n_sel, chunk_w=blk_w),
        out_shape=[jax.ShapeDtypeStruct((rows, LANES), i32), jax.ShapeDtypeStruct((rows, LANES), f32)],
        compiler_params=pltpu.CompilerParams(vmem_limit_bytes=VMEM_LIMIT),
        name="dsa_sample_threshold",
    )(keys.reshape(rows, width))
    thr = thr.reshape(b, SUBLANES, LANES)
    need = need.reshape(b, SUBLANES, LANES)

    out = pl.pallas_call(
        functools.partial(_ds_attn_kernel, g=g),
        grid_spec=pltpu.PrefetchScalarGridSpec(
            num_scalar_prefetch=1, grid=(b, ns + 1),
            in_specs=[per_b(q_r), keys_spec, per_b(thr), per_b(need), per_b(k_new), per_b(v_new), hbm, hbm],
            out_specs=per_b(q_r),
            scratch_shapes=[pltpu.VMEM((N_HEADS, SUBLANES, 1), f32), pltpu.VMEM((N_HEADS, SUBLANES, 1), f32),
                            pltpu.VMEM((N_HEADS, SUBLANES, LANES), f32), pltpu.VMEM((SUBLANES, 1), f32),
                            pltpu.VMEM((PAGE_SLOTS, g, LANES, PAGE_SIZE), f32),
                            pltpu.VMEM((PAGE_SLOTS, g, LANES, PAGE_SIZE), f32),
                            pltpu.SemaphoreType.DMA((2, PAGE_SLOTS))]),
        out_shape=jax.ShapeDtypeStruct((b, N_HEADS * SUBLANES, LANES), f32),
        compiler_params=_cparams(("arbitrary", "arbitrary")),
        name="dsa_sample_attention",
    )(pt, q_r, keys, thr, need, k_new, v_new, ck, cv)
    out = out.reshape(b, N_HEADS, SUBLANES, N_KV_HEADS, HEAD_DIM)[:, :, :t]
    out = jnp.concatenate([out[:, :GROUP, :, 0], out[:, GROUP:, :, 1]], axis=1)
    return out.transpose(0, 2, 1, 3).reshape(b, t, A_WIDTH)


def _mid_kernel(x_ref, oa_ref, ob_ref, ga_ref, gb_ref, wa_ref, wb_ref, wo_ref, gf_ref, wrh_ref, wrl_ref, br_ref,
                x1_ref, h2_ref, te_ref, tg_ref):
    a = jnp.dot(oa_ref[...].astype(bf16), wa_ref[...], preferred_element_type=f32)
    b = jnp.dot(ob_ref[...].astype(bf16), wb_ref[...], preferred_element_type=f32)
    merged = ga_ref[...] * a + gb_ref[...] * b
    x1 = x_ref[...] + jnp.dot(merged.astype(bf16), wo_ref[...], preferred_element_type=f32)
    x1_ref[...] = x1
    h2 = _rms(x1, gf_ref[...])
    h2_ref[...] = h2
    hi = h2.astype(bf16)
    lo = (h2 - hi.astype(f32)).astype(bf16)
    lg = (jnp.dot(hi, wrh_ref[...], preferred_element_type=f32) + jnp.dot(lo, wrh_ref[...], preferred_element_type=f32)
          + jnp.dot(hi, wrl_ref[...], preferred_element_type=f32)) + br_ref[...]
    lane = lax.broadcasted_iota(i32, lg.shape, 1)
    lane_f = lane.astype(f32)
    vals, ids = [], []
    for _ in range(TOP_K):
        m = jnp.max(lg, axis=1, keepdims=True)
        idx = jnp.min(jnp.where(lg == m, lane_f, float(LANES)), axis=1, keepdims=True).astype(i32)
        vals.append(m)
        ids.append(idx)
        lg = jnp.where(lane == idx, -jnp.inf, lg)
    ex = [jnp.exp(v - vals[0]) for v in vals]
    tot = ex[0] + ex[1] + ex[2] + ex[3]
    te = jnp.zeros(lg.shape, i32)
    tg = jnp.zeros(lg.shape, f32)
    for j in range(TOP_K):
        te = jnp.where(lane == j, ids[j], te)
        tg = jnp.where(lane == j, ex[j] / tot, tg)
    te_ref[...] = te
    tg_ref[...] = tg


def mid(x, oa, ob, ga, gb, wa, wb, wo, g_ffn, wr_hi, wr_lo, br):
    n = x.shape[0]
    tm = min(256, n)
    assert n % tm == 0
    row = lambda w: pl.BlockSpec((tm, w), lambda i: (i, 0))
    full = lambda a: pl.BlockSpec(a.shape, lambda i: (0,) * a.ndim)
    return pl.pallas_call(
        _mid_kernel,
        grid=(n // tm,),
        in_specs=[row(D_MODEL), row(A_WIDTH), row(M_WIDTH), row(D_MODEL), row(D_MODEL),
                  full(wa), full(wb), full(wo), full(g_ffn), full(wr_hi), full(wr_lo), full(br)],
        out_specs=[row(D_MODEL), row(D_MODEL), row(LANES), row(LANES)],
        out_shape=[jax.ShapeDtypeStruct((n, D_MODEL), f32), jax.ShapeDtypeStruct((n, D_MODEL), f32),
                   jax.ShapeDtypeStruct((n, LANES), i32), jax.ShapeDtypeStruct((n, LANES), f32)],
        compiler_params=_cparams(("parallel",)),
        name="mid",
    )(x, oa, ob, ga, gb, wa, wb, wo, g_ffn, wr_hi, wr_lo, br)


def _rank_kernel(te_ref, dest_ref, meta_ref, cnt_ref, carry_ref, *, blk):
    ph = pl.program_id(0)
    i = pl.program_id(1)
    tm = te_ref.shape[0]
    lane = lax.broadcasted_iota(i32, (tm, LANES), 1)
    te = te_ref[...]
    oh = jnp.zeros((tm, LANES), f32)
    for j in range(TOP_K):
        oh = oh + jnp.where(lane == te[:, j:j + 1], 1.0, 0.0)
    tile_cnt = jnp.sum(oh, axis=0, keepdims=True)

    @pl.when((ph == 0) & (i == 0))
    def _():
        cnt_ref[...] = jnp.zeros(cnt_ref.shape, f32)

    @pl.when(ph == 0)
    def _():
        cnt_ref[...] += tile_cnt

    @pl.when((ph == 1) & (i == 0))
    def _():
        cnt = cnt_ref[...]
        padded = jnp.floor((cnt + (blk - 1)) / blk) * blk
        r = lax.broadcasted_iota(i32, (LANES, LANES), 0)
        c = lax.broadcasted_iota(i32, (LANES, LANES), 1)
        col = jnp.sum(jnp.where(r == c, padded, 0.0), axis=1, keepdims=True)
        start = jnp.sum(jnp.where(r < c, col, 0.0), axis=0, keepdims=True)
        carry_ref[...] = start
        pad_end = start + padded
        nbp = meta_ref.shape[0]
        jb = (lax.broadcasted_iota(i32, (nbp, LANES), 0) * blk).astype(f32)
        lane2 = lax.broadcasted_iota(i32, (nbp, LANES), 1)
        be = jnp.sum(jnp.where((pad_end <= jb) & (lane2 < N_EXPERTS), 1.0, 0.0), axis=1, keepdims=True)
        be = jnp.minimum(be, float(N_EXPERTS - 1))
        n_act = jnp.sum(jnp.where(lane2 == N_EXPERTS - 1, pad_end, 0.0), axis=1, keepdims=True) / blk
        meta_ref[...] = jnp.where(lane2 == 0, be, jnp.where(lane2 == 1, n_act, 0.0)).astype(i32)

    @pl.when(ph == 1)
    def _():
        r = lax.broadcasted_iota(i32, (tm, tm), 0)
        c = lax.broadcasted_iota(i32, (tm, tm), 1)
        before = jnp.where(c < r, 1.0, 0.0).astype(bf16)
        pos = carry_ref[...] + jnp.dot(before, oh.astype(bf16), preferred_element_type=f32)
        d = jnp.zeros((tm, LANES), f32)
        for j in range(TOP_K):
            dj = jnp.sum(jnp.where(lane == te[:, j:j + 1], pos, 0.0), axis=1, keepdims=True)
            d = jnp.where(lane == j, dj, d)
        dest_ref[...] = d.astype(i32)
        carry_ref[...] += tile_cnt


def rank(te, blk, n_blocks):
    n = te.shape[0]
    tm = next(c for c in (512, 384, 256, 128, 64, 32, 16, SUBLANES) if n % c == 0)
    nbp = -(-n_blocks // SUBLANES) * SUBLANES
    return pl.pallas_call(
        functools.partial(_rank_kernel, blk=blk),
        grid=(2, n // tm),
        in_specs=[pl.BlockSpec((tm, LANES), lambda ph, i: (i, 0))],
        out_specs=[pl.BlockSpec((tm, LANES), lambda ph, i: (i * ph, 0)),
                   pl.BlockSpec((nbp, LANES), lambda ph, i: (0, 0))],
        out_shape=[jax.ShapeDtypeStruct((n, LANES), i32), jax.ShapeDtypeStruct((nbp, LANES), i32)],
        scratch_shapes=[pltpu.VMEM((1, LANES), f32), pltpu.VMEM((1, LANES), f32)],
        compiler_params=_cparams(("arbitrary", "arbitrary")),
        name="rank",
    )(te)


def _ffn_kernel(be_ref, nact_ref, rt_ref, x_hbm, *rest):
    wgu_refs = rest[:W_SPLIT]
    bgu_ref = rest[W_SPLIT]
    wdn_refs = rest[W_SPLIT + 1:2 * W_SPLIT + 1]
    bdn_ref, y_ref, xbuf0, xbuf1, sem, wgu_bf, wdn_bf = rest[2 * W_SPLIT + 1:]
    xbufs = (xbuf0, xbuf1)
    i = pl.program_id(0)
    n_act = nact_ref[0]
    blk = xbuf0.shape[0]

    def row_copy(b, r, slot):
        tok = rt_ref[b * blk + r]
        return pltpu.make_async_copy(x_hbm.at[pl.ds(tok, 1)], xbufs[slot].at[pl.ds(r, 1)], sem.at[slot])

    def wait_slot(slot):
        pltpu.make_async_copy(xbufs[slot], xbufs[slot], sem.at[slot]).wait()

    @pl.when(i == 0)
    def _():
        def body(r, c):
            row_copy(0, r, 0).start()
            return c
        lax.fori_loop(0, blk, body, 0)

    for slot in range(2):
        pl.when((i == n_act) & (i % 2 == slot))(functools.partial(wait_slot, slot))

    @pl.when(i >= n_act)
    def _():
        y_ref[...] = jnp.zeros(y_ref.shape, f32)

    @pl.when((i < n_act) & ((i == 0) | (be_ref[i] != be_ref[jnp.maximum(i - 1, 0)])))
    def _():
        wg = wgu_bf.shape[1] // W_SPLIT
        wd = wdn_bf.shape[1] // W_SPLIT
        for c in range(W_SPLIT):
            wgu_bf[:, c * wg:(c + 1) * wg] = wgu_refs[c][0].astype(bf16)
            wdn_bf[:, c * wd:(c + 1) * wd] = wdn_refs[c][0].astype(bf16)

    def block(slot):
        wait_slot(slot)
        for r in range(blk):
            row_copy(i + 1, r, 1 - slot).start()
        x = xbufs[slot][...].astype(bf16)
        gu = jnp.dot(x, wgu_bf[...], preferred_element_type=f32) + bgu_ref[0]
        gate = jnp.minimum(gu[:, :EXPERT_FF], SWIGLU_LIMIT)
        up = jnp.clip(gu[:, EXPERT_FF:], -SWIGLU_LIMIT, SWIGLU_LIMIT)
        act = (up + 1.0) * (gate * _sigmoid(SWIGLU_ALPHA * gate))
        y_ref[...] = jnp.dot(act.astype(bf16), wdn_bf[...], preferred_element_type=f32) + bdn_ref[0]

    for slot in range(2):
        pl.when((i < n_act) & (i % 2 == slot))(functools.partial(block, slot))


def ffn(blk_e, n_act, row_tok, h2, w_gu, b_gu, w_dn, b_dn, blk, n_blocks):
    d = h2.shape[1]
    ff2 = w_gu.shape[2]
    return pl.pallas_call(
        _ffn_kernel,
        grid_spec=pltpu.PrefetchScalarGridSpec(
            num_scalar_prefetch=3, grid=(n_blocks,),
            in_specs=[pl.BlockSpec(memory_space=pl.ANY)]
            + [pl.BlockSpec((1, d, ff2 // W_SPLIT), functools.partial(lambda i, be, na, rt, c: (be[i], 0, c), c=c))
               for c in range(W_SPLIT)]
            + [pl.BlockSpec((1, 1, ff2), lambda i, be, na, rt: (be[i], 0, 0))]
            + [pl.BlockSpec((1, ff2 // 2, d // W_SPLIT), functools.partial(lambda i, be, na, rt, c: (be[i], 0, c), c=c))
               for c in range(W_SPLIT)]
            + [pl.BlockSpec((1, 1, d), lambda i, be, na, rt: (be[i], 0, 0))],
            out_specs=pl.BlockSpec((blk, d), lambda i, be, na, rt: (i, 0)),
            scratch_shapes=[pltpu.VMEM((blk, d), f32), pltpu.VMEM((blk, d), f32), pltpu.SemaphoreType.DMA((2,)),
                            pltpu.VMEM((d, ff2), bf16), pltpu.VMEM((ff2 // 2, d), bf16)]),
        out_shape=jax.ShapeDtypeStruct((n_blocks * blk, d), f32),
        compiler_params=_cparams(("arbitrary",)),
        name="ffn",
    )(blk_e, n_act, row_tok, h2, *([w_gu] * W_SPLIT), b_gu.reshape(b_gu.shape[0], 1, ff2),
      *([w_dn] * W_SPLIT), b_dn.reshape(b_dn.shape[0], 1, d))


def _out_kernel(dest_ref, x1_ref, tg_ref, p_ref, ys_hbm, gp_ref, wg_ref, wp_ref, gfin_ref, y_ref,
                buf0, buf1, sem):
    bufs = (buf0, buf1)
    i = pl.program_id(0)
    n = pl.num_programs(0)
    tm = x1_ref.shape[0]

    def row_copy(t, r, j, slot):
        d = dest_ref[(t * tm + r) * TOP_K + j]
        return pltpu.make_async_copy(ys_hbm.at[pl.ds(d, 1)], bufs[slot].at[j, pl.ds(r, 1)], sem.at[slot])

    def wait_slot(slot):
        pltpu.make_async_copy(bufs[slot], bufs[slot], sem.at[slot]).wait()

    @pl.when(i == 0)
    def _():
        def body(r, c):
            for j in range(TOP_K):
                row_copy(0, r, j, 0).start()
            return c
        lax.fori_loop(0, tm, body, 0)

    def tile(slot):
        wait_slot(slot)
        nxt = jnp.minimum(i + 1, n - 1)
        for r in range(tm):
            for j in range(TOP_K):
                row_copy(nxt, r, j, 1 - slot).start()
        tg = tg_ref[...]
        x2 = x1_ref[...]
        for j in range(TOP_K):
            x2 = x2 + tg[:, j:j + 1] * bufs[slot][j]
        hn = _rms(x2, gp_ref[...]).astype(bf16)
        gate = _sigmoid(jnp.dot(hn, wg_ref[...], preferred_element_type=f32))
        x3 = x2 + gate * jnp.dot(p_ref[...].astype(bf16), wp_ref[...], preferred_element_type=f32)
        y_ref[...] = _rms(x3, gfin_ref[...])

    for slot in range(2):
        pl.when(i % 2 == slot)(functools.partial(tile, slot))

    for slot in range(2):
        pl.when((i == n - 1) & (i % 2 == slot))(functools.partial(wait_slot, 1 - slot))


def out_stage(dest_flat, x1, tg, p, ys, g_ple, wg, wp, g_final):
    n, d = x1.shape
    tm = min(OUT_TM, n)
    assert n % tm == 0
    row = lambda w: pl.BlockSpec((tm, w), lambda i, ds: (i, 0))
    full = lambda a: pl.BlockSpec(a.shape, lambda i, ds: (0,) * a.ndim)
    return pl.pallas_call(
        _out_kernel,
        grid_spec=pltpu.PrefetchScalarGridSpec(
            num_scalar_prefetch=1, grid=(n // tm,),
            in_specs=[row(d), row(LANES), row(p.shape[1]), pl.BlockSpec(memory_space=pl.ANY),
                      full(g_ple), full(wg), full(wp), full(g_final)],
            out_specs=row(d),
            scratch_shapes=[pltpu.VMEM((TOP_K, tm, d), f32), pltpu.VMEM((TOP_K, tm, d), f32),
                            pltpu.SemaphoreType.DMA((2,))]),
        out_shape=jax.ShapeDtypeStruct((n, d), f32),
        compiler_params=_cparams(("arbitrary",)),
        name="out",
    )(dest_flat, x1, tg, p, ys, g_ple, wg, wp, g_final)


def mid_stage(x, oa, ob, ga, gb, w):
    return mid(x, oa, ob, ga, gb, w["wa"], w["wb"], w["wo"], w["g_ffn"], w["wr_hi"], w["wr_lo"], w["br"])


def moe_out(parts, w):
    h2 = jnp.concatenate([pt[1] for pt in parts], axis=0)
    te = jnp.concatenate([pt[2] for pt in parts], axis=0)
    n = h2.shape[0]
    blk = FFN_BLK
    n_blocks = -(-(n * TOP_K) // blk) + N_EXPERTS
    dest, meta = rank(te, blk, n_blocks)
    dest_flat = dest[:, :TOP_K].reshape(-1)
    blk_e = meta[:n_blocks, 0]
    n_act = meta[0:1, 1]
    tok = jnp.repeat(jnp.arange(n, dtype=i32), TOP_K)
    row_tok = jnp.zeros((n_blocks * blk,), i32).at[dest_flat].set(tok)
    ys = ffn(blk_e, n_act, row_tok, h2, w["w_gu"], w["b_gu"], w["w_dn"], w["b_dn"], blk, n_blocks)
    outs, off = [], 0
    for x1, _, _, tg, p in parts:
        ni = x1.shape[0]
        outs.append(out_stage(dest_flat[off * TOP_K:(off + ni) * TOP_K], x1, tg, p, ys,
                              w["g_ple"], w["wg"], w["wp"], w["g_final"]))
        off += ni
    return outs


def tail(x, oa, ob, ga, gb, p, w):
    return moe_out([mid_stage(x, oa, ob, ga, gb, w) + (p,)], w)[0]


def _prep_weights(g_mix, w_in, b_gate, g_mnorm, w_up_a, w_up_b, w_out, g_ffn, w_router, b_router,
                  w_gu, b_gu, w_dn, b_dn, g_ple, w_ple_gate, w_ple_proj, g_final):
    wr = jnp.pad(w_router.astype(f32), ((0, 0), (0, LANES - N_EXPERTS)))
    wr_hi = wr.astype(bf16)
    wr_lo = (wr - wr_hi.astype(f32)).astype(bf16)
    br = jnp.full((1, LANES), -jnp.inf, f32).at[0, :N_EXPERTS].set(b_router.astype(f32))
    bias = jnp.zeros((1, LANES), f32).at[0, SM_IG:SM_IG + 2 * M_HEADS].set(b_gate.astype(f32))
    return dict(g_mix=g_mix.reshape(1, -1), w_in=_pack_w_in(w_in), bias=bias, g_mnorm=g_mnorm,
                wa=w_up_a.astype(bf16), wb=w_up_b.astype(bf16), wo=w_out.astype(bf16), g_ffn=g_ffn.reshape(1, -1),
                wr_hi=wr_hi, wr_lo=wr_lo, br=br, w_gu=w_gu, b_gu=b_gu, w_dn=w_dn, b_dn=b_dn,
                g_ple=g_ple.reshape(1, -1), wg=w_ple_gate.astype(bf16), wp=w_ple_proj.astype(bf16),
                g_final=g_final.reshape(1, -1))


def _layer(x, p, pos, attn_fn, c0, n0, m0, w):
    b, t = x.shape[:2]
    n = b * t
    cos, sin = _rope_tables(pos)
    cos = jnp.tile(cos, (b, 1))
    sin = jnp.tile(sin, (b, 1))
    x2d = x.reshape(n, D_MODEL)
    q, k, v, qi, sm, mq, mk, mv, og, ga, gb = in_proj(x2d, cos, sin, w["g_mix"], w["w_in"], w["bias"])
    r3 = lambda a: a.reshape(b, t, a.shape[-1])
    o_a = attn_fn(r3(q), r3(qi), r3(sm), r3(k), r3(v))
    o_b, c, nn, m = mlstm_branch(r3(mq), r3(mk), r3(mv), r3(og), r3(sm), c0, n0, m0, w["g_mnorm"])
    part = mid_stage(x2d, o_a.reshape(n, A_WIDTH), o_b.reshape(n, M_WIDTH), ga, gb, w) + (p.reshape(n, PLE_DIM),)
    state = (k.reshape(b, t, N_KV_HEADS, HEAD_DIM), v.reshape(b, t, N_KV_HEADS, HEAD_DIM),
             sm[:, :IDX_DIM].reshape(b, t, IDX_DIM), c, nn, m)
    return part, state


def kernel(x_prompt, x_sample, cache_k, cache_v, cache_idx_k, state_C, state_n, state_m, page_table,
           p_prompt, p_sample, g_mix, w_in, b_gate, g_mnorm, w_up_a, w_up_b, w_out, g_ffn,
           w_router, b_router, w_gu, b_gu, w_dn, b_dn, g_ple, w_ple_gate, w_ple_proj, g_final):
    assert x_prompt.shape[-1] == D_MODEL and w_in.shape[0] == 1, "single-layer model of width D_MODEL"
    bp, tp = x_prompt.shape[:2]
    ts = x_sample.shape[1]
    past = page_table.shape[1] * PAGE_SIZE
    w = _prep_weights(g_mix[0], w_in[0], b_gate[0], g_mnorm[0], w_up_a[0], w_up_b[0], w_out[0], g_ffn[0],
                      w_router[0], b_router[0], w_gu[0], b_gu[0], w_dn[0], b_dn[0], g_ple[0],
                      w_ple_gate[0], w_ple_proj[0], g_final)
    zeros = lambda *s: jnp.zeros(s, f32)
    part_p, sp = _layer(x_prompt, p_prompt[0], jnp.arange(tp, dtype=i32), dsa_prompt_branch,
                        zeros(bp, M_HEADS, M_V, M_QK), zeros(bp, M_HEADS, M_QK), zeros(bp, M_HEADS), w)
    attn_s = functools.partial(dsa_sample, cache_k=cache_k[0], cache_v=cache_v[0], cache_idx_k=cache_idx_k[0],
                               page_table=page_table)
    part_s, ss = _layer(x_sample, p_sample[0], past + jnp.arange(ts, dtype=i32), attn_s,
                        state_C[0], state_n[0], state_m[0], w)
    yp, ys = moe_out([part_p, part_s], w)
    return ((yp.reshape(x_prompt.shape), ys.reshape(x_sample.shape))
            + tuple(s[None] for s in sp) + tuple(s[None] for s in ss))
```

```python
import functools

import jax
import jax.numpy as jnp
from jax import lax
from jax.experimental import pallas as pl
from jax.experimental.pallas import tpu as pltpu

f32 = jnp.float32
bf16 = jnp.bfloat16
i32 = jnp.int32

D_MODEL = 1024
PAGE_SIZE = 128
N_HEADS = 8
N_KV_HEADS = 2
HEAD_DIM = 64
GROUP = N_HEADS // N_KV_HEADS
IDX_HEADS = 4
IDX_DIM = 64
TOPK_MAX = 256
ROPE_THETA = 10000.0
M_HEADS = 4
M_QK = 64
M_V = 128
N_EXPERTS = 32
TOP_K = 4
EXPERT_FF = D_MODEL
SWIGLU_LIMIT = 7.0
SWIGLU_ALPHA = 1.702
PLE_DIM = 256
EPS = 1e-6
A_WIDTH = N_HEADS * HEAD_DIM
M_WIDTH = M_HEADS * M_V

LANES = 128
SUBLANES = 8
VMEM_LIMIT = 56 * 1024 * 1024

M_CHUNK = 128
MLSTM_BATCH = 1
TQ = 256
KC = 1024
PAGE_SLOTS = 4
PAGES_PER_STEP = 16
FFN_BLK = 256
W_SPLIT = 4
OUT_TM = 128

INT_MIN = -2147483648
INT_MAX = 2147483647
NEG_BIG = -1e30
LOG2E = 1.4426950408889634
ROW_SLICE = 16

SM_KI = 0
SM_WI = IDX_DIM
SM_IG = SM_WI + IDX_HEADS
SM_LF = SM_IG + M_HEADS


def _cparams(sem):
    return pltpu.CompilerParams(dimension_semantics=sem, vmem_limit_bytes=VMEM_LIMIT)


def _rms(x, g):
    return x * lax.rsqrt(jnp.mean(x * x, axis=-1, keepdims=True) + EPS) * g


def _sigmoid(x):
    return 1.0 / (1.0 + jnp.exp(-x))


def _log_sigmoid(x):
    return jnp.minimum(x, 0.0) - jnp.log1p(jnp.exp(-jnp.abs(x)))


def _sortable(x):
    bits = lax.bitcast_convert_type(x, i32)
    key = bits ^ ((bits >> 31) & INT_MAX)
    return jnp.where(x == 0.0, 0, key)


_G_Q = (0, 512)
_G_K = (512, 640)
_G_V = (640, 768)
_G_QI = (768, 1024)
_G_SM = (1024, 1152)
_G_MQ = (1152, 1408)
_G_MK = (1408, 1664)
_G_MV = (1664, 2176)
_G_MO = (2176, 2688)
_G_GA = (2688, 3712)
_G_GB = (3712, 4736)
_W_COLS = 4736


def _pack_w_in(w_in):
    o = [0]
    for s in (A_WIDTH, 128, 128, 256, 64, 4, 256, 256, 512, 4, 4, 512, 1024, 1024):
        o.append(o[-1] + s)
    aq, ak, av, iq, ik, iw, mq, mk, mv, mi, mf, mo, ga, gb = [w_in[:, o[i]:o[i + 1]] for i in range(14)]
    pad = jnp.zeros((w_in.shape[0], LANES - (IDX_DIM + IDX_HEADS + 2 * M_HEADS)), w_in.dtype)
    small = jnp.concatenate([ik, iw, mi, mf, pad], axis=1)
    w = jnp.concatenate([aq, ak, av, iq, small, mq, mk, mv, mo, ga, gb], axis=1)
    return w.astype(bf16)


def _in_kernel(x_ref, cos_ref, sin_ref, g_ref, w_ref, bias_ref,
               q_ref, k_ref, v_ref, qi_ref, sm_ref, mq_ref, mk_ref, mv_ref, og_ref, ga_ref, gb_ref):
    x = x_ref[...]
    hb = _rms(x, g_ref[...]).astype(bf16)
    cos = cos_ref[...]
    sin = sin_ref[...]
    tm = x.shape[0]
    lane = lax.broadcasted_iota(i32, (tm, LANES), 1)
    first_half = (lane % HEAD_DIM) < (HEAD_DIM // 2)

    def rope(z):
        rot = jnp.where(first_half, pltpu.roll(z, LANES - HEAD_DIM // 2, 1), pltpu.roll(z, HEAD_DIM // 2, 1))
        return z * cos + rot * sin

    def proj(grp):
        return jnp.dot(hb, w_ref[:, grp[0]:grp[1]], preferred_element_type=f32)

    z = proj(_G_Q)
    for j in range(4):
        q_ref[:, j * LANES:(j + 1) * LANES] = rope(z[:, j * LANES:(j + 1) * LANES])
    k_ref[...] = rope(proj(_G_K))
    v_ref[...] = proj(_G_V)
    z = proj(_G_QI)
    for j in range(2):
        qi_ref[:, j * LANES:(j + 1) * LANES] = rope(z[:, j * LANES:(j + 1) * LANES])
    z = proj(_G_SM)
    zb = z + bias_ref[...]
    sm = jnp.where(lane < SM_WI, rope(z),
                   jnp.where(lane < SM_IG, z * (IDX_HEADS ** -0.5 * IDX_DIM ** -0.5),
                             jnp.where(lane < SM_LF, zb,
                                       jnp.where(lane < SM_LF + M_HEADS, _log_sigmoid(zb), 0.0))))
    sm_ref[...] = sm
    mq_ref[...] = proj(_G_MQ)
    mk_ref[...] = proj(_G_MK) * (M_QK ** -0.5)
    mv_ref[...] = proj(_G_MV)
    og_ref[...] = _sigmoid(proj(_G_MO))
    ga_ref[...] = _sigmoid(proj(_G_GA))
    gb_ref[...] = _sigmoid(proj(_G_GB))


def _rope_tables(pos):
    half = HEAD_DIM // 2
    inv = ROPE_THETA ** (-jnp.arange(half, dtype=f32) / half)
    ang = pos.astype(f32)[:, None] * inv[None, :]
    cos = jnp.cos(ang)
    sin = jnp.sin(ang)
    cos128 = jnp.tile(cos, (1, 4))
    sin128 = jnp.tile(jnp.concatenate([-sin, sin], axis=1), (1, 2))
    return cos128, sin128


def in_proj(x2d, cos128, sin128, g_mix, w_packed, bias128):
    n = x2d.shape[0]
    tm = min(256, n)
    assert n % tm == 0
    row = lambda w: pl.BlockSpec((tm, w), lambda i: (i, 0))
    full = lambda a: pl.BlockSpec(a.shape, lambda i: (0,) * a.ndim)
    widths = (512, 128, 128, 256, 128, 256, 256, 512, 512, 1024, 1024)
    return pl.pallas_call(
        _in_kernel,
        grid=(n // tm,),
        in_specs=[row(D_MODEL), row(LANES), row(LANES), full(g_mix), full(w_packed), full(bias128)],
        out_specs=[row(w) for w in widths],
        out_shape=[jax.ShapeDtypeStruct((n, w), f32) for w in widths],
        compiler_params=_cparams(("parallel",)),
        name="in_proj",
    )(x2d, cos128, sin128, g_mix, w_packed, bias128)


def _mlstm_kernel(mq_ref, mk_ref, mv_ref, og_ref, sm_ref, gr_ref, c0_ref, n0_ref, m0_ref, gn_ref,
                  ob_ref, c_ref, n_ref, m_ref):
    ci = pl.program_id(1)
    L = mq_ref.shape[1]

    @pl.when(ci == 0)
    def _():
        c_ref[...] = c0_ref[...]
        n_ref[...] = n0_ref[...]
        m_ref[...] = m0_ref[...]

    row = lax.broadcasted_iota(i32, (L, L), 0)
    col = lax.broadcasted_iota(i32, (L, L), 1)
    tril = row >= col
    for bi, hd in [(bi, hd) for bi in range(mq_ref.shape[0]) for hd in range(M_HEADS)]:
        sm = sm_ref[bi]
        gr = gr_ref[bi]
        q = mq_ref[bi, :, hd * M_QK:(hd + 1) * M_QK]
        k = mk_ref[bi, :, hd * M_QK:(hd + 1) * M_QK]
        v = mv_ref[bi, :, hd * M_V:(hd + 1) * M_V]
        ig_r = gr[hd:hd + 1, :]
        lf_r = gr[M_HEADS + hd:M_HEADS + hd + 1, :]
        ig_c = sm[:, SM_IG + hd:SM_IG + hd + 1]
        lf_c = sm[:, SM_LF + hd:SM_LF + hd + 1]
        C = c_ref[bi, hd]
        nrow = n_ref[bi, hd]
        m_prev = m_ref[bi, hd]
        b_c = jnp.sum(jnp.where(tril, lf_r, 0.0), axis=1, keepdims=True)
        b_r = jnp.sum(jnp.where(tril, 0.0, lf_c) + jnp.where(row == col, lf_c, 0.0), axis=0, keepdims=True)
        dmat = jnp.where(tril, b_c - b_r + ig_r, -jnp.inf)
        inter = b_c + m_prev
        m_t = jnp.maximum(inter, jnp.max(dmat, axis=1, keepdims=True))
        qb = q.astype(bf16)
        kb = k.astype(bf16)
        qk = lax.dot_general(qb, kb, (((1,), (1,)), ((), ())), preferred_element_type=f32)
        s = qk * jnp.exp(dmat - m_t)
        w_inter = jnp.exp(inter - m_t)
        qc = lax.dot_general(qb, C.astype(bf16), (((1,), (1,)), ((), ())), preferred_element_type=f32)
        num = jnp.dot(s.astype(bf16), v.astype(bf16), preferred_element_type=f32) + w_inter * qc
        den = jnp.sum(s, axis=1, keepdims=True) + w_inter * jnp.sum(q * nrow, axis=1, keepdims=True)
        h = num / jnp.maximum(jnp.abs(den), jnp.exp(-m_t))
        b_last = b_c[L - 1:L, :]
        g_c = b_last - b_c + ig_c
        m_new = jnp.maximum(b_last + m_prev, jnp.max(g_c, axis=0, keepdims=True))
        w_k = jnp.exp(g_c - m_new)
        decay = jnp.exp(b_last + m_prev - m_new)
        wv = (w_k * v).astype(bf16)
        c_ref[bi, hd] = decay * C + lax.dot_general(wv, kb, (((0,), (0,)), ((), ())), preferred_element_type=f32)
        n_ref[bi, hd] = decay * nrow + jnp.sum(w_k * k, axis=0, keepdims=True)
        m_ref[bi, hd] = m_new
        gn = gn_ref[:, hd * M_V:(hd + 1) * M_V]
        ob_ref[bi, :, hd * M_V:(hd + 1) * M_V] = og_ref[bi, :, hd * M_V:(hd + 1) * M_V] * _rms(h, gn)


def mlstm(mq, mk, mv, og, sm, grow, c0, n0, m0, g_mnorm):
    b, t = mq.shape[:2]
    L = M_CHUNK
    bb = MLSTM_BATCH if b % MLSTM_BATCH == 0 else 1
    assert t % L == 0
    tok = lambda w: pl.BlockSpec((bb, L, w), lambda bi, ci: (bi, ci, 0))
    st = lambda a: pl.BlockSpec((bb,) + a.shape[1:], lambda bi, ci: (bi,) + (0,) * (a.ndim - 1))
    return pl.pallas_call(
        _mlstm_kernel,
        grid=(b // bb, t // L),
        in_specs=[tok(256), tok(256), tok(512), tok(512), tok(LANES),
                  pl.BlockSpec((bb, 2 * M_HEADS, L), lambda bi, ci: (bi, 0, ci)),
                  st(c0), st(n0), st(m0), pl.BlockSpec(g_mnorm.shape, lambda bi, ci: (0, 0))],
        out_specs=[tok(512), st(c0), st(n0), st(m0)],
        out_shape=[jax.ShapeDtypeStruct((b, t, M_WIDTH), f32), jax.ShapeDtypeStruct(c0.shape, f32),
                   jax.ShapeDtypeStruct(n0.shape, f32), jax.ShapeDtypeStruct(m0.shape, f32)],
        compiler_params=_cparams(("parallel", "arbitrary")),
        name="mlstm",
    )(mq, mk, mv, og, sm, grow, c0, n0, m0, g_mnorm)


def mlstm_branch(mq, mk, mv, og, sm, c0, n0, m0, g_mnorm):
    b, t = mq.shape[:2]
    tp = -(-t // M_CHUNK) * M_CHUNK
    if tp != t:
        pad = lambda a: jnp.pad(a, ((0, 0), (0, tp - t), (0, 0)))
        mq, mk, mv, og = pad(mq), pad(mk), pad(mv), pad(og)
        sm_pad = jnp.zeros((b, tp - t, LANES), f32).at[:, :, SM_IG:SM_IG + M_HEADS].set(NEG_BIG)
        sm = jnp.concatenate([sm, sm_pad], axis=1)
    grow = sm[:, :, SM_IG:SM_IG + 2 * M_HEADS].transpose(0, 2, 1)
    ob, c, n, m = mlstm(mq, mk, mv, og, sm, grow, c0.astype(f32), n0.astype(f32).reshape(b, M_HEADS, 1, M_QK),
                        m0.astype(f32).reshape(b, M_HEADS, 1, 1), g_mnorm.reshape(1, M_WIDTH))
    return ob[:, :t], c, n.reshape(b, M_HEADS, M_QK), m.reshape(b, M_HEADS)


def _select_threshold(get_chunk, n_chunks, chunk_w, rows, n_sel, dynamic):
    def count(pred):
        def body(c, acc):
            hit = jnp.where(pred(get_chunk(c)), 1.0, 0.0)
            for j in range(chunk_w // LANES):
                acc = acc + hit[:, j * LANES:(j + 1) * LANES]
            return acc
        acc0 = jnp.zeros((rows, LANES), f32)
        if dynamic:
            acc = lax.fori_loop(0, n_chunks, body, acc0)
        else:
            acc = acc0
            for c in range(n_chunks):
                acc = body(c, acc)
        return jnp.sum(acc, axis=1, keepdims=True)

    kf = float(n_sel)

    total = jnp.zeros((rows, 1), f32) + (n_chunks * chunk_w).astype(f32) if dynamic else \
        jnp.full((rows, 1), float(n_chunks * chunk_w), f32)
    c_nonneg = count(lambda keys: keys >= 0)
    c_pos = count(lambda keys: keys >= 1)
    take0 = c_nonneg >= kf
    zero_tie = take0 & (c_pos < kf)

    def unsettled(st):
        it, _, cnt = st
        return (it < 32) & (jnp.max(jnp.where(zero_tie, 0.0, jnp.abs(cnt - kf))) > 0.0)

    def bit_step(st):
        it, thr_u, cnt = st
        cand_u = thr_u | lax.shift_left(jnp.int32(1), 31 - it)
        cand_s = cand_u ^ INT_MIN
        c = count(lambda keys: keys >= cand_s)
        take = c >= kf
        return it + 1, jnp.where(take, cand_u, thr_u), jnp.where(take, c, cnt)

    state0 = (jnp.int32(1), jnp.where(take0, INT_MIN, 0).astype(i32), jnp.where(take0, c_nonneg, total))
    _, thr_u, _ = lax.while_loop(unsettled, lambda st: bit_step(bit_step(st)), bit_step(state0))
    thr = jnp.maximum(thr_u ^ INT_MIN, INT_MIN + 1)
    need = kf - count(lambda keys: keys > thr)
    return thr, need


def _tie_prefix_matrix():
    r = lax.broadcasted_iota(i32, (LANES, LANES), 0)
    c = lax.broadcasted_iota(i32, (LANES, LANES), 1)
    return jnp.where(r <= c, 1.0, 0.0).astype(bf16)


def _selected_blocks(key_blocks, thr, need, seen, tri):
    rows = key_blocks[0].shape[0]
    eqs = [kb == thr for kb in key_blocks]
    stack = jnp.concatenate([jnp.where(eq, 1.0, 0.0) for eq in eqs], axis=0).astype(bf16)
    pre = jnp.dot(stack, tri, preferred_element_type=f32)
    sels = []
    for j, (kb, eq) in enumerate(zip(key_blocks, eqs)):
        pj = pre[j * rows:(j + 1) * rows]
        sels.append((kb > thr) | (eq & ((seen + pj) <= need)))
        seen = seen + pj[:, LANES - 1:LANES]
    return sels, seen


def _dsa_prompt_kernel(q_ref, qi_ref, sm_ref, kit_ref, kt_ref, vd_ref, o_ref,
                       keys_ref, bias_ref, lg_ref, p_ref, m_ref, acc_ref, *, n_sel):
    qb = pl.program_id(1)
    tq = q_ref.shape[1]
    kc = kit_ref.shape[3]
    n_chunks = (qb * tq + tq - 1) // kc + 1
    lane = lax.broadcasted_iota(i32, (tq, LANES), 1)
    lo = lane < HEAD_DIM
    t_col = qb * tq + lax.broadcasted_iota(i32, (tq, 1), 0)
    sm = sm_ref[0]

    qi = qi_ref[0]
    qi_h = []
    for h in range(IDX_HEADS):
        blk = qi[:, (h // 2) * LANES:(h // 2 + 1) * LANES]
        qi_h.append(jnp.where(lo if h % 2 == 0 else ~lo, blk, 0.0).astype(bf16))
    w_h = [sm[:, SM_WI + h:SM_WI + h + 1] for h in range(IDX_HEADS)]

    def score_chunk(c, carry):
        kt = kit_ref[0, c]
        sc = jnp.zeros((tq, kc), f32)
        for h in range(IDX_HEADS):
            s = jnp.dot(qi_h[h], kt, preferred_element_type=f32)
            sc = sc + w_h[h] * jnp.maximum(s, 0.0)
        idx = c * kc + lax.broadcasted_iota(i32, (tq, kc), 1)
        keys_ref[c] = jnp.where(idx <= t_col, _sortable(sc), INT_MIN)
        return carry

    lax.fori_loop(0, n_chunks, score_chunk, 0)

    thr, need = _select_threshold(lambda c: keys_ref[c], n_chunks, kc, tq, n_sel, True)
    tri = _tie_prefix_matrix()

    q = q_ref[0] * (HEAD_DIM ** -0.5 * LOG2E)
    q_g = []
    for g in range(N_KV_HEADS):
        parts = []
        for j in range(GROUP):
            h = g * GROUP + j
            blk = q[:, (h // 2) * LANES:(h // 2 + 1) * LANES]
            parts.append(jnp.where(lo if h % 2 == 0 else ~lo, blk, 0.0).astype(bf16))
        q_g.append(jnp.concatenate(parts, axis=0))
    m_ref[...] = jnp.full(m_ref.shape, NEG_BIG, f32)
    acc_ref[...] = jnp.zeros(acc_ref.shape, f32)
    rs = min(ROW_SLICE, tq)

    def attend_chunk(c, seen):
        blocks = [keys_ref[c, :, j * LANES:(j + 1) * LANES] for j in range(kc // LANES)]
        sels, seen = _selected_blocks(blocks, thr, need, seen, tri)
        for j, sel in enumerate(sels):
            bias_ref[:, j * LANES:(j + 1) * LANES] = jnp.where(sel, 0.0, NEG_BIG)
        for g in range(N_KV_HEADS):
            lg_ref[...] = jnp.dot(q_g[g], kt_ref[0, g, c], preferred_element_type=f32)
            for r0 in range(0, GROUP * tq, rs):
                x = lg_ref[r0:r0 + rs] + bias_ref[r0 % tq:r0 % tq + rs]
                m_old = m_ref[g, r0:r0 + rs]
                m_new = jnp.maximum(m_old, jnp.max(x, axis=-1, keepdims=True))
                p_ref[r0:r0 + rs] = jnp.exp2(x - m_new).astype(bf16)
                acc_ref[g, r0:r0 + rs] = jnp.exp2(m_old - m_new) * acc_ref[g, r0:r0 + rs]
                m_ref[g, r0:r0 + rs] = m_new
            acc_ref[g] += jnp.dot(p_ref[...], vd_ref[0, g, c], preferred_element_type=f32)
        return seen

    lax.fori_loop(0, n_chunks, attend_chunk, jnp.zeros((tq, 1), f32))

    for g in range(N_KV_HEADS):
        acc = acc_ref[g]
        out = acc / acc[:, HEAD_DIM:HEAD_DIM + 1]
        for jp in range(GROUP // 2):
            even = out[(2 * jp) * tq:(2 * jp + 1) * tq]
            odd = pltpu.roll(out[(2 * jp + 1) * tq:(2 * jp + 2) * tq], HEAD_DIM, 1)
            o_ref[0, :, (g * 2 + jp) * LANES:(g * 2 + jp + 1) * LANES] = jnp.where(lo, even, odd)


def dsa_prompt(q, qi, sm, kit2, kt2, vd):
    b, t = q.shape[:2]
    nc, kc = kit2.shape[1], kit2.shape[3]
    tq = min(TQ, t)
    n_sel = min(TOPK_MAX, t // 4)
    tok = lambda w: pl.BlockSpec((1, tq, w), lambda bi, qb: (bi, qb, 0))
    return pl.pallas_call(
        functools.partial(_dsa_prompt_kernel, n_sel=n_sel),
        grid=(b, t // tq),
        in_specs=[tok(A_WIDTH), tok(IDX_HEADS * IDX_DIM), tok(LANES),
                  pl.BlockSpec((1, nc, LANES, kc), lambda bi, qb: (bi, 0, 0, 0)),
                  pl.BlockSpec((1, N_KV_HEADS, nc, LANES, kc), lambda bi, qb: (bi, 0, 0, 0, 0)),
                  pl.BlockSpec((1, N_KV_HEADS, nc, kc, LANES), lambda bi, qb: (bi, 0, 0, 0, 0))],
        out_specs=tok(A_WIDTH),
        out_shape=jax.ShapeDtypeStruct((b, t, A_WIDTH), f32),
        scratch_shapes=[pltpu.VMEM((nc, tq, kc), i32),
                        pltpu.VMEM((tq, kc), f32),
                        pltpu.VMEM((GROUP * tq, kc), f32),
                        pltpu.VMEM((GROUP * tq, kc), bf16),
                        pltpu.VMEM((N_KV_HEADS, GROUP * tq, 1), f32),
                        pltpu.VMEM((N_KV_HEADS, GROUP * tq, LANES), f32)],
        compiler_params=_cparams(("parallel", "arbitrary")),
        name="dsa_prompt",
    )(q, qi, sm, kit2, kt2, vd)


def dsa_prompt_branch(q, qi, sm, k, v):
    b, t = q.shape[:2]
    kc = min(KC, t)
    nc = t // kc
    kit = sm[:, :, :IDX_DIM].astype(bf16).reshape(b, nc, kc, IDX_DIM).transpose(0, 1, 3, 2)
    kit2 = jnp.concatenate([kit, kit], axis=2)
    kt = k.astype(bf16).reshape(b, nc, kc, N_KV_HEADS, HEAD_DIM).transpose(0, 3, 1, 4, 2)
    kt2 = jnp.concatenate([kt, kt], axis=3)
    vb = v.astype(bf16).reshape(b, nc, kc, N_KV_HEADS, HEAD_DIM).transpose(0, 3, 1, 2, 4)
    vd = jnp.concatenate([vb, jnp.ones_like(vb)], axis=4)
    return dsa_prompt(q, qi, sm, kit2, kt2, vd)


def _idx_scores(qi, w, ktpage):
    s = jnp.dot(qi, ktpage.astype(bf16), preferred_element_type=f32)
    r = (w * jnp.maximum(s, 0.0)).reshape(IDX_HEADS, SUBLANES, s.shape[1])
    sc = r[0]
    for h in range(1, IDX_HEADS):
        sc = sc + r[h]
    return sc


def _page_fetcher(pt_ref, srcs, bufs, sem, g):
    def start(step, slot):
        for j in range(g):
            pg = pt_ref[step * g + j]
            for i, (src, buf) in enumerate(zip(srcs, bufs)):
                pltpu.make_async_copy(src.at[pg], buf.at[slot, j], sem.at[i, slot]).start()

    def wait(slot):
        for i, buf in enumerate(bufs):
            pltpu.make_async_copy(buf.at[slot], buf.at[slot], sem.at[i, slot]).wait()

    return start, wait


def _run_paged(t, n_tot, active, start, wait, nbuf, body):
    @pl.when(active & (t == 0))
    def _():
        for d in range(nbuf - 1):
            pl.when(d < n_tot)(functools.partial(start, d, d))

    def one(slot):
        wait(slot)
        pl.when(t + nbuf - 1 < n_tot)(functools.partial(start, t + nbuf - 1, (slot + nbuf - 1) % nbuf))
        body(slot)

    for slot in range(nbuf):
        pl.when(active & (t % nbuf == slot))(functools.partial(one, slot))


def _ds_score_kernel(pt_ref, qi_ref, w_ref, kin_ref, cki_hbm, keys_ref, kbuf, sem, *, g):
    bi = pl.program_id(0)
    s = pl.program_id(1)
    last = pl.num_programs(1) - 1
    t = bi * last + s
    n_tot = pl.num_programs(0) * last
    qi = qi_ref[0]
    w = w_ref[0]
    start, wait = _page_fetcher(pt_ref, [cki_hbm], [kbuf], sem, g)

    def pages(slot):
        kcat = jnp.concatenate([kbuf[slot, j].astype(bf16) for j in range(g)], axis=1)
        keys_ref[0] = _sortable(_idx_scores(qi, w, kcat))

    _run_paged(t, n_tot, s < last, start, wait, kbuf.shape[0], pages)

    @pl.when(s == last)
    def _():
        keys_ref[0] = jnp.full(keys_ref.shape[1:], INT_MIN, i32)
        sc = _idx_scores(qi, w, kin_ref[0])
        tok = lax.broadcasted_iota(i32, sc.shape, 0)
        j = lax.broadcasted_iota(i32, sc.shape, 1)
        keys_ref[0, :, 0:PAGE_SIZE] = jnp.where(j <= tok, _sortable(sc), INT_MIN)


def _ds_thr_kernel(keys_ref, thr_ref, need_ref, *, n_sel, chunk_w):
    rows, width = keys_ref.shape

    def get_chunk(c):
        return keys_ref[:, c * chunk_w:(c + 1) * chunk_w]

    thr, need = _select_threshold(get_chunk, width // chunk_w, chunk_w, rows, n_sel, False)
    thr_ref[...] = jnp.broadcast_to(thr, thr_ref.shape)
    need_ref[...] = jnp.broadcast_to(need, need_ref.shape)


def _ds_attn_kernel(pt_ref, q_ref, keys_ref, thr_ref, need_ref, knew_ref, vnew_ref, ck_hbm, cv_hbm,
                    o_ref, m_ref, l_ref, acc_ref, seen_ref, kbuf, vbuf, sem, *, g):
    bi = pl.program_id(0)
    s = pl.program_id(1)
    last = pl.num_programs(1) - 1
    t = bi * last + s
    n_tot = pl.num_programs(0) * last
    start, wait = _page_fetcher(pt_ref, [ck_hbm, cv_hbm], [kbuf, vbuf], sem, g)
    q = q_ref[0]
    thr = thr_ref[0][:, 0:1]
    need = need_ref[0][:, 0:1]
    tri = _tie_prefix_matrix()

    def process(keys, kps, vps):
        n = len(kps)
        seen = seen_ref[...]
        sels, seen = _selected_blocks([keys[:, j * PAGE_SIZE:(j + 1) * PAGE_SIZE] for j in range(n)],
                                      thr, need, seen, tri)
        bias = [jnp.where(sel, 0.0, NEG_BIG) for sel in sels]
        seen_ref[...] = seen
        bias = jnp.concatenate(bias, axis=1) if n > 1 else bias[0]
        kcat = jnp.concatenate([kp.astype(bf16) for kp in kps], axis=1) if n > 1 else kps[0].astype(bf16)
        vcat = jnp.concatenate([vp.astype(bf16) for vp in vps], axis=1) if n > 1 else vps[0].astype(bf16)
        lg = jnp.dot(q, kcat, preferred_element_type=f32)
        lg = lg.reshape(N_HEADS, SUBLANES, n * PAGE_SIZE) + bias[None]
        m_old = m_ref[...]
        m_new = jnp.maximum(m_old, jnp.max(lg, axis=-1, keepdims=True))
        p = jnp.exp(lg - m_new)
        alpha = jnp.exp(m_old - m_new)
        l_ref[...] = alpha * l_ref[...] + jnp.sum(p, axis=-1, keepdims=True)
        pb = p.reshape(N_HEADS * SUBLANES, n * PAGE_SIZE).astype(bf16)
        pv = lax.dot_general(pb, vcat, (((1,), (1,)), ((), ())), preferred_element_type=f32)
        acc_ref[...] = alpha * acc_ref[...] + pv.reshape(N_HEADS, SUBLANES, LANES)
        m_ref[...] = m_new

    @pl.when(s == 0)
    def _():
        m_ref[...] = jnp.full(m_ref.shape, NEG_BIG, f32)
        l_ref[...] = jnp.zeros(l_ref.shape, f32)
        acc_ref[...] = jnp.zeros(acc_ref.shape, f32)
        seen_ref[...] = jnp.zeros(seen_ref.shape, f32)

    def pages(slot):
        process(keys_ref[0], [kbuf[slot, j] for j in range(g)], [vbuf[slot, j] for j in range(g)])

    _run_paged(t, n_tot, s < last, start, wait, kbuf.shape[0], pages)

    @pl.when(s == last)
    def _():
        process(keys_ref[0, :, 0:PAGE_SIZE], [knew_ref[0]], [vnew_ref[0]])
        o_ref[0] = (acc_ref[...] / l_ref[...]).reshape(N_HEADS * SUBLANES, LANES)


def dsa_sample(q, qi, sm, k, v, cache_k, cache_v, cache_idx_k, page_table):
    b, t = q.shape[:2]
    assert t <= SUBLANES
    n_pages = page_table.shape[1]
    g = min(PAGES_PER_STEP, n_pages)
    assert n_pages % g == 0
    ns = n_pages // g
    past = n_pages * PAGE_SIZE
    n_sel = min(TOPK_MAX, (past + t) // 4)
    pt = page_table.reshape(-1).astype(i32)
    padt = lambda a: jnp.pad(a, ((0, 0), (0, SUBLANES - t)) + ((0, 0),) * (a.ndim - 2))

    qi_r = padt(qi.reshape(b, t, IDX_HEADS, IDX_DIM)).transpose(0, 2, 1, 3).reshape(b, IDX_HEADS * SUBLANES, IDX_DIM)
    w_r = padt(sm[:, :, SM_WI:SM_WI + IDX_HEADS]).transpose(0, 2, 1).reshape(b, IDX_HEADS * SUBLANES, 1)
    padk = lambda a: jnp.pad(a, ((0, 0), (0, PAGE_SIZE - t), (0, 0))).transpose(0, 2, 1)
    ki_new = padk(sm[:, :, :IDX_DIM])
    k_new = padk(k)
    v_new = padk(v)
    qh = padt(q.reshape(b, t, N_HEADS, HEAD_DIM)).transpose(0, 2, 1, 3) * (HEAD_DIM ** -0.5)
    grp = (jnp.arange(N_HEADS) // GROUP)[None, :, None, None]
    q_r = jnp.concatenate([jnp.where(grp == 0, qh, 0.0), jnp.where(grp == 1, qh, 0.0)], axis=-1)
    q_r = q_r.reshape(b, N_HEADS * SUBLANES, LANES).astype(bf16)
    n_pool = cache_k.shape[0]
    ck = cache_k.transpose(0, 2, 3, 1).reshape(n_pool, N_KV_HEADS * HEAD_DIM, PAGE_SIZE)
    cv = cache_v.transpose(0, 2, 3, 1).reshape(n_pool, N_KV_HEADS * HEAD_DIM, PAGE_SIZE)
    cki = cache_idx_k.transpose(0, 2, 1)

    hbm = pl.BlockSpec(memory_space=pl.ANY)
    per_b = lambda a: pl.BlockSpec((1,) + a.shape[1:], lambda bi, s, ptr: (bi,) + (0,) * (a.ndim - 1))
    blk_w = g * PAGE_SIZE
    keys_spec = pl.BlockSpec((1, SUBLANES, blk_w), lambda bi, s, ptr: (bi, 0, s))
    width = past + blk_w

    keys = pl.pallas_call(
        functools.partial(_ds_score_kernel, g=g),
        grid_spec=pltpu.PrefetchScalarGridSpec(
            num_scalar_prefetch=1, grid=(b, ns + 1),
            in_specs=[per_b(qi_r), per_b(w_r), per_b(ki_new), hbm],
            out_specs=keys_spec,
            scratch_shapes=[pltpu.VMEM((PAGE_SLOTS, g, IDX_DIM, PAGE_SIZE), f32),
                            pltpu.SemaphoreType.DMA((1, PAGE_SLOTS))]),
        out_shape=jax.ShapeDtypeStruct((b, SUBLANES, width), i32),
        compiler_params=_cparams(("arbitrary", "arbitrary")),
        name="dsa_sample_scores",
    )(pt, qi_r.astype(bf16), w_r, ki_new, cki)

    rows = b * SUBLANES
    thr, need = pl.pallas_call(
        functools.partial(_ds_thr_kernel, n_sel=n_sel, chunk_w=blk_w),
        out_shape=[jax.ShapeDtypeStruct((rows, LANES), i32), jax.ShapeDtypeStruct((rows, LANES), f32)],
        compiler_params=pltpu.CompilerParams(vmem_limit_bytes=VMEM_LIMIT),
        name="dsa_sample_threshold",
    )(keys.reshape(rows, width))
    thr = thr.reshape(b, SUBLANES, LANES)
    need = need.reshape(b, SUBLANES, LANES)

    out = pl.pallas_call(
        functools.partial(_ds_attn_kernel, g=g),
        grid_spec=pltpu.PrefetchScalarGridSpec(
            num_scalar_prefetch=1, grid=(b, ns + 1),
            in_specs=[per_b(q_r), keys_spec, per_b(thr), per_b(need), per_b(k_new), per_b(v_new), hbm, hbm],
            out_specs=per_b(q_r),
            scratch_shapes=[pltpu.VMEM((N_HEADS, SUBLANES, 1), f32), pltpu.VMEM((N_HEADS, SUBLANES, 1), f32),
                            pltpu.VMEM((N_HEADS, SUBLANES, LANES), f32), pltpu.VMEM((SUBLANES, 1), f32),
                            pltpu.VMEM((PAGE_SLOTS, g, LANES, PAGE_SIZE), f32),
                            pltpu.VMEM((PAGE_SLOTS, g, LANES, PAGE_SIZE), f32),
                            pltpu.SemaphoreType.DMA((2, PAGE_SLOTS))]),
        out_shape=jax.ShapeDtypeStruct((b, N_HEADS * SUBLANES, LANES), f32),
        compiler_params=_cparams(("arbitrary", "arbitrary")),
        name="dsa_sample_attention",
    )(pt, q_r, keys, thr, need, k_new, v_new, ck, cv)
    out = out.reshape(b, N_HEADS, SUBLANES, N_KV_HEADS, HEAD_DIM)[:, :, :t]
    out = jnp.concatenate([out[:, :GROUP, :, 0], out[:, GROUP:, :, 1]], axis=1)
    return out.transpose(0, 2, 1, 3).reshape(b, t, A_WIDTH)


def _mid_kernel(x_ref, oa_ref, ob_ref, ga_ref, gb_ref, wa_ref, wb_ref, wo_ref, gf_ref, wrh_ref, wrl_ref, br_ref,
                x1_ref, h2_ref, te_ref, tg_ref):
    a = jnp.dot(oa_ref[...].astype(bf16), wa_ref[...], preferred_element_type=f32)
    b = jnp.dot(ob_ref[...].astype(bf16), wb_ref[...], preferred_element_type=f32)
    merged = ga_ref[...] * a + gb_ref[...] * b
    x1 = x_ref[...] + jnp.dot(merged.astype(bf16), wo_ref[...], preferred_element_type=f32)
    x1_ref[...] = x1
    h2 = _rms(x1, gf_ref[...])
    h2_ref[...] = h2
    hi = h2.astype(bf16)
    lo = (h2 - hi.astype(f32)).astype(bf16)
    lg = (jnp.dot(hi, wrh_ref[...], preferred_element_type=f32) + jnp.dot(lo, wrh_ref[...], preferred_element_type=f32)
          + jnp.dot(hi, wrl_ref[...], preferred_element_type=f32)) + br_ref[...]
    lane = lax.broadcasted_iota(i32, lg.shape, 1)
    lane_f = lane.astype(f32)
    vals, ids = [], []
    for _ in range(TOP_K):
        m = jnp.max(lg, axis=1, keepdims=True)
        idx = jnp.min(jnp.where(lg == m, lane_f, float(LANES)), axis=1, keepdims=True).astype(i32)
        vals.append(m)
        ids.append(idx)
        lg = jnp.where(lane == idx, -jnp.inf, lg)
    ex = [jnp.exp(v - vals[0]) for v in vals]
    tot = ex[0] + ex[1] + ex[2] + ex[3]
    te = jnp.zeros(lg.shape, i32)
    tg = jnp.zeros(lg.shape, f32)
    for j in range(TOP_K):
        te = jnp.where(lane == j, ids[j], te)
        tg = jnp.where(lane == j, ex[j] / tot, tg)
    te_ref[...] = te
    tg_ref[...] = tg


def mid(x, oa, ob, ga, gb, wa, wb, wo, g_ffn, wr_hi, wr_lo, br):
    n = x.shape[0]
    tm = min(256, n)
    assert n % tm == 0
    row = lambda w: pl.BlockSpec((tm, w), lambda i: (i, 0))
    full = lambda a: pl.BlockSpec(a.shape, lambda i: (0,) * a.ndim)
    return pl.pallas_call(
        _mid_kernel,
        grid=(n // tm,),
        in_specs=[row(D_MODEL), row(A_WIDTH), row(M_WIDTH), row(D_MODEL), row(D_MODEL),
                  full(wa), full(wb), full(wo), full(g_ffn), full(wr_hi), full(wr_lo), full(br)],
        out_specs=[row(D_MODEL), row(D_MODEL), row(LANES), row(LANES)],
        out_shape=[jax.ShapeDtypeStruct((n, D_MODEL), f32), jax.ShapeDtypeStruct((n, D_MODEL), f32),
                   jax.ShapeDtypeStruct((n, LANES), i32), jax.ShapeDtypeStruct((n, LANES), f32)],
        compiler_params=_cparams(("parallel",)),
        name="mid",
    )(x, oa, ob, ga, gb, wa, wb, wo, g_ffn, wr_hi, wr_lo, br)


def _rank_kernel(te_ref, dest_ref, meta_ref, cnt_ref, carry_ref, *, blk):
    ph = pl.program_id(0)
    i = pl.program_id(1)
    tm = te_ref.shape[0]
    lane = lax.broadcasted_iota(i32, (tm, LANES), 1)
    te = te_ref[...]
    oh = jnp.zeros((tm, LANES), f32)
    for j in range(TOP_K):
        oh = oh + jnp.where(lane == te[:, j:j + 1], 1.0, 0.0)
    tile_cnt = jnp.sum(oh, axis=0, keepdims=True)

    @pl.when((ph == 0) & (i == 0))
    def _():
        cnt_ref[...] = jnp.zeros(cnt_ref.shape, f32)

    @pl.when(ph == 0)
    def _():
        cnt_ref[...] += tile_cnt

    @pl.when((ph == 1) & (i == 0))
    def _():
        cnt = cnt_ref[...]
        padded = jnp.floor((cnt + (blk - 1)) / blk) * blk
        r = lax.broadcasted_iota(i32, (LANES, LANES), 0)
        c = lax.broadcasted_iota(i32, (LANES, LANES), 1)
        col = jnp.sum(jnp.where(r == c, padded, 0.0), axis=1, keepdims=True)
        start = jnp.sum(jnp.where(r < c, col, 0.0), axis=0, keepdims=True)
        carry_ref[...] = start
        pad_end = start + padded
        nbp = meta_ref.shape[0]
        jb = (lax.broadcasted_iota(i32, (nbp, LANES), 0) * blk).astype(f32)
        lane2 = lax.broadcasted_iota(i32, (nbp, LANES), 1)
        be = jnp.sum(jnp.where((pad_end <= jb) & (lane2 < N_EXPERTS), 1.0, 0.0), axis=1, keepdims=True)
        be = jnp.minimum(be, float(N_EXPERTS - 1))
        n_act = jnp.sum(jnp.where(lane2 == N_EXPERTS - 1, pad_end, 0.0), axis=1, keepdims=True) / blk
        meta_ref[...] = jnp.where(lane2 == 0, be, jnp.where(lane2 == 1, n_act, 0.0)).astype(i32)

    @pl.when(ph == 1)
    def _():
        r = lax.broadcasted_iota(i32, (tm, tm), 0)
        c = lax.broadcasted_iota(i32, (tm, tm), 1)
        before = jnp.where(c < r, 1.0, 0.0).astype(bf16)
        pos = carry_ref[...] + jnp.dot(before, oh.astype(bf16), preferred_element_type=f32)
        d = jnp.zeros((tm, LANES), f32)
        for j in range(TOP_K):
            dj = jnp.sum(jnp.where(lane == te[:, j:j + 1], pos, 0.0), axis=1, keepdims=True)
            d = jnp.where(lane == j, dj, d)
        dest_ref[...] = d.astype(i32)
        carry_ref[...] += tile_cnt


def rank(te, blk, n_blocks):
    n = te.shape[0]
    tm = next(c for c in (512, 384, 256, 128, 64, 32, 16, SUBLANES) if n % c == 0)
    nbp = -(-n_blocks // SUBLANES) * SUBLANES
    return pl.pallas_call(
        functools.partial(_rank_kernel, blk=blk),
        grid=(2, n // tm),
        in_specs=[pl.BlockSpec((tm, LANES), lambda ph, i: (i, 0))],
        out_specs=[pl.BlockSpec((tm, LANES), lambda ph, i: (i * ph, 0)),
                   pl.BlockSpec((nbp, LANES), lambda ph, i: (0, 0))],
        out_shape=[jax.ShapeDtypeStruct((n, LANES), i32), jax.ShapeDtypeStruct((nbp, LANES), i32)],
        scratch_shapes=[pltpu.VMEM((1, LANES), f32), pltpu.VMEM((1, LANES), f32)],
        compiler_params=_cparams(("arbitrary", "arbitrary")),
        name="rank",
    )(te)


def _ffn_kernel(be_ref, nact_ref, rt_ref, x_hbm, *rest):
    wgu_refs = rest[:W_SPLIT]
    bgu_ref = rest[W_SPLIT]
    wdn_refs = rest[W_SPLIT + 1:2 * W_SPLIT + 1]
    bdn_ref, y_ref, xbuf, sem, wgu_bf, wdn_bf = rest[2 * W_SPLIT + 1:]
    i = pl.program_id(0)
    n_act = nact_ref[0]
    blk = xbuf.shape[1]

    def row_copy(b, r, slot):
        tok = rt_ref[b * blk + r]
        return pltpu.make_async_copy(x_hbm.at[pl.ds(tok, 1)], xbuf.at[slot, pl.ds(r, 1)], sem.at[slot])

    def wait_slot(slot):
        pltpu.make_async_copy(xbuf.at[slot], xbuf.at[slot], sem.at[slot]).wait()

    @pl.when(i == 0)
    def _():
        def body(r, c):
            row_copy(0, r, 0).start()
            return c
        lax.fori_loop(0, blk, body, 0)

    @pl.when(i == n_act)
    def _():
        wait_slot(i % 2)

    @pl.when(i >= n_act)
    def _():
        y_ref[...] = jnp.zeros(y_ref.shape, f32)

    @pl.when((i < n_act) & ((i == 0) | (be_ref[i] != be_ref[jnp.maximum(i - 1, 0)])))
    def _():
        wg = wgu_bf.shape[1] // W_SPLIT
        wd = wdn_bf.shape[1] // W_SPLIT
        for c in range(W_SPLIT):
            wgu_bf[:, c * wg:(c + 1) * wg] = wgu_refs[c][0].astype(bf16)
            wdn_bf[:, c * wd:(c + 1) * wd] = wdn_refs[c][0].astype(bf16)

    def block(slot):
        wait_slot(slot)
        for r in range(blk):
            row_copy(i + 1, r, 1 - slot).start()
        x = xbuf[slot].astype(bf16)
        gu = jnp.dot(x, wgu_bf[...], preferred_element_type=f32) + bgu_ref[0]
        gate = jnp.minimum(gu[:, :EXPERT_FF], SWIGLU_LIMIT)
        up = jnp.clip(gu[:, EXPERT_FF:], -SWIGLU_LIMIT, SWIGLU_LIMIT)
        act = (up + 1.0) * (gate * _sigmoid(SWIGLU_ALPHA * gate))
        y_ref[...] = jnp.dot(act.astype(bf16), wdn_bf[...], preferred_element_type=f32) + bdn_ref[0]

    for slot in range(2):
        pl.when((i < n_act) & (i % 2 == slot))(functools.partial(block, slot))


def ffn(blk_e, n_act, row_tok, h2, w_gu, b_gu, w_dn, b_dn, blk, n_blocks):
    d = h2.shape[1]
    ff2 = w_gu.shape[2]
    return pl.pallas_call(
        _ffn_kernel,
        grid_spec=pltpu.PrefetchScalarGridSpec(
            num_scalar_prefetch=3, grid=(n_blocks,),
            in_specs=[pl.BlockSpec(memory_space=pl.ANY)]
            + [pl.BlockSpec((1, d, ff2 // W_SPLIT), functools.partial(lambda i, be, na, rt, c: (be[i], 0, c), c=c))
               for c in range(W_SPLIT)]
            + [pl.BlockSpec((1, 1, ff2), lambda i, be, na, rt: (be[i], 0, 0))]
            + [pl.BlockSpec((1, ff2 // 2, d // W_SPLIT), functools.partial(lambda i, be, na, rt, c: (be[i], 0, c), c=c))
               for c in range(W_SPLIT)]
            + [pl.BlockSpec((1, 1, d), lambda i, be, na, rt: (be[i], 0, 0))],
            out_specs=pl.BlockSpec((blk, d), lambda i, be, na, rt: (i, 0)),
            scratch_shapes=[pltpu.VMEM((2, blk, d), f32), pltpu.SemaphoreType.DMA((2,)),
                            pltpu.VMEM((d, ff2), bf16), pltpu.VMEM((ff2 // 2, d), bf16)]),
        out_shape=jax.ShapeDtypeStruct((n_blocks * blk, d), f32),
        compiler_params=_cparams(("arbitrary",)),
        name="ffn",
    )(blk_e, n_act, row_tok, h2, *([w_gu] * W_SPLIT), b_gu.reshape(b_gu.shape[0], 1, ff2),
      *([w_dn] * W_SPLIT), b_dn.reshape(b_dn.shape[0], 1, d))


def _out_kernel(dest_ref, x1_ref, tg_ref, p_ref, ys_hbm, gp_ref, wg_ref, wp_ref, gfin_ref, y_ref, buf, sem):
    i = pl.program_id(0)
    n = pl.num_programs(0)
    tm = x1_ref.shape[0]

    def row_copy(t, r, j, slot):
        d = dest_ref[(t * tm + r) * TOP_K + j]
        return pltpu.make_async_copy(ys_hbm.at[pl.ds(d, 1)], buf.at[slot, j, pl.ds(r, 1)], sem.at[slot])

    def wait_slot(slot):
        pltpu.make_async_copy(buf.at[slot], buf.at[slot], sem.at[slot]).wait()

    @pl.when(i == 0)
    def _():
        def body(r, c):
            for j in range(TOP_K):
                row_copy(0, r, j, 0).start()
            return c
        lax.fori_loop(0, tm, body, 0)

    def tile(slot):
        wait_slot(slot)
        nxt = jnp.minimum(i + 1, n - 1)
        for r in range(tm):
            for j in range(TOP_K):
                row_copy(nxt, r, j, 1 - slot).start()
        tg = tg_ref[...]
        x2 = x1_ref[...]
        for j in range(TOP_K):
            x2 = x2 + tg[:, j:j + 1] * buf[slot, j]
        hn = _rms(x2, gp_ref[...]).astype(bf16)
        gate = _sigmoid(jnp.dot(hn, wg_ref[...], preferred_element_type=f32))
        x3 = x2 + gate * jnp.dot(p_ref[...].astype(bf16), wp_ref[...], preferred_element_type=f32)
        y_ref[...] = _rms(x3, gfin_ref[...])

    for slot in range(2):
        pl.when(i % 2 == slot)(functools.partial(tile, slot))

    @pl.when(i == n - 1)
    def _():
        wait_slot(1 - i % 2)


def out_stage(dest_flat, x1, tg, p, ys, g_ple, wg, wp, g_final):
    n, d = x1.shape
    tm = min(OUT_TM, n)
    assert n % tm == 0
    row = lambda w: pl.BlockSpec((tm, w), lambda i, ds: (i, 0))
    full = lambda a: pl.BlockSpec(a.shape, lambda i, ds: (0,) * a.ndim)
    return pl.pallas_call(
        _out_kernel,
        grid_spec=pltpu.PrefetchScalarGridSpec(
            num_scalar_prefetch=1, grid=(n // tm,),
            in_specs=[row(d), row(LANES), row(p.shape[1]), pl.BlockSpec(memory_space=pl.ANY),
                      full(g_ple), full(wg), full(wp), full(g_final)],
            out_specs=row(d),
            scratch_shapes=[pltpu.VMEM((2, TOP_K, tm, d), f32), pltpu.SemaphoreType.DMA((2,))]),
        out_shape=jax.ShapeDtypeStruct((n, d), f32),
        compiler_params=_cparams(("arbitrary",)),
        name="out",
    )(dest_flat, x1, tg, p, ys, g_ple, wg, wp, g_final)


def mid_stage(x, oa, ob, ga, gb, w):
    return mid(x, oa, ob, ga, gb, w["wa"], w["wb"], w["wo"], w["g_ffn"], w["wr_hi"], w["wr_lo"], w["br"])


def moe_out(parts, w):
    h2 = jnp.concatenate([pt[1] for pt in parts], axis=0)
    te = jnp.concatenate([pt[2] for pt in parts], axis=0)
    n = h2.shape[0]
    blk = FFN_BLK
    n_blocks = -(-(n * TOP_K) // blk) + N_EXPERTS
    dest, meta = rank(te, blk, n_blocks)
    dest_flat = dest[:, :TOP_K].reshape(-1)
    blk_e = meta[:n_blocks, 0]
    n_act = meta[0:1, 1]
    tok = jnp.repeat(jnp.arange(n, dtype=i32), TOP_K)
    row_tok = jnp.zeros((n_blocks * blk,), i32).at[dest_flat].set(tok)
    ys = ffn(blk_e, n_act, row_tok, h2, w["w_gu"], w["b_gu"], w["w_dn"], w["b_dn"], blk, n_blocks)
    outs, off = [], 0
    for x1, _, _, tg, p in parts:
        ni = x1.shape[0]
        outs.append(out_stage(dest_flat[off * TOP_K:(off + ni) * TOP_K], x1, tg, p, ys,
                              w["g_ple"], w["wg"], w["wp"], w["g_final"]))
        off += ni
    return outs


def tail(x, oa, ob, ga, gb, p, w):
    return moe_out([mid_stage(x, oa, ob, ga, gb, w) + (p,)], w)[0]


def _prep_weights(g_mix, w_in, b_gate, g_mnorm, w_up_a, w_up_b, w_out, g_ffn, w_router, b_router,
                  w_gu, b_gu, w_dn, b_dn, g_ple, w_ple_gate, w_ple_proj, g_final):
    wr = jnp.pad(w_router.astype(f32), ((0, 0), (0, LANES - N_EXPERTS)))
    wr_hi = wr.astype(bf16)
    wr_lo = (wr - wr_hi.astype(f32)).astype(bf16)
    br = jnp.full((1, LANES), -jnp.inf, f32).at[0, :N_EXPERTS].set(b_router.astype(f32))
    bias = jnp.zeros((1, LANES), f32).at[0, SM_IG:SM_IG + 2 * M_HEADS].set(b_gate.astype(f32))
    return dict(g_mix=g_mix.reshape(1, -1), w_in=_pack_w_in(w_in), bias=bias, g_mnorm=g_mnorm,
                wa=w_up_a.astype(bf16), wb=w_up_b.astype(bf16), wo=w_out.astype(bf16), g_ffn=g_ffn.reshape(1, -1),
                wr_hi=wr_hi, wr_lo=wr_lo, br=br, w_gu=w_gu, b_gu=b_gu, w_dn=w_dn, b_dn=b_dn,
                g_ple=g_ple.reshape(1, -1), wg=w_ple_gate.astype(bf16), wp=w_ple_proj.astype(bf16),
                g_final=g_final.reshape(1, -1))


def _layer(x, p, pos, attn_fn, c0, n0, m0, w):
    b, t = x.shape[:2]
    n = b * t
    cos, sin = _rope_tables(pos)
    cos = jnp.tile(cos, (b, 1))
    sin = jnp.tile(sin, (b, 1))
    x2d = x.reshape(n, D_MODEL)
    q, k, v, qi, sm, mq, mk, mv, og, ga, gb = in_proj(x2d, cos, sin, w["g_mix"], w["w_in"], w["bias"])
    r3 = lambda a: a.reshape(b, t, a.shape[-1])
    o_a = attn_fn(r3(q), r3(qi), r3(sm), r3(k), r3(v))
    o_b, c, nn, m = mlstm_branch(r3(mq), r3(mk), r3(mv), r3(og), r3(sm), c0, n0, m0, w["g_mnorm"])
    part = mid_stage(x2d, o_a.reshape(n, A_WIDTH), o_b.reshape(n, M_WIDTH), ga, gb, w) + (p.reshape(n, PLE_DIM),)
    state = (k.reshape(b, t, N_KV_HEADS, HEAD_DIM), v.reshape(b, t, N_KV_HEADS, HEAD_DIM),
             sm[:, :IDX_DIM].reshape(b, t, IDX_DIM), c, nn, m)
    return part, state


def kernel(x_prompt, x_sample, cache_k, cache_v, cache_idx_k, state_C, state_n, state_m, page_table,
           p_prompt, p_sample, g_mix, w_in, b_gate, g_mnorm, w_up_a, w_up_b, w_out, g_ffn,
           w_router, b_router, w_gu, b_gu, w_dn, b_dn, g_ple, w_ple_gate, w_ple_proj, g_final):
    assert x_prompt.shape[-1] == D_MODEL and w_in.shape[0] == 1, "single-layer model of width D_MODEL"
    bp, tp = x_prompt.shape[:2]
    ts = x_sample.shape[1]
    past = page_table.shape[1] * PAGE_SIZE
    w = _prep_weights(g_mix[0], w_in[0], b_gate[0], g_mnorm[0], w_up_a[0], w_up_b[0], w_out[0], g_ffn[0],
                      w_router[0], b_router[0], w_gu[0], b_gu[0], w_dn[0], b_dn[0], g_ple[0],
                      w_ple_gate[0], w_ple_proj[0], g_final)
    zeros = lambda *s: jnp.zeros(s, f32)
    part_p, sp = _layer(x_prompt, p_prompt[0], jnp.arange(tp, dtype=i32), dsa_prompt_branch,
                        zeros(bp, M_HEADS, M_V, M_QK), zeros(bp, M_HEADS, M_QK), zeros(bp, M_HEADS), w)
    attn_s = functools.partial(dsa_sample, cache_k=cache_k[0], cache_v=cache_v[0], cache_idx_k=cache_idx_k[0],
                               page_table=page_table)
    part_s, ss = _layer(x_sample, p_sample[0], past + jnp.arange(ts, dtype=i32), attn_s,
                        state_C[0], state_n[0], state_m[0], w)
    yp, ys = moe_out([part_p, part_s], w)
    return ((yp.reshape(x_prompt.shape), ys.reshape(x_sample.shape))
            + tuple(s[None] for s in sp) + tuple(s[None] for s in ss))
```

```python
import functools

import jax
import jax.numpy as jnp
from jax import lax
from jax.experimental import pallas as pl
from jax.experimental.pallas import tpu as pltpu

f32 = jnp.float32
bf16 = jnp.bfloat16
i32 = jnp.int32

D_MODEL = 1024
PAGE_SIZE = 128
N_HEADS = 8
N_KV_HEADS = 2
HEAD_DIM = 64
GROUP = N_HEADS // N_KV_HEADS
IDX_HEADS = 4
IDX_DIM = 64
TOPK_MAX = 256
ROPE_THETA = 10000.0
M_HEADS = 4
M_QK = 64
M_V = 128
N_EXPERTS = 32
TOP_K = 4
EXPERT_FF = D_MODEL
SWIGLU_LIMIT = 7.0
SWIGLU_ALPHA = 1.702
PLE_DIM = 256
EPS = 1e-6
A_WIDTH = N_HEADS * HEAD_DIM
M_WIDTH = M_HEADS * M_V

LANES = 128
SUBLANES = 8
VMEM_LIMIT = 56 * 1024 * 1024

M_CHUNK = 128
MLSTM_BATCH = 1
TQ = 256
KC = 1024
PAGE_SLOTS = 4
PAGES_PER_STEP = 16
FFN_BLK = 256
W_SPLIT = 4
OUT_TM = 128

INT_MIN = -2147483648
INT_MAX = 2147483647
NEG_BIG = -1e30
LOG2E = 1.4426950408889634
ROW_SLICE = 16

SM_KI = 0
SM_WI = IDX_DIM
SM_IG = SM_WI + IDX_HEADS
SM_LF = SM_IG + M_HEADS


def _cparams(sem):
    return pltpu.CompilerParams(dimension_semantics=sem, vmem_limit_bytes=VMEM_LIMIT)


def _rms(x, g):
    return x * lax.rsqrt(jnp.mean(x * x, axis=-1, keepdims=True) + EPS) * g


def _sigmoid(x):
    return 1.0 / (1.0 + jnp.exp(-x))


def _log_sigmoid(x):
    return jnp.minimum(x, 0.0) - jnp.log1p(jnp.exp(-jnp.abs(x)))


def _sortable(x):
    bits = lax.bitcast_convert_type(x, i32)
    key = bits ^ ((bits >> 31) & INT_MAX)
    return jnp.where(x == 0.0, 0, key)


_G_Q = (0, 512)
_G_K = (512, 640)
_G_V = (640, 768)
_G_QI = (768, 1024)
_G_SM = (1024, 1152)
_G_MQ = (1152, 1408)
_G_MK = (1408, 1664)
_G_MV = (1664, 2176)
_G_MO = (2176, 2688)
_G_GA = (2688, 3712)
_G_GB = (3712, 4736)
_W_COLS = 4736


def _pack_w_in(w_in):
    o = [0]
    for s in (A_WIDTH, 128, 128, 256, 64, 4, 256, 256, 512, 4, 4, 512, 1024, 1024):
        o.append(o[-1] + s)
    aq, ak, av, iq, ik, iw, mq, mk, mv, mi, mf, mo, ga, gb = [w_in[:, o[i]:o[i + 1]] for i in range(14)]
    pad = jnp.zeros((w_in.shape[0], LANES - (IDX_DIM + IDX_HEADS + 2 * M_HEADS)), w_in.dtype)
    small = jnp.concatenate([ik, iw, mi, mf, pad], axis=1)
    w = jnp.concatenate([aq, ak, av, iq, small, mq, mk, mv, mo, ga, gb], axis=1)
    return w.astype(bf16)


def _in_kernel(x_ref, cos_ref, sin_ref, g_ref, w_ref, bias_ref,
               q_ref, k_ref, v_ref, qi_ref, sm_ref, mq_ref, mk_ref, mv_ref, og_ref, ga_ref, gb_ref):
    x = x_ref[...]
    hb = _rms(x, g_ref[...]).astype(bf16)
    cos = cos_ref[...]
    sin = sin_ref[...]
    tm = x.shape[0]
    lane = lax.broadcasted_iota(i32, (tm, LANES), 1)
    first_half = (lane % HEAD_DIM) < (HEAD_DIM // 2)

    def rope(z):
        rot = jnp.where(first_half, pltpu.roll(z, LANES - HEAD_DIM // 2, 1), pltpu.roll(z, HEAD_DIM // 2, 1))
        return z * cos + rot * sin

    def proj(grp):
        return jnp.dot(hb, w_ref[:, grp[0]:grp[1]], preferred_element_type=f32)

    z = proj(_G_Q)
    for j in range(4):
        q_ref[:, j * LANES:(j + 1) * LANES] = rope(z[:, j * LANES:(j + 1) * LANES])
    k_ref[...] = rope(proj(_G_K))
    v_ref[...] = proj(_G_V)
    z = proj(_G_QI)
    for j in range(2):
        qi_ref[:, j * LANES:(j + 1) * LANES] = rope(z[:, j * LANES:(j + 1) * LANES])
    z = proj(_G_SM)
    zb = z + bias_ref[...]
    sm = jnp.where(lane < SM_WI, rope(z),
                   jnp.where(lane < SM_IG, z * (IDX_HEADS ** -0.5 * IDX_DIM ** -0.5),
                             jnp.where(lane < SM_LF, zb,
                                       jnp.where(lane < SM_LF + M_HEADS, _log_sigmoid(zb), 0.0))))
    sm_ref[...] = sm
    mq_ref[...] = proj(_G_MQ)
    mk_ref[...] = proj(_G_MK) * (M_QK ** -0.5)
    mv_ref[...] = proj(_G_MV)
    og_ref[...] = _sigmoid(proj(_G_MO))
    ga_ref[...] = _sigmoid(proj(_G_GA))
    gb_ref[...] = _sigmoid(proj(_G_GB))


def _rope_tables(pos):
    half = HEAD_DIM // 2
    inv = ROPE_THETA ** (-jnp.arange(half, dtype=f32) / half)
    ang = pos.astype(f32)[:, None] * inv[None, :]
    cos = jnp.cos(ang)
    sin = jnp.sin(ang)
    cos128 = jnp.tile(cos, (1, 4))
    sin128 = jnp.tile(jnp.concatenate([-sin, sin], axis=1), (1, 2))
    return cos128, sin128


def in_proj(x2d, cos128, sin128, g_mix, w_packed, bias128):
    n = x2d.shape[0]
    tm = min(256, n)
    assert n % tm == 0
    row = lambda w: pl.BlockSpec((tm, w), lambda i: (i, 0))
    full = lambda a: pl.BlockSpec(a.shape, lambda i: (0,) * a.ndim)
    widths = (512, 128, 128, 256, 128, 256, 256, 512, 512, 1024, 1024)
    return pl.pallas_call(
        _in_kernel,
        grid=(n // tm,),
        in_specs=[row(D_MODEL), row(LANES), row(LANES), full(g_mix), full(w_packed), full(bias128)],
        out_specs=[row(w) for w in widths],
        out_shape=[jax.ShapeDtypeStruct((n, w), f32) for w in widths],
        compiler_params=_cparams(("parallel",)),
        name="in_proj",
    )(x2d, cos128, sin128, g_mix, w_packed, bias128)


def _mlstm_kernel(mq_ref, mk_ref, mv_ref, og_ref, sm_ref, gr_ref, c0_ref, n0_ref, m0_ref, gn_ref,
                  ob_ref, c_ref, n_ref, m_ref):
    ci = pl.program_id(1)
    L = mq_ref.shape[1]

    @pl.when(ci == 0)
    def _():
        c_ref[...] = c0_ref[...]
        n_ref[...] = n0_ref[...]
        m_ref[...] = m0_ref[...]

    row = lax.broadcasted_iota(i32, (L, L), 0)
    col = lax.broadcasted_iota(i32, (L, L), 1)
    tril = row >= col
    for bi, hd in [(bi, hd) for bi in range(mq_ref.shape[0]) for hd in range(M_HEADS)]:
        sm = sm_ref[bi]
        gr = gr_ref[bi]
        q = mq_ref[bi, :, hd * M_QK:(hd + 1) * M_QK]
        k = mk_ref[bi, :, hd * M_QK:(hd + 1) * M_QK]
        v = mv_ref[bi, :, hd * M_V:(hd + 1) * M_V]
        ig_r = gr[hd:hd + 1, :]
        lf_r = gr[M_HEADS + hd:M_HEADS + hd + 1, :]
        ig_c = sm[:, SM_IG + hd:SM_IG + hd + 1]
        lf_c = sm[:, SM_LF + hd:SM_LF + hd + 1]
        C = c_ref[bi, hd]
        nrow = n_ref[bi, hd]
        m_prev = m_ref[bi, hd]
        b_c = jnp.sum(jnp.where(tril, lf_r, 0.0), axis=1, keepdims=True)
        b_r = jnp.sum(jnp.where(tril, 0.0, lf_c) + jnp.where(row == col, lf_c, 0.0), axis=0, keepdims=True)
        dmat = jnp.where(tril, b_c - b_r + ig_r, -jnp.inf)
        inter = b_c + m_prev
        m_t = jnp.maximum(inter, jnp.max(dmat, axis=1, keepdims=True))
        qb = q.astype(bf16)
        kb = k.astype(bf16)
        qk = lax.dot_general(qb, kb, (((1,), (1,)), ((), ())), preferred_element_type=f32)
        s = qk * jnp.exp(dmat - m_t)
        w_inter = jnp.exp(inter - m_t)
        qc = lax.dot_general(qb, C.astype(bf16), (((1,), (1,)), ((), ())), preferred_element_type=f32)
        num = jnp.dot(s.astype(bf16), v.astype(bf16), preferred_element_type=f32) + w_inter * qc
        den = jnp.sum(s, axis=1, keepdims=True) + w_inter * jnp.sum(q * nrow, axis=1, keepdims=True)
        h = num / jnp.maximum(jnp.abs(den), jnp.exp(-m_t))
        b_last = b_c[L - 1:L, :]
        g_c = b_last - b_c + ig_c
        m_new = jnp.maximum(b_last + m_prev, jnp.max(g_c, axis=0, keepdims=True))
        w_k = jnp.exp(g_c - m_new)
        decay = jnp.exp(b_last + m_prev - m_new)
        wv = (w_k * v).astype(bf16)
        c_ref[bi, hd] = decay * C + lax.dot_general(wv, kb, (((0,), (0,)), ((), ())), preferred_element_type=f32)
        n_ref[bi, hd] = decay * nrow + jnp.sum(w_k * k, axis=0, keepdims=True)
        m_ref[bi, hd] = m_new
        gn = gn_ref[:, hd * M_V:(hd + 1) * M_V]
        ob_ref[bi, :, hd * M_V:(hd + 1) * M_V] = og_ref[bi, :, hd * M_V:(hd + 1) * M_V] * _rms(h, gn)


def mlstm(mq, mk, mv, og, sm, grow, c0, n0, m0, g_mnorm):
    b, t = mq.shape[:2]
    L = M_CHUNK
    bb = MLSTM_BATCH if b % MLSTM_BATCH == 0 else 1
    assert t % L == 0
    tok = lambda w: pl.BlockSpec((bb, L, w), lambda bi, ci: (bi, ci, 0))
    st = lambda a: pl.BlockSpec((bb,) + a.shape[1:], lambda bi, ci: (bi,) + (0,) * (a.ndim - 1))
    return pl.pallas_call(
        _mlstm_kernel,
        grid=(b // bb, t // L),
        in_specs=[tok(256), tok(256), tok(512), tok(512), tok(LANES),
                  pl.BlockSpec((bb, 2 * M_HEADS, L), lambda bi, ci: (bi, 0, ci)),
                  st(c0), st(n0), st(m0), pl.BlockSpec(g_mnorm.shape, lambda bi, ci: (0, 0))],
        out_specs=[tok(512), st(c0), st(n0), st(m0)],
        out_shape=[jax.ShapeDtypeStruct((b, t, M_WIDTH), f32), jax.ShapeDtypeStruct(c0.shape, f32),
                   jax.ShapeDtypeStruct(n0.shape, f32), jax.ShapeDtypeStruct(m0.shape, f32)],
        compiler_params=_cparams(("parallel", "arbitrary")),
        name="mlstm",
    )(mq, mk, mv, og, sm, grow, c0, n0, m0, g_mnorm)


def mlstm_branch(mq, mk, mv, og, sm, c0, n0, m0, g_mnorm):
    b, t = mq.shape[:2]
    tp = -(-t // M_CHUNK) * M_CHUNK
    if tp != t:
        pad = lambda a: jnp.pad(a, ((0, 0), (0, tp - t), (0, 0)))
        mq, mk, mv, og = pad(mq), pad(mk), pad(mv), pad(og)
        sm_pad = jnp.zeros((b, tp - t, LANES), f32).at[:, :, SM_IG:SM_IG + M_HEADS].set(NEG_BIG)
        sm = jnp.concatenate([sm, sm_pad], axis=1)
    grow = sm[:, :, SM_IG:SM_IG + 2 * M_HEADS].transpose(0, 2, 1)
    ob, c, n, m = mlstm(mq, mk, mv, og, sm, grow, c0.astype(f32), n0.astype(f32).reshape(b, M_HEADS, 1, M_QK),
                        m0.astype(f32).reshape(b, M_HEADS, 1, 1), g_mnorm.reshape(1, M_WIDTH))
    return ob[:, :t], c, n.reshape(b, M_HEADS, M_QK), m.reshape(b, M_HEADS)


def _select_threshold(get_chunk, n_chunks, chunk_w, rows, n_sel, dynamic):
    def count(pred):
        def body(c, acc):
            hit = jnp.where(pred(get_chunk(c)), 1.0, 0.0)
            for j in range(chunk_w // LANES):
                acc = acc + hit[:, j * LANES:(j + 1) * LANES]
            return acc
        acc0 = jnp.zeros((rows, LANES), f32)
        if dynamic:
            acc = lax.fori_loop(0, n_chunks, body, acc0)
        else:
            acc = acc0
            for c in range(n_chunks):
                acc = body(c, acc)
        return jnp.sum(acc, axis=1, keepdims=True)

    kf = float(n_sel)

    total = jnp.zeros((rows, 1), f32) + (n_chunks * chunk_w).astype(f32) if dynamic else \
        jnp.full((rows, 1), float(n_chunks * chunk_w), f32)
    c_nonneg = count(lambda keys: keys >= 0)
    c_pos = count(lambda keys: keys >= 1)
    take0 = c_nonneg >= kf
    zero_tie = take0 & (c_pos < kf)

    def unsettled(st):
        it, _, cnt = st
        return (it < 32) & (jnp.max(jnp.where(zero_tie, 0.0, jnp.abs(cnt - kf))) > 0.0)

    def bit_step(st):
        it, thr_u, cnt = st
        cand_u = thr_u | lax.shift_left(jnp.int32(1), 31 - it)
        cand_s = cand_u ^ INT_MIN
        c = count(lambda keys: keys >= cand_s)
        take = c >= kf
        return it + 1, jnp.where(take, cand_u, thr_u), jnp.where(take, c, cnt)

    state0 = (jnp.int32(1), jnp.where(take0, INT_MIN, 0).astype(i32), jnp.where(take0, c_nonneg, total))
    _, thr_u, _ = lax.while_loop(unsettled, lambda st: bit_step(bit_step(st)), bit_step(state0))
    thr = jnp.maximum(thr_u ^ INT_MIN, INT_MIN + 1)
    need = kf - count(lambda keys: keys > thr)
    return thr, need


def _tie_prefix_matrix():
    r = lax.broadcasted_iota(i32, (LANES, LANES), 0)
    c = lax.broadcasted_iota(i32, (LANES, LANES), 1)
    return jnp.where(r <= c, 1.0, 0.0).astype(bf16)


def _selected_blocks(key_blocks, thr, need, seen, tri):
    rows = key_blocks[0].shape[0]
    eqs = [kb == thr for kb in key_blocks]
    stack = jnp.concatenate([jnp.where(eq, 1.0, 0.0) for eq in eqs], axis=0).astype(bf16)
    pre = jnp.dot(stack, tri, preferred_element_type=f32)
    sels = []
    for j, (kb, eq) in enumerate(zip(key_blocks, eqs)):
        pj = pre[j * rows:(j + 1) * rows]
        sels.append((kb > thr) | (eq & ((seen + pj) <= need)))
        seen = seen + pj[:, LANES - 1:LANES]
    return sels, seen


def _dsa_prompt_kernel(q_ref, qi_ref, sm_ref, kit_ref, kt_ref, vd_ref, o_ref,
                       keys_ref, bias_ref, lg_ref, p_ref, m_ref, acc_ref, *, n_sel):
    qb = pl.program_id(1)
    tq = q_ref.shape[1]
    kc = kit_ref.shape[3]
    n_chunks = (qb * tq + tq - 1) // kc + 1
    lane = lax.broadcasted_iota(i32, (tq, LANES), 1)
    lo = lane < HEAD_DIM
    t_col = qb * tq + lax.broadcasted_iota(i32, (tq, 1), 0)
    sm = sm_ref[0]

    qi = qi_ref[0]
    qi_h = []
    for h in range(IDX_HEADS):
        blk = qi[:, (h // 2) * LANES:(h // 2 + 1) * LANES]
        qi_h.append(jnp.where(lo if h % 2 == 0 else ~lo, blk, 0.0).astype(bf16))
    w_h = [sm[:, SM_WI + h:SM_WI + h + 1] for h in range(IDX_HEADS)]

    def score_chunk(c, carry):
        kt = kit_ref[0, c]
        sc = jnp.zeros((tq, kc), f32)
        for h in range(IDX_HEADS):
            s = jnp.dot(qi_h[h], kt, preferred_element_type=f32)
            sc = sc + w_h[h] * jnp.maximum(s, 0.0)
        idx = c * kc + lax.broadcasted_iota(i32, (tq, kc), 1)
        keys_ref[c] = jnp.where(idx <= t_col, _sortable(sc), INT_MIN)
        return carry

    lax.fori_loop(0, n_chunks, score_chunk, 0)

    thr, need = _select_threshold(lambda c: keys_ref[c], n_chunks, kc, tq, n_sel, True)
    tri = _tie_prefix_matrix()

    q = q_ref[0] * (HEAD_DIM ** -0.5 * LOG2E)
    q_g = []
    for g in range(N_KV_HEADS):
        parts = []
        for j in range(GROUP):
            h = g * GROUP + j
            blk = q[:, (h // 2) * LANES:(h // 2 + 1) * LANES]
            parts.append(jnp.where(lo if h % 2 == 0 else ~lo, blk, 0.0).astype(bf16))
        q_g.append(jnp.concatenate(parts, axis=0))
    m_ref[...] = jnp.full(m_ref.shape, NEG_BIG, f32)
    acc_ref[...] = jnp.zeros(acc_ref.shape, f32)
    rs = min(ROW_SLICE, tq)

    def attend_chunk(c, seen):
        blocks = [keys_ref[c, :, j * LANES:(j + 1) * LANES] for j in range(kc // LANES)]
        sels, seen = _selected_blocks(blocks, thr, need, seen, tri)
        for j, sel in enumerate(sels):
            bias_ref[:, j * LANES:(j + 1) * LANES] = jnp.where(sel, 0.0, NEG_BIG)
        for g in range(N_KV_HEADS):
            lg_ref[...] = jnp.dot(q_g[g], kt_ref[0, g, c], preferred_element_type=f32)
            for r0 in range(0, GROUP * tq, rs):
                x = lg_ref[r0:r0 + rs] + bias_ref[r0 % tq:r0 % tq + rs]
                m_old = m_ref[g, r0:r0 + rs]
                m_new = jnp.maximum(m_old, jnp.max(x, axis=-1, keepdims=True))
                p_ref[r0:r0 + rs] = jnp.exp2(x - m_new).astype(bf16)
                acc_ref[g, r0:r0 + rs] = jnp.exp2(m_old - m_new) * acc_ref[g, r0:r0 + rs]
                m_ref[g, r0:r0 + rs] = m_new
            acc_ref[g] += jnp.dot(p_ref[...], vd_ref[0, g, c], preferred_element_type=f32)
        return seen

    lax.fori_loop(0, n_chunks, attend_chunk, jnp.zeros((tq, 1), f32))

    for g in range(N_KV_HEADS):
        acc = acc_ref[g]
        out = acc / acc[:, HEAD_DIM:HEAD_DIM + 1]
        for jp in range(GROUP // 2):
            even = out[(2 * jp) * tq:(2 * jp + 1) * tq]
            odd = pltpu.roll(out[(2 * jp + 1) * tq:(2 * jp + 2) * tq], HEAD_DIM, 1)
            o_ref[0, :, (g * 2 + jp) * LANES:(g * 2 + jp + 1) * LANES] = jnp.where(lo, even, odd)


def dsa_prompt(q, qi, sm, kit2, kt2, vd):
    b, t = q.shape[:2]
    nc, kc = kit2.shape[1], kit2.shape[3]
    tq = min(TQ, t)
    n_sel = min(TOPK_MAX, t // 4)
    tok = lambda w: pl.BlockSpec((1, tq, w), lambda bi, qb: (bi, qb, 0))
    return pl.pallas_call(
        functools.partial(_dsa_prompt_kernel, n_sel=n_sel),
        grid=(b, t // tq),
        in_specs=[tok(A_WIDTH), tok(IDX_HEADS * IDX_DIM), tok(LANES),
                  pl.BlockSpec((1, nc, LANES, kc), lambda bi, qb: (bi, 0, 0, 0)),
                  pl.BlockSpec((1, N_KV_HEADS, nc, LANES, kc), lambda bi, qb: (bi, 0, 0, 0, 0)),
                  pl.BlockSpec((1, N_KV_HEADS, nc, kc, LANES), lambda bi, qb: (bi, 0, 0, 0, 0))],
        out_specs=tok(A_WIDTH),
        out_shape=jax.ShapeDtypeStruct((b, t, A_WIDTH), f32),
        scratch_shapes=[pltpu.VMEM((nc, tq, kc), i32),
                        pltpu.VMEM((tq, kc), f32),
                        pltpu.VMEM((GROUP * tq, kc), f32),
                        pltpu.VMEM((GROUP * tq, kc), bf16),
                        pltpu.VMEM((N_KV_HEADS, GROUP * tq, 1), f32),
                        pltpu.VMEM((N_KV_HEADS, GROUP * tq, LANES), f32)],
        compiler_params=_cparams(("parallel", "arbitrary")),
        name="dsa_prompt",
    )(q, qi, sm, kit2, kt2, vd)


def dsa_prompt_branch(q, qi, sm, k, v):
    b, t = q.shape[:2]
    kc = min(KC, t)
    nc = t // kc
    kit = sm[:, :, :IDX_DIM].astype(bf16).reshape(b, nc, kc, IDX_DIM).transpose(0, 1, 3, 2)
    kit2 = jnp.concatenate([kit, kit], axis=2)
    kt = k.astype(bf16).reshape(b, nc, kc, N_KV_HEADS, HEAD_DIM).transpose(0, 3, 1, 4, 2)
    kt2 = jnp.concatenate([kt, kt], axis=3)
    vb = v.astype(bf16).reshape(b, nc, kc, N_KV_HEADS, HEAD_DIM).transpose(0, 3, 1, 2, 4)
    vd = jnp.concatenate([vb, jnp.ones_like(vb)], axis=4)
    return dsa_prompt(q, qi, sm, kit2, kt2, vd)


def _idx_scores(qi, w, ktpage):
    s = jnp.dot(qi, ktpage.astype(bf16), preferred_element_type=f32)
    r = (w * jnp.maximum(s, 0.0)).reshape(IDX_HEADS, SUBLANES, s.shape[1])
    sc = r[0]
    for h in range(1, IDX_HEADS):
        sc = sc + r[h]
    return sc


def _page_fetcher(pt_ref, srcs, bufs, sem, g):
    def start(step, slot):
        for j in range(g):
            pg = pt_ref[step * g + j]
            for i, (src, buf) in enumerate(zip(srcs, bufs)):
                pltpu.make_async_copy(src.at[pg], buf.at[slot, j], sem.at[i, slot]).start()

    def wait(slot):
        for i, buf in enumerate(bufs):
            pltpu.make_async_copy(buf.at[slot], buf.at[slot], sem.at[i, slot]).wait()

    return start, wait


def _run_paged(t, n_tot, active, start, wait, nbuf, body):
    @pl.when(active & (t == 0))
    def _():
        for d in range(nbuf - 1):
            pl.when(d < n_tot)(functools.partial(start, d, d))

    def one(slot):
        wait(slot)
        pl.when(t + nbuf - 1 < n_tot)(functools.partial(start, t + nbuf - 1, (slot + nbuf - 1) % nbuf))
        body(slot)

    for slot in range(nbuf):
        pl.when(active & (t % nbuf == slot))(functools.partial(one, slot))


def _ds_score_kernel(pt_ref, qi_ref, w_ref, kin_ref, cki_hbm, keys_ref, kbuf, sem, *, g):
    bi = pl.program_id(0)
    s = pl.program_id(1)
    last = pl.num_programs(1) - 1
    t = bi * last + s
    n_tot = pl.num_programs(0) * last
    qi = qi_ref[0]
    w = w_ref[0]
    start, wait = _page_fetcher(pt_ref, [cki_hbm], [kbuf], sem, g)

    def pages(slot):
        kcat = jnp.concatenate([kbuf[slot, j].astype(bf16) for j in range(g)], axis=1)
        keys_ref[0] = _sortable(_idx_scores(qi, w, kcat))

    _run_paged(t, n_tot, s < last, start, wait, kbuf.shape[0], pages)

    @pl.when(s == last)
    def _():
        keys_ref[0] = jnp.full(keys_ref.shape[1:], INT_MIN, i32)
        sc = _idx_scores(qi, w, kin_ref[0])
        tok = lax.broadcasted_iota(i32, sc.shape, 0)
        j = lax.broadcasted_iota(i32, sc.shape, 1)
        keys_ref[0, :, 0:PAGE_SIZE] = jnp.where(j <= tok, _sortable(sc), INT_MIN)


def _ds_thr_kernel(keys_ref, thr_ref, need_ref, *, n_sel, chunk_w):
    rows, width = keys_ref.shape

    def get_chunk(c):
        return keys_ref[:, c * chunk_w:(c + 1) * chunk_w]

    thr, need = _select_threshold(get_chunk, width // chunk_w, chunk_w, rows, n_sel, False)
    thr_ref[...] = jnp.broadcast_to(thr, thr_ref.shape)
    need_ref[...] = jnp.broadcast_to(need, need_ref.shape)


def _ds_attn_kernel(pt_ref, q_ref, keys_ref, thr_ref, need_ref, knew_ref, vnew_ref, ck_hbm, cv_hbm,
                    o_ref, m_ref, l_ref, acc_ref, seen_ref, kbuf, vbuf, sem, *, g):
    bi = pl.program_id(0)
    s = pl.program_id(1)
    last = pl.num_programs(1) - 1
    t = bi * last + s
    n_tot = pl.num_programs(0) * last
    start, wait = _page_fetcher(pt_ref, [ck_hbm, cv_hbm], [kbuf, vbuf], sem, g)
    q = q_ref[0]
    thr = thr_ref[0][:, 0:1]
    need = need_ref[0][:, 0:1]
    tri = _tie_prefix_matrix()

    def process(keys, kps, vps):
        n = len(kps)
        seen = seen_ref[...]
        sels, seen = _selected_blocks([keys[:, j * PAGE_SIZE:(j + 1) * PAGE_SIZE] for j in range(n)],
                                      thr, need, seen, tri)
        bias = [jnp.where(sel, 0.0, NEG_BIG) for sel in sels]
        seen_ref[...] = seen
        bias = jnp.concatenate(bias, axis=1) if n > 1 else bias[0]
        kcat = jnp.concatenate([kp.astype(bf16) for kp in kps], axis=1) if n > 1 else kps[0].astype(bf16)
        vcat = jnp.concatenate([vp.astype(bf16) for vp in vps], axis=1) if n > 1 else vps[0].astype(bf16)
        lg = jnp.dot(q, kcat, preferred_element_type=f32)
        lg = lg.reshape(N_HEADS, SUBLANES, n * PAGE_SIZE) + bias[None]
        m_old = m_ref[...]
        m_new = jnp.maximum(m_old, jnp.max(lg, axis=-1, keepdims=True))
        p = jnp.exp(lg - m_new)
        alpha = jnp.exp(m_old - m_new)
        l_ref[...] = alpha * l_ref[...] + jnp.sum(p, axis=-1, keepdims=True)
        pb = p.reshape(N_HEADS * SUBLANES, n * PAGE_SIZE).astype(bf16)
        pv = lax.dot_general(pb, vcat, (((1,), (1,)), ((), ())), preferred_element_type=f32)
        acc_ref[...] = alpha * acc_ref[...] + pv.reshape(N_HEADS, SUBLANES, LANES)
        m_ref[...] = m_new

    @pl.when(s == 0)
    def _():
        m_ref[...] = jnp.full(m_ref.shape, NEG_BIG, f32)
        l_ref[...] = jnp.zeros(l_ref.shape, f32)
        acc_ref[...] = jnp.zeros(acc_ref.shape, f32)
        seen_ref[...] = jnp.zeros(seen_ref.shape, f32)

    def pages(slot):
        process(keys_ref[0], [kbuf[slot, j] for j in range(g)], [vbuf[slot, j] for j in range(g)])

    _run_paged(t, n_tot, s < last, start, wait, kbuf.shape[0], pages)

    @pl.when(s == last)
    def _():
        process(keys_ref[0, :, 0:PAGE_SIZE], [knew_ref[0]], [vnew_ref[0]])
        o_ref[0] = (acc_ref[...] / l_ref[...]).reshape(N_HEADS * SUBLANES, LANES)


def dsa_sample(q, qi, sm, k, v, cache_k, cache_v, cache_idx_k, page_table):
    b, t = q.shape[:2]
    assert t <= SUBLANES
    n_pages = page_table.shape[1]
    g = min(PAGES_PER_STEP, n_pages)
    assert n_pages % g == 0
    ns = n_pages // g
    past = n_pages * PAGE_SIZE
    n_sel = min(TOPK_MAX, (past + t) // 4)
    pt = page_table.reshape(-1).astype(i32)
    padt = lambda a: jnp.pad(a, ((0, 0), (0, SUBLANES - t)) + ((0, 0),) * (a.ndim - 2))

    qi_r = padt(qi.reshape(b, t, IDX_HEADS, IDX_DIM)).transpose(0, 2, 1, 3).reshape(b, IDX_HEADS * SUBLANES, IDX_DIM)
    w_r = padt(sm[:, :, SM_WI:SM_WI + IDX_HEADS]).transpose(0, 2, 1).reshape(b, IDX_HEADS * SUBLANES, 1)
    padk = lambda a: jnp.pad(a, ((0, 0), (0, PAGE_SIZE - t), (0, 0))).transpose(0, 2, 1)
    ki_new = padk(sm[:, :, :IDX_DIM])
    k_new = padk(k)
    v_new = padk(v)
    qh = padt(q.reshape(b, t, N_HEADS, HEAD_DIM)).transpose(0, 2, 1, 3) * (HEAD_DIM ** -0.5)
    grp = (jnp.arange(N_HEADS) // GROUP)[None, :, None, None]
    q_r = jnp.concatenate([jnp.where(grp == 0, qh, 0.0), jnp.where(grp == 1, qh, 0.0)], axis=-1)
    q_r = q_r.reshape(b, N_HEADS * SUBLANES, LANES).astype(bf16)
    n_pool = cache_k.shape[0]
    ck = cache_k.transpose(0, 2, 3, 1).reshape(n_pool, N_KV_HEADS * HEAD_DIM, PAGE_SIZE)
    cv = cache_v.transpose(0, 2, 3, 1).reshape(n_pool, N_KV_HEADS * HEAD_DIM, PAGE_SIZE)
    cki = cache_idx_k.transpose(0, 2, 1)

    hbm = pl.BlockSpec(memory_space=pl.ANY)
    per_b = lambda a: pl.BlockSpec((1,) + a.shape[1:], lambda bi, s, ptr: (bi,) + (0,) * (a.ndim - 1))
    blk_w = g * PAGE_SIZE
    keys_spec = pl.BlockSpec((1, SUBLANES, blk_w), lambda bi, s, ptr: (bi, 0, s))
    width = past + blk_w

    keys = pl.pallas_call(
        functools.partial(_ds_score_kernel, g=g),
        grid_spec=pltpu.PrefetchScalarGridSpec(
            num_scalar_prefetch=1, grid=(b, ns + 1),
            in_specs=[per_b(qi_r), per_b(w_r), per_b(ki_new), hbm],
            out_specs=keys_spec,
            scratch_shapes=[pltpu.VMEM((PAGE_SLOTS, g, IDX_DIM, PAGE_SIZE), f32),
                            pltpu.SemaphoreType.DMA((1, PAGE_SLOTS))]),
        out_shape=jax.ShapeDtypeStruct((b, SUBLANES, width), i32),
        compiler_params=_cparams(("arbitrary", "arbitrary")),
        name="dsa_sample_scores",
    )(pt, qi_r.astype(bf16), w_r, ki_new, cki)

    rows = b * SUBLANES
    thr, need = pl.pallas_call(
        functools.partial(_ds_thr_kernel, n_sel=n_sel, chunk_w=blk_w),
        out_shape=[jax.ShapeDtypeStruct((rows, LANES), i32), jax.ShapeDtypeStruct((rows, LANES), f32)],
        compiler_params=pltpu.CompilerParams(vmem_limit_bytes=VMEM_LIMIT),
        name="dsa_sample_threshold",
    )(keys.reshape(rows, width))
    thr = thr.reshape(b, SUBLANES, LANES)
    need = need.reshape(b, SUBLANES, LANES)

    out = pl.pallas_call(
        functools.partial(_ds_attn_kernel, g=g),
        grid_spec=pltpu.PrefetchScalarGridSpec(
            num_scalar_prefetch=1, grid=(b, ns + 1),
            in_specs=[per_b(q_r), keys_spec, per_b(thr), per_b(need), per_b(k_new), per_b(v_new), hbm, hbm],
            out_specs=per_b(q_r),
            scratch_shapes=[pltpu.VMEM((N_HEADS, SUBLANES, 1), f32), pltpu.VMEM((N_HEADS, SUBLANES, 1), f32),
                            pltpu.VMEM((N_HEADS, SUBLANES, LANES), f32), pltpu.VMEM((SUBLANES, 1), f32),
                            pltpu.VMEM((PAGE_SLOTS, g, LANES, PAGE_SIZE), f32),
                            pltpu.VMEM((PAGE_SLOTS, g, LANES, PAGE_SIZE), f32),
                            pltpu.SemaphoreType.DMA((2, PAGE_SLOTS))]),
        out_shape=jax.ShapeDtypeStruct((b, N_HEADS * SUBLANES, LANES), f32),
        compiler_params=_cparams(("arbitrary", "arbitrary")),
        name="dsa_sample_attention",
    )(pt, q_r, keys, thr, need, k_new, v_new, ck, cv)
    out = out.reshape(b, N_HEADS, SUBLANES, N_KV_HEADS, HEAD_DIM)[:, :, :t]
    out = jnp.concatenate([out[:, :GROUP, :, 0], out[:, GROUP:, :, 1]], axis=1)
    return out.transpose(0, 2, 1, 3).reshape(b, t, A_WIDTH)


def _mid_kernel(x_ref, oa_ref, ob_ref, ga_ref, gb_ref, wa_ref, wb_ref, wo_ref, gf_ref, wrh_ref, wrl_ref, br_ref,
                x1_ref, te_ref, tg_ref):
    a = jnp.dot(oa_ref[...].astype(bf16), wa_ref[...], preferred_element_type=f32)
    b = jnp.dot(ob_ref[...].astype(bf16), wb_ref[...], preferred_element_type=f32)
    merged = ga_ref[...] * a + gb_ref[...] * b
    x1 = x_ref[...] + jnp.dot(merged.astype(bf16), wo_ref[...], preferred_element_type=f32)
    x1_ref[...] = x1
    h2 = _rms(x1, gf_ref[...])
    hi = h2.astype(bf16)
    lo = (h2 - hi.astype(f32)).astype(bf16)
    lg = (jnp.dot(hi, wrh_ref[...], preferred_element_type=f32) + jnp.dot(lo, wrh_ref[...], preferred_element_type=f32)
          + jnp.dot(hi, wrl_ref[...], preferred_element_type=f32)) + br_ref[...]
    lane = lax.broadcasted_iota(i32, lg.shape, 1)
    lane_f = lane.astype(f32)
    vals, ids = [], []
    for _ in range(TOP_K):
        m = jnp.max(lg, axis=1, keepdims=True)
        idx = jnp.min(jnp.where(lg == m, lane_f, float(LANES)), axis=1, keepdims=True).astype(i32)
        vals.append(m)
        ids.append(idx)
        lg = jnp.where(lane == idx, -jnp.inf, lg)
    ex = [jnp.exp(v - vals[0]) for v in vals]
    tot = ex[0] + ex[1] + ex[2] + ex[3]
    te = jnp.zeros(lg.shape, i32)
    tg = jnp.zeros(lg.shape, f32)
    for j in range(TOP_K):
        te = jnp.where(lane == j, ids[j], te)
        tg = jnp.where(lane == j, ex[j] / tot, tg)
    te_ref[...] = te
    tg_ref[...] = tg


def mid(x, oa, ob, ga, gb, wa, wb, wo, g_ffn, wr_hi, wr_lo, br):
    n = x.shape[0]
    tm = min(256, n)
    assert n % tm == 0
    row = lambda w: pl.BlockSpec((tm, w), lambda i: (i, 0))
    full = lambda a: pl.BlockSpec(a.shape, lambda i: (0,) * a.ndim)
    return pl.pallas_call(
        _mid_kernel,
        grid=(n // tm,),
        in_specs=[row(D_MODEL), row(A_WIDTH), row(M_WIDTH), row(D_MODEL), row(D_MODEL),
                  full(wa), full(wb), full(wo), full(g_ffn), full(wr_hi), full(wr_lo), full(br)],
        out_specs=[row(D_MODEL), row(LANES), row(LANES)],
        out_shape=[jax.ShapeDtypeStruct((n, D_MODEL), f32),
                   jax.ShapeDtypeStruct((n, LANES), i32), jax.ShapeDtypeStruct((n, LANES), f32)],
        compiler_params=_cparams(("parallel",)),
        name="mid",
    )(x, oa, ob, ga, gb, wa, wb, wo, g_ffn, wr_hi, wr_lo, br)


def _rank_kernel(te_ref, dest_ref, meta_ref, einfo_ref, cnt_ref, carry_ref, *, blk):
    ph = pl.program_id(0)
    i = pl.program_id(1)
    tm = te_ref.shape[0]
    lane = lax.broadcasted_iota(i32, (tm, LANES), 1)
    te = te_ref[...]
    oh = jnp.zeros((tm, LANES), f32)
    for j in range(TOP_K):
        oh = oh + jnp.where(lane == te[:, j:j + 1], 1.0, 0.0)
    tile_cnt = jnp.sum(oh, axis=0, keepdims=True)

    @pl.when((ph == 0) & (i == 0))
    def _():
        cnt_ref[...] = jnp.zeros(cnt_ref.shape, f32)

    @pl.when(ph == 0)
    def _():
        cnt_ref[...] += tile_cnt

    @pl.when((ph == 1) & (i == 0))
    def _():
        cnt = cnt_ref[...]
        padded = jnp.floor((cnt + (blk - 1)) / blk) * blk
        r = lax.broadcasted_iota(i32, (LANES, LANES), 0)
        c = lax.broadcasted_iota(i32, (LANES, LANES), 1)
        col = jnp.sum(jnp.where(r == c, padded, 0.0), axis=1, keepdims=True)
        start = jnp.sum(jnp.where(r < c, col, 0.0), axis=0, keepdims=True)
        carry_ref[...] = start
        pad_end = start + padded
        row8 = lax.broadcasted_iota(i32, einfo_ref.shape, 0)
        einfo_ref[...] = jnp.where(row8 == 0, start + cnt, jnp.where(row8 == 1, pad_end, 0.0)).astype(i32)
        nbp = meta_ref.shape[0]
        jb = (lax.broadcasted_iota(i32, (nbp, LANES), 0) * blk).astype(f32)
        lane2 = lax.broadcasted_iota(i32, (nbp, LANES), 1)
        be = jnp.sum(jnp.where((pad_end <= jb) & (lane2 < N_EXPERTS), 1.0, 0.0), axis=1, keepdims=True)
        be = jnp.minimum(be, float(N_EXPERTS - 1))
        n_act = jnp.sum(jnp.where(lane2 == N_EXPERTS - 1, pad_end, 0.0), axis=1, keepdims=True) / blk
        meta_ref[...] = jnp.where(lane2 == 0, be, jnp.where(lane2 == 1, n_act, 0.0)).astype(i32)

    @pl.when(ph == 1)
    def _():
        r = lax.broadcasted_iota(i32, (tm, tm), 0)
        c = lax.broadcasted_iota(i32, (tm, tm), 1)
        before = jnp.where(c < r, 1.0, 0.0).astype(bf16)
        pos = carry_ref[...] + jnp.dot(before, oh.astype(bf16), preferred_element_type=f32)
        d = jnp.zeros((tm, LANES), f32)
        for j in range(TOP_K):
            dj = jnp.sum(jnp.where(lane == te[:, j:j + 1], pos, 0.0), axis=1, keepdims=True)
            d = jnp.where(lane == j, dj, d)
        dest_ref[...] = d.astype(i32)
        carry_ref[...] += tile_cnt


def rank(te, blk, n_blocks):
    n = te.shape[0]
    tm = next(c for c in (512, 384, 256, 128, 64, 32, 16, SUBLANES) if n % c == 0)
    nbp = -(-n_blocks // SUBLANES) * SUBLANES
    return pl.pallas_call(
        functools.partial(_rank_kernel, blk=blk),
        grid=(2, n // tm),
        in_specs=[pl.BlockSpec((tm, LANES), lambda ph, i: (i, 0))],
        out_specs=[pl.BlockSpec((tm, LANES), lambda ph, i: (i * ph, 0)),
                   pl.BlockSpec((nbp, LANES), lambda ph, i: (0, 0)),
                   pl.BlockSpec((SUBLANES, LANES), lambda ph, i: (0, 0))],
        out_shape=[jax.ShapeDtypeStruct((n, LANES), i32), jax.ShapeDtypeStruct((nbp, LANES), i32),
                   jax.ShapeDtypeStruct((SUBLANES, LANES), i32)],
        scratch_shapes=[pltpu.VMEM((1, LANES), f32), pltpu.VMEM((1, LANES), f32)],
        compiler_params=_cparams(("arbitrary", "arbitrary")),
        name="rank",
    )(te)


def _ffn_kernel(be_ref, nact_ref, rt_ref, x_hbm, *rest):
    wgu_refs = rest[:W_SPLIT]
    bgu_ref = rest[W_SPLIT]
    wdn_refs = rest[W_SPLIT + 1:2 * W_SPLIT + 1]
    bdn_ref, y_ref, xbuf, sem, wgu_bf, wdn_bf = rest[2 * W_SPLIT + 1:]
    i = pl.program_id(0)
    n_act = nact_ref[0]
    blk = xbuf.shape[1]

    def row_copy(b, r, slot):
        tok = rt_ref[b * blk + r]
        return pltpu.make_async_copy(x_hbm.at[pl.ds(tok, 1)], xbuf.at[slot, pl.ds(r, 1)], sem.at[slot])

    def wait_slot(slot):
        pltpu.make_async_copy(xbuf.at[slot], xbuf.at[slot], sem.at[slot]).wait()

    @pl.when(i == 0)
    def _():
        def body(r, c):
            row_copy(0, r, 0).start()
            return c
        lax.fori_loop(0, blk, body, 0)

    @pl.when(i == n_act)
    def _():
        wait_slot(i % 2)

    @pl.when(i >= n_act)
    def _():
        y_ref[...] = jnp.zeros(y_ref.shape, f32)

    @pl.when((i < n_act) & ((i == 0) | (be_ref[i] != be_ref[jnp.maximum(i - 1, 0)])))
    def _():
        wg = wgu_bf.shape[1] // W_SPLIT
        wd = wdn_bf.shape[1] // W_SPLIT
        for c in range(W_SPLIT):
            wgu_bf[:, c * wg:(c + 1) * wg] = wgu_refs[c][0].astype(bf16)
            wdn_bf[:, c * wd:(c + 1) * wd] = wdn_refs[c][0].astype(bf16)

    def block(slot):
        wait_slot(slot)
        for r in range(blk):
            row_copy(i + 1, r, 1 - slot).start()
        x = xbuf[slot].astype(bf16)
        gu = jnp.dot(x, wgu_bf[...], preferred_element_type=f32) + bgu_ref[0]
        gate = jnp.minimum(gu[:, :EXPERT_FF], SWIGLU_LIMIT)
        up = jnp.clip(gu[:, EXPERT_FF:], -SWIGLU_LIMIT, SWIGLU_LIMIT)
        act = (up + 1.0) * (gate * _sigmoid(SWIGLU_ALPHA * gate))
        y_ref[...] = jnp.dot(act.astype(bf16), wdn_bf[...], preferred_element_type=f32) + bdn_ref[0]

    for slot in range(2):
        pl.when((i < n_act) & (i % 2 == slot))(functools.partial(block, slot))


def ffn(blk_e, n_act, row_tok, h2, w_gu, b_gu, w_dn, b_dn, blk, n_blocks):
    d = h2.shape[1]
    ff2 = w_gu.shape[2]
    return pl.pallas_call(
        _ffn_kernel,
        grid_spec=pltpu.PrefetchScalarGridSpec(
            num_scalar_prefetch=3, grid=(n_blocks,),
            in_specs=[pl.BlockSpec(memory_space=pl.ANY)]
            + [pl.BlockSpec((1, d, ff2 // W_SPLIT), functools.partial(lambda i, be, na, rt, c: (be[i], 0, c), c=c))
               for c in range(W_SPLIT)]
            + [pl.BlockSpec((1, 1, ff2), lambda i, be, na, rt: (be[i], 0, 0))]
            + [pl.BlockSpec((1, ff2 // 2, d // W_SPLIT), functools.partial(lambda i, be, na, rt, c: (be[i], 0, c), c=c))
               for c in range(W_SPLIT)]
            + [pl.BlockSpec((1, 1, d), lambda i, be, na, rt: (be[i], 0, 0))],
            out_specs=pl.BlockSpec((blk, d), lambda i, be, na, rt: (i, 0)),
            scratch_shapes=[pltpu.VMEM((2, blk, d), f32), pltpu.SemaphoreType.DMA((2,)),
                            pltpu.VMEM((d, ff2), bf16), pltpu.VMEM((ff2 // 2, d), bf16)]),
        out_shape=jax.ShapeDtypeStruct((n_blocks * blk, d), f32),
        compiler_params=_cparams(("arbitrary",)),
        name="ffn",
    )(blk_e, n_act, row_tok, h2, *([w_gu] * W_SPLIT), b_gu.reshape(b_gu.shape[0], 1, ff2),
      *([w_dn] * W_SPLIT), b_dn.reshape(b_dn.shape[0], 1, d))


def _dispatch_kernel(dest_ref, einfo_ref, x1_ref, g_ref, xs_hbm, hbuf, zrow, zblk, sem, zsem):
    i = pl.program_id(0)
    n = pl.num_programs(0)
    tm = x1_ref.shape[0]

    def pad_rows(e, act):
        def body(r, c):
            cp = pltpu.make_async_copy(zrow, xs_hbm.at[pl.ds(r, 1)], zsem)
            cp.start() if act == "start" else cp.wait()
            return c
        lax.fori_loop(einfo_ref[e], einfo_ref[LANES + e], body, 0)

    def tail_blocks(act):
        blk = zblk.shape[0]

        def body(b, c):
            cp = pltpu.make_async_copy(zblk, xs_hbm.at[pl.ds(b * blk, blk)], zsem)
            cp.start() if act == "start" else cp.wait()
            return c
        lax.fori_loop(einfo_ref[LANES + N_EXPERTS - 1] // blk, xs_hbm.shape[0] // blk, body, 0)

    @pl.when(i == 0)
    def _():
        zrow[...] = jnp.zeros(zrow.shape, f32)
        zblk[...] = jnp.zeros(zblk.shape, f32)
        for e in range(N_EXPERTS):
            pad_rows(e, "start")
        tail_blocks("start")
        for e in range(N_EXPERTS):
            pad_rows(e, "wait")
        tail_blocks("wait")

    def row_copy(r, j, slot):
        d = dest_ref[(i * tm + r) * TOP_K + j]
        return pltpu.make_async_copy(hbuf.at[slot, pl.ds(r, 1)], xs_hbm.at[pl.ds(d, 1)], sem.at[slot])

    def wait_slot(slot):
        for _ in range(TOP_K):
            pltpu.make_async_copy(hbuf.at[slot], hbuf.at[slot], sem.at[slot]).wait()

    def tile(slot):
        pl.when(i >= 2)(functools.partial(wait_slot, slot))
        hbuf[slot] = _rms(x1_ref[...], g_ref[...])
        for r in range(tm):
            for j in range(TOP_K):
                row_copy(r, j, slot).start()
        pl.when(i == n - 1)(functools.partial(wait_slot, slot))
        pl.when((i == n - 1) & (i >= 1))(functools.partial(wait_slot, 1 - slot))

    for slot in range(2):
        pl.when(i % 2 == slot)(functools.partial(tile, slot))


def dispatch(dest_flat, einfo, x1, g_ffn, n_rows):
    n, d = x1.shape
    tm = next(c for c in (OUT_TM, 64, 32, 16, SUBLANES) if n % c == 0)
    return pl.pallas_call(
        _dispatch_kernel,
        grid_spec=pltpu.PrefetchScalarGridSpec(
            num_scalar_prefetch=2, grid=(n // tm,),
            in_specs=[pl.BlockSpec((tm, d), lambda i, ds, ei: (i, 0)),
                      pl.BlockSpec(g_ffn.shape, lambda i, ds, ei: (0, 0))],
            out_specs=pl.BlockSpec(memory_space=pl.ANY),
            scratch_shapes=[pltpu.VMEM((2, tm, d), f32), pltpu.VMEM((1, d), f32), pltpu.VMEM((FFN_BLK, d), f32),
                            pltpu.SemaphoreType.DMA((2,)), pltpu.SemaphoreType.DMA(())]),
        out_shape=jax.ShapeDtypeStruct((n_rows, d), f32),
        compiler_params=_cparams(("arbitrary",)),
        name="dispatch",
    )(dest_flat, einfo.reshape(-1), x1, g_ffn)


def _ffn_rows_kernel(be_ref, nact_ref, xs_ref, *rest):
    wgu_refs = rest[:W_SPLIT]
    bgu_ref = rest[W_SPLIT]
    wdn_refs = rest[W_SPLIT + 1:2 * W_SPLIT + 1]
    bdn_ref, y_ref, wgu_bf, wdn_bf = rest[2 * W_SPLIT + 1:]
    i = pl.program_id(0)
    n_act = nact_ref[0]

    @pl.when(i >= n_act)
    def _():
        y_ref[...] = jnp.zeros(y_ref.shape, f32)

    @pl.when((i < n_act) & ((i == 0) | (be_ref[i] != be_ref[jnp.maximum(i - 1, 0)])))
    def _():
        wg = wgu_bf.shape[1] // W_SPLIT
        wd = wdn_bf.shape[1] // W_SPLIT
        for c in range(W_SPLIT):
            wgu_bf[:, c * wg:(c + 1) * wg] = wgu_refs[c][0].astype(bf16)
            wdn_bf[:, c * wd:(c + 1) * wd] = wdn_refs[c][0].astype(bf16)

    @pl.when(i < n_act)
    def _():
        x = xs_ref[...].astype(bf16)
        gu = jnp.dot(x, wgu_bf[...], preferred_element_type=f32) + bgu_ref[0]
        gate = jnp.minimum(gu[:, :EXPERT_FF], SWIGLU_LIMIT)
        up = jnp.clip(gu[:, EXPERT_FF:], -SWIGLU_LIMIT, SWIGLU_LIMIT)
        act = (up + 1.0) * (gate * _sigmoid(SWIGLU_ALPHA * gate))
        y_ref[...] = jnp.dot(act.astype(bf16), wdn_bf[...], preferred_element_type=f32) + bdn_ref[0]


def ffn_rows(blk_e, n_act, xs, w_gu, b_gu, w_dn, b_dn, blk, n_blocks):
    d = xs.shape[1]
    ff2 = w_gu.shape[2]
    wspec = lambda shape, col: pl.BlockSpec(shape, functools.partial(lambda i, be, na, c: (be[i], 0, c), c=col))
    return pl.pallas_call(
        _ffn_rows_kernel,
        grid_spec=pltpu.PrefetchScalarGridSpec(
            num_scalar_prefetch=2, grid=(n_blocks,),
            in_specs=[pl.BlockSpec((blk, d), lambda i, be, na: (jnp.minimum(i, jnp.maximum(na[0] - 1, 0)), 0))]
            + [wspec((1, d, ff2 // W_SPLIT), c) for c in range(W_SPLIT)]
            + [pl.BlockSpec((1, 1, ff2), lambda i, be, na: (be[i], 0, 0))]
            + [wspec((1, ff2 // 2, d // W_SPLIT), c) for c in range(W_SPLIT)]
            + [pl.BlockSpec((1, 1, d), lambda i, be, na: (be[i], 0, 0))],
            out_specs=pl.BlockSpec((blk, d), lambda i, be, na: (i, 0)),
            scratch_shapes=[pltpu.VMEM((d, ff2), bf16), pltpu.VMEM((ff2 // 2, d), bf16)]),
        out_shape=jax.ShapeDtypeStruct((n_blocks * blk, d), f32),
        compiler_params=_cparams(("arbitrary",)),
        name="ffn",
    )(blk_e, n_act, xs, *([w_gu] * W_SPLIT), b_gu.reshape(b_gu.shape[0], 1, ff2),
      *([w_dn] * W_SPLIT), b_dn.reshape(b_dn.shape[0], 1, d))


def _out_kernel(dest_ref, x1_ref, tg_ref, p_ref, ys_hbm, gp_ref, wg_ref, wp_ref, gfin_ref, y_ref, buf, sem):
    i = pl.program_id(0)
    n = pl.num_programs(0)
    tm = x1_ref.shape[0]

    def row_copy(t, r, j, slot):
        d = dest_ref[(t * tm + r) * TOP_K + j]
        return pltpu.make_async_copy(ys_hbm.at[pl.ds(d, 1)], buf.at[slot, j, pl.ds(r, 1)], sem.at[slot])

    def wait_slot(slot):
        pltpu.make_async_copy(buf.at[slot], buf.at[slot], sem.at[slot]).wait()

    @pl.when(i == 0)
    def _():
        def body(r, c):
            for j in range(TOP_K):
                row_copy(0, r, j, 0).start()
            return c
        lax.fori_loop(0, tm, body, 0)

    def tile(slot):
        wait_slot(slot)
        nxt = jnp.minimum(i + 1, n - 1)
        for r in range(tm):
            for j in range(TOP_K):
                row_copy(nxt, r, j, 1 - slot).start()
        tg = tg_ref[...]
        x2 = x1_ref[...]
        for j in range(TOP_K):
            x2 = x2 + tg[:, j:j + 1] * buf[slot, j]
        hn = _rms(x2, gp_ref[...]).astype(bf16)
        gate = _sigmoid(jnp.dot(hn, wg_ref[...], preferred_element_type=f32))
        x3 = x2 + gate * jnp.dot(p_ref[...].astype(bf16), wp_ref[...], preferred_element_type=f32)
        y_ref[...] = _rms(x3, gfin_ref[...])

    for slot in range(2):
        pl.when(i % 2 == slot)(functools.partial(tile, slot))

    @pl.when(i == n - 1)
    def _():
        wait_slot(1 - i % 2)


def out_stage(dest_flat, x1, tg, p, ys, g_ple, wg, wp, g_final):
    n, d = x1.shape
    tm = min(OUT_TM, n)
    assert n % tm == 0
    row = lambda w: pl.BlockSpec((tm, w), lambda i, ds: (i, 0))
    full = lambda a: pl.BlockSpec(a.shape, lambda i, ds: (0,) * a.ndim)
    return pl.pallas_call(
        _out_kernel,
        grid_spec=pltpu.PrefetchScalarGridSpec(
            num_scalar_prefetch=1, grid=(n // tm,),
            in_specs=[row(d), row(LANES), row(p.shape[1]), pl.BlockSpec(memory_space=pl.ANY),
                      full(g_ple), full(wg), full(wp), full(g_final)],
            out_specs=row(d),
            scratch_shapes=[pltpu.VMEM((2, TOP_K, tm, d), f32), pltpu.SemaphoreType.DMA((2,))]),
        out_shape=jax.ShapeDtypeStruct((n, d), f32),
        compiler_params=_cparams(("arbitrary",)),
        name="out",
    )(dest_flat, x1, tg, p, ys, g_ple, wg, wp, g_final)


def mid_stage(x, oa, ob, ga, gb, w):
    return mid(x, oa, ob, ga, gb, w["wa"], w["wb"], w["wo"], w["g_ffn"], w["wr_hi"], w["wr_lo"], w["br"])


def moe_out(parts, w):
    x1_all = jnp.concatenate([pt[0] for pt in parts], axis=0)
    te = jnp.concatenate([pt[1] for pt in parts], axis=0)
    n = x1_all.shape[0]
    blk = FFN_BLK
    n_blocks = -(-(n * TOP_K) // blk) + N_EXPERTS
    dest, meta, einfo = rank(te, blk, n_blocks)
    dest_flat = dest[:, :TOP_K].reshape(-1)
    blk_e = meta[:n_blocks, 0]
    n_act = meta[0:1, 1]
    xs = dispatch(dest_flat, einfo, x1_all, w["g_ffn"], n_blocks * blk)
    ys = ffn_rows(blk_e, n_act, xs, w["w_gu"], w["b_gu"], w["w_dn"], w["b_dn"], blk, n_blocks)
    outs, off = [], 0
    for x1, _, tg, p in parts:
        ni = x1.shape[0]
        outs.append(out_stage(dest_flat[off * TOP_K:(off + ni) * TOP_K], x1, tg, p, ys,
                              w["g_ple"], w["wg"], w["wp"], w["g_final"]))
        off += ni
    return outs


def tail(x, oa, ob, ga, gb, p, w):
    return moe_out([mid_stage(x, oa, ob, ga, gb, w) + (p,)], w)[0]


def _prep_weights(g_mix, w_in, b_gate, g_mnorm, w_up_a, w_up_b, w_out, g_ffn, w_router, b_router,
                  w_gu, b_gu, w_dn, b_dn, g_ple, w_ple_gate, w_ple_proj, g_final):
    wr = jnp.pad(w_router.astype(f32), ((0, 0), (0, LANES - N_EXPERTS)))
    wr_hi = wr.astype(bf16)
    wr_lo = (wr - wr_hi.astype(f32)).astype(bf16)
    br = jnp.full((1, LANES), -jnp.inf, f32).at[0, :N_EXPERTS].set(b_router.astype(f32))
    bias = jnp.zeros((1, LANES), f32).at[0, SM_IG:SM_IG + 2 * M_HEADS].set(b_gate.astype(f32))
    return dict(g_mix=g_mix.reshape(1, -1), w_in=_pack_w_in(w_in), bias=bias, g_mnorm=g_mnorm,
                wa=w_up_a.astype(bf16), wb=w_up_b.astype(bf16), wo=w_out.astype(bf16), g_ffn=g_ffn.reshape(1, -1),
                wr_hi=wr_hi, wr_lo=wr_lo, br=br, w_gu=w_gu, b_gu=b_gu, w_dn=w_dn, b_dn=b_dn,
                g_ple=g_ple.reshape(1, -1), wg=w_ple_gate.astype(bf16), wp=w_ple_proj.astype(bf16),
                g_final=g_final.reshape(1, -1))


def _layer(x, p, pos, attn_fn, c0, n0, m0, w):
    b, t = x.shape[:2]
    n = b * t
    cos, sin = _rope_tables(pos)
    cos = jnp.tile(cos, (b, 1))
    sin = jnp.tile(sin, (b, 1))
    x2d = x.reshape(n, D_MODEL)
    q, k, v, qi, sm, mq, mk, mv, og, ga, gb = in_proj(x2d, cos, sin, w["g_mix"], w["w_in"], w["bias"])
    r3 = lambda a: a.reshape(b, t, a.shape[-1])
    o_a = attn_fn(r3(q), r3(qi), r3(sm), r3(k), r3(v))
    o_b, c, nn, m = mlstm_branch(r3(mq), r3(mk), r3(mv), r3(og), r3(sm), c0, n0, m0, w["g_mnorm"])
    part = mid_stage(x2d, o_a.reshape(n, A_WIDTH), o_b.reshape(n, M_WIDTH), ga, gb, w) + (p.reshape(n, PLE_DIM),)
    state = (k.reshape(b, t, N_KV_HEADS, HEAD_DIM), v.reshape(b, t, N_KV_HEADS, HEAD_DIM),
             sm[:, :IDX_DIM].reshape(b, t, IDX_DIM), c, nn, m)
    return part, state


def kernel(x_prompt, x_sample, cache_k, cache_v, cache_idx_k, state_C, state_n, state_m, page_table,
           p_prompt, p_sample, g_mix, w_in, b_gate, g_mnorm, w_up_a, w_up_b, w_out, g_ffn,
           w_router, b_router, w_gu, b_gu, w_dn, b_dn, g_ple, w_ple_gate, w_ple_proj, g_final):
    assert x_prompt.shape[-1] == D_MODEL and w_in.shape[0] == 1, "single-layer model of width D_MODEL"
    bp, tp = x_prompt.shape[:2]
    ts = x_sample.shape[1]
    past = page_table.shape[1] * PAGE_SIZE
    w = _prep_weights(g_mix[0], w_in[0], b_gate[0], g_mnorm[0], w_up_a[0], w_up_b[0], w_out[0], g_ffn[0],
                      w_router[0], b_router[0], w_gu[0], b_gu[0], w_dn[0], b_dn[0], g_ple[0],
                      w_ple_gate[0], w_ple_proj[0], g_final)
    zeros = lambda *s: jnp.zeros(s, f32)
    part_p, sp = _layer(x_prompt, p_prompt[0], jnp.arange(tp, dtype=i32), dsa_prompt_branch,
                        zeros(bp, M_HEADS, M_V, M_QK), zeros(bp, M_HEADS, M_QK), zeros(bp, M_HEADS), w)
    attn_s = functools.partial(dsa_sample, cache_k=cache_k[0], cache_v=cache_v[0], cache_idx_k=cache_idx_k[0],
                               page_table=page_table)
    part_s, ss = _layer(x_sample, p_sample[0], past + jnp.arange(ts, dtype=i32), attn_s,
                        state_C[0], state_n[0], state_m[0], w)
    yp, ys = moe_out([part_p, part_s], w)
    return ((yp.reshape(x_prompt.shape), ys.reshape(x_sample.shape))
            + tuple(s[None] for s in sp) + tuple(s[None] for s in ss))
```

```python
import functools

import jax
import jax.numpy as jnp
from jax import lax
from jax.experimental import pallas as pl
from jax.experimental.pallas import tpu as pltpu

f32 = jnp.float32
bf16 = jnp.bfloat16
i32 = jnp.int32

D_MODEL = 1024
PAGE_SIZE = 128
N_HEADS = 8
N_KV_HEADS = 2
HEAD_DIM = 64
GROUP = N_HEADS // N_KV_HEADS
IDX_HEADS = 4
IDX_DIM = 64
TOPK_MAX = 256
ROPE_THETA = 10000.0
M_HEADS = 4
M_QK = 64
M_V = 128
N_EXPERTS = 32
TOP_K = 4
EXPERT_FF = D_MODEL
SWIGLU_LIMIT = 7.0
SWIGLU_ALPHA = 1.702
PLE_DIM = 256
EPS = 1e-6
A_WIDTH = N_HEADS * HEAD_DIM
M_WIDTH = M_HEADS * M_V

LANES = 128
SUBLANES = 8
VMEM_LIMIT = 56 * 1024 * 1024

M_CHUNK = 128
MLSTM_BATCH = 1
TQ = 256
KC = 1024
PAGE_SLOTS = 4
PAGES_PER_STEP = 16
FFN_BLK = 256
W_SPLIT = 1
OUT_TM = 128

INT_MIN = -2147483648
INT_MAX = 2147483647
NEG_BIG = -1e30
LOG2E = 1.4426950408889634
ROW_SLICE = 16

SM_KI = 0
SM_WI = IDX_DIM
SM_IG = SM_WI + IDX_HEADS
SM_LF = SM_IG + M_HEADS


def _cparams(sem):
    return pltpu.CompilerParams(dimension_semantics=sem, vmem_limit_bytes=VMEM_LIMIT)


def _rms(x, g):
    return x * lax.rsqrt(jnp.mean(x * x, axis=-1, keepdims=True) + EPS) * g


def _sigmoid(x):
    return 1.0 / (1.0 + jnp.exp(-x))


def _log_sigmoid(x):
    return jnp.minimum(x, 0.0) - jnp.log1p(jnp.exp(-jnp.abs(x)))


def _sortable(x):
    bits = lax.bitcast_convert_type(x, i32)
    key = bits ^ ((bits >> 31) & INT_MAX)
    return jnp.where(x == 0.0, 0, key)


_G_Q = (0, 512)
_G_K = (512, 640)
_G_V = (640, 768)
_G_QI = (768, 1024)
_G_SM = (1024, 1152)
_G_MQ = (1152, 1408)
_G_MK = (1408, 1664)
_G_MV = (1664, 2176)
_G_MO = (2176, 2688)
_G_GA = (2688, 3712)
_G_GB = (3712, 4736)
_W_COLS = 4736


def _pack_w_in(w_in):
    o = [0]
    for s in (A_WIDTH, 128, 128, 256, 64, 4, 256, 256, 512, 4, 4, 512, 1024, 1024):
        o.append(o[-1] + s)
    aq, ak, av, iq, ik, iw, mq, mk, mv, mi, mf, mo, ga, gb = [w_in[:, o[i]:o[i + 1]] for i in range(14)]
    pad = jnp.zeros((w_in.shape[0], LANES - (IDX_DIM + IDX_HEADS + 2 * M_HEADS)), w_in.dtype)
    small = jnp.concatenate([ik, iw, mi, mf, pad], axis=1)
    w = jnp.concatenate([aq, ak, av, iq, small, mq, mk, mv, mo, ga, gb], axis=1)
    return w.astype(bf16)


def _in_kernel(x_ref, cos_ref, sin_ref, g_ref, w_ref, bias_ref,
               q_ref, k_ref, v_ref, qi_ref, sm_ref, mq_ref, mk_ref, mv_ref, og_ref, ga_ref, gb_ref):
    x = x_ref[...]
    hb = _rms(x, g_ref[...]).astype(bf16)
    cos = cos_ref[...]
    sin = sin_ref[...]
    tm = x.shape[0]
    lane = lax.broadcasted_iota(i32, (tm, LANES), 1)
    first_half = (lane % HEAD_DIM) < (HEAD_DIM // 2)

    def rope(z):
        rot = jnp.where(first_half, pltpu.roll(z, LANES - HEAD_DIM // 2, 1), pltpu.roll(z, HEAD_DIM // 2, 1))
        return z * cos + rot * sin

    def proj(grp):
        return jnp.dot(hb, w_ref[:, grp[0]:grp[1]], preferred_element_type=f32)

    z = proj(_G_Q)
    for j in range(4):
        q_ref[:, j * LANES:(j + 1) * LANES] = rope(z[:, j * LANES:(j + 1) * LANES])
    k_ref[...] = rope(proj(_G_K))
    v_ref[...] = proj(_G_V)
    z = proj(_G_QI)
    for j in range(2):
        qi_ref[:, j * LANES:(j + 1) * LANES] = rope(z[:, j * LANES:(j + 1) * LANES])
    z = proj(_G_SM)
    zb = z + bias_ref[...]
    sm = jnp.where(lane < SM_WI, rope(z),
                   jnp.where(lane < SM_IG, z * (IDX_HEADS ** -0.5 * IDX_DIM ** -0.5),
                             jnp.where(lane < SM_LF, zb,
                                       jnp.where(lane < SM_LF + M_HEADS, _log_sigmoid(zb), 0.0))))
    sm_ref[...] = sm
    mq_ref[...] = proj(_G_MQ)
    mk_ref[...] = proj(_G_MK) * (M_QK ** -0.5)
    mv_ref[...] = proj(_G_MV)
    og_ref[...] = _sigmoid(proj(_G_MO))
    ga_ref[...] = _sigmoid(proj(_G_GA))
    gb_ref[...] = _sigmoid(proj(_G_GB))


def _rope_tables(pos):
    half = HEAD_DIM // 2
    inv = ROPE_THETA ** (-jnp.arange(half, dtype=f32) / half)
    ang = pos.astype(f32)[:, None] * inv[None, :]
    cos = jnp.cos(ang)
    sin = jnp.sin(ang)
    cos128 = jnp.tile(cos, (1, 4))
    sin128 = jnp.tile(jnp.concatenate([-sin, sin], axis=1), (1, 2))
    return cos128, sin128


def in_proj(x2d, cos128, sin128, g_mix, w_packed, bias128):
    n = x2d.shape[0]
    tm = min(256, n)
    assert n % tm == 0
    row = lambda w: pl.BlockSpec((tm, w), lambda i: (i, 0))
    full = lambda a: pl.BlockSpec(a.shape, lambda i: (0,) * a.ndim)
    widths = (512, 128, 128, 256, 128, 256, 256, 512, 512, 1024, 1024)
    return pl.pallas_call(
        _in_kernel,
        grid=(n // tm,),
        in_specs=[row(D_MODEL), row(LANES), row(LANES), full(g_mix), full(w_packed), full(bias128)],
        out_specs=[row(w) for w in widths],
        out_shape=[jax.ShapeDtypeStruct((n, w), f32) for w in widths],
        compiler_params=_cparams(("parallel",)),
        name="in_proj",
    )(x2d, cos128, sin128, g_mix, w_packed, bias128)


def _mlstm_kernel(mq_ref, mk_ref, mv_ref, og_ref, sm_ref, gr_ref, c0_ref, n0_ref, m0_ref, gn_ref,
                  ob_ref, c_ref, n_ref, m_ref):
    ci = pl.program_id(1)
    L = mq_ref.shape[1]

    @pl.when(ci == 0)
    def _():
        c_ref[...] = c0_ref[...]
        n_ref[...] = n0_ref[...]
        m_ref[...] = m0_ref[...]

    row = lax.broadcasted_iota(i32, (L, L), 0)
    col = lax.broadcasted_iota(i32, (L, L), 1)
    tril = row >= col
    for bi, hd in [(bi, hd) for bi in range(mq_ref.shape[0]) for hd in range(M_HEADS)]:
        sm = sm_ref[bi]
        gr = gr_ref[bi]
        q = mq_ref[bi, :, hd * M_QK:(hd + 1) * M_QK]
        k = mk_ref[bi, :, hd * M_QK:(hd + 1) * M_QK]
        v = mv_ref[bi, :, hd * M_V:(hd + 1) * M_V]
        ig_r = gr[hd:hd + 1, :]
        lf_r = gr[M_HEADS + hd:M_HEADS + hd + 1, :]
        ig_c = sm[:, SM_IG + hd:SM_IG + hd + 1]
        lf_c = sm[:, SM_LF + hd:SM_LF + hd + 1]
        C = c_ref[bi, hd]
        nrow = n_ref[bi, hd]
        m_prev = m_ref[bi, hd]
        b_c = jnp.sum(jnp.where(tril, lf_r, 0.0), axis=1, keepdims=True)
        b_r = jnp.sum(jnp.where(tril, 0.0, lf_c) + jnp.where(row == col, lf_c, 0.0), axis=0, keepdims=True)
        dmat = jnp.where(tril, b_c - b_r + ig_r, -jnp.inf)
        inter = b_c + m_prev
        m_t = jnp.maximum(inter, jnp.max(dmat, axis=1, keepdims=True))
        qb = q.astype(bf16)
        kb = k.astype(bf16)
        qk = lax.dot_general(qb, kb, (((1,), (1,)), ((), ())), preferred_element_type=f32)
        s = qk * jnp.exp(dmat - m_t)
        w_inter = jnp.exp(inter - m_t)
        qc = lax.dot_general(qb, C.astype(bf16), (((1,), (1,)), ((), ())), preferred_element_type=f32)
        num = jnp.dot(s.astype(bf16), v.astype(bf16), preferred_element_type=f32) + w_inter * qc
        den = jnp.sum(s, axis=1, keepdims=True) + w_inter * jnp.sum(q * nrow, axis=1, keepdims=True)
        h = num / jnp.maximum(jnp.abs(den), jnp.exp(-m_t))
        b_last = b_c[L - 1:L, :]
        g_c = b_last - b_c + ig_c
        m_new = jnp.maximum(b_last + m_prev, jnp.max(g_c, axis=0, keepdims=True))
        w_k = jnp.exp(g_c - m_new)
        decay = jnp.exp(b_last + m_prev - m_new)
        wv = (w_k * v).astype(bf16)
        c_ref[bi, hd] = decay * C + lax.dot_general(wv, kb, (((0,), (0,)), ((), ())), preferred_element_type=f32)
        n_ref[bi, hd] = decay * nrow + jnp.sum(w_k * k, axis=0, keepdims=True)
        m_ref[bi, hd] = m_new
        gn = gn_ref[:, hd * M_V:(hd + 1) * M_V]
        ob_ref[bi, :, hd * M_V:(hd + 1) * M_V] = og_ref[bi, :, hd * M_V:(hd + 1) * M_V] * _rms(h, gn)


def mlstm(mq, mk, mv, og, sm, grow, c0, n0, m0, g_mnorm):
    b, t = mq.shape[:2]
    L = M_CHUNK
    bb = MLSTM_BATCH if b % MLSTM_BATCH == 0 else 1
    assert t % L == 0
    tok = lambda w: pl.BlockSpec((bb, L, w), lambda bi, ci: (bi, ci, 0))
    st = lambda a: pl.BlockSpec((bb,) + a.shape[1:], lambda bi, ci: (bi,) + (0,) * (a.ndim - 1))
    return pl.pallas_call(
        _mlstm_kernel,
        grid=(b // bb, t // L),
        in_specs=[tok(256), tok(256), tok(512), tok(512), tok(LANES),
                  pl.BlockSpec((bb, 2 * M_HEADS, L), lambda bi, ci: (bi, 0, ci)),
                  st(c0), st(n0), st(m0), pl.BlockSpec(g_mnorm.shape, lambda bi, ci: (0, 0))],
        out_specs=[tok(512), st(c0), st(n0), st(m0)],
        out_shape=[jax.ShapeDtypeStruct((b, t, M_WIDTH), f32), jax.ShapeDtypeStruct(c0.shape, f32),
                   jax.ShapeDtypeStruct(n0.shape, f32), jax.ShapeDtypeStruct(m0.shape, f32)],
        compiler_params=_cparams(("parallel", "arbitrary")),
        name="mlstm",
    )(mq, mk, mv, og, sm, grow, c0, n0, m0, g_mnorm)


def mlstm_branch(mq, mk, mv, og, sm, c0, n0, m0, g_mnorm):
    b, t = mq.shape[:2]
    tp = -(-t // M_CHUNK) * M_CHUNK
    if tp != t:
        pad = lambda a: jnp.pad(a, ((0, 0), (0, tp - t), (0, 0)))
        mq, mk, mv, og = pad(mq), pad(mk), pad(mv), pad(og)
        sm_pad = jnp.zeros((b, tp - t, LANES), f32).at[:, :, SM_IG:SM_IG + M_HEADS].set(NEG_BIG)
        sm = jnp.concatenate([sm, sm_pad], axis=1)
    grow = sm[:, :, SM_IG:SM_IG + 2 * M_HEADS].transpose(0, 2, 1)
    ob, c, n, m = mlstm(mq, mk, mv, og, sm, grow, c0.astype(f32), n0.astype(f32).reshape(b, M_HEADS, 1, M_QK),
                        m0.astype(f32).reshape(b, M_HEADS, 1, 1), g_mnorm.reshape(1, M_WIDTH))
    return ob[:, :t], c, n.reshape(b, M_HEADS, M_QK), m.reshape(b, M_HEADS)


def _select_threshold(get_chunk, n_chunks, chunk_w, rows, n_sel, dynamic):
    def count(pred):
        def body(c, acc):
            hit = jnp.where(pred(get_chunk(c)), 1.0, 0.0)
            for j in range(chunk_w // LANES):
                acc = acc + hit[:, j * LANES:(j + 1) * LANES]
            return acc
        acc0 = jnp.zeros((rows, LANES), f32)
        if dynamic:
            acc = lax.fori_loop(0, n_chunks, body, acc0)
        else:
            acc = acc0
            for c in range(n_chunks):
                acc = body(c, acc)
        return jnp.sum(acc, axis=1, keepdims=True)

    kf = float(n_sel)

    total = jnp.zeros((rows, 1), f32) + (n_chunks * chunk_w).astype(f32) if dynamic else \
        jnp.full((rows, 1), float(n_chunks * chunk_w), f32)
    c_nonneg = count(lambda keys: keys >= 0)
    c_pos = count(lambda keys: keys >= 1)
    take0 = c_nonneg >= kf
    zero_tie = take0 & (c_pos < kf)

    def unsettled(st):
        it, _, cnt = st
        return (it < 32) & (jnp.max(jnp.where(zero_tie, 0.0, jnp.abs(cnt - kf))) > 0.0)

    def bit_step(st):
        it, thr_u, cnt = st
        cand_u = thr_u | lax.shift_left(jnp.int32(1), 31 - it)
        cand_s = cand_u ^ INT_MIN
        c = count(lambda keys: keys >= cand_s)
        take = c >= kf
        return it + 1, jnp.where(take, cand_u, thr_u), jnp.where(take, c, cnt)

    state0 = (jnp.int32(1), jnp.where(take0, INT_MIN, 0).astype(i32), jnp.where(take0, c_nonneg, total))
    _, thr_u, _ = lax.while_loop(unsettled, lambda st: bit_step(bit_step(st)), bit_step(state0))
    thr = jnp.maximum(thr_u ^ INT_MIN, INT_MIN + 1)
    need = kf - count(lambda keys: keys > thr)
    return thr, need


def _tie_prefix_matrix():
    r = lax.broadcasted_iota(i32, (LANES, LANES), 0)
    c = lax.broadcasted_iota(i32, (LANES, LANES), 1)
    return jnp.where(r <= c, 1.0, 0.0).astype(bf16)


def _selected_blocks(key_blocks, thr, need, seen, tri):
    rows = key_blocks[0].shape[0]
    eqs = [kb == thr for kb in key_blocks]
    stack = jnp.concatenate([jnp.where(eq, 1.0, 0.0) for eq in eqs], axis=0).astype(bf16)
    pre = jnp.dot(stack, tri, preferred_element_type=f32)
    sels = []
    for j, (kb, eq) in enumerate(zip(key_blocks, eqs)):
        pj = pre[j * rows:(j + 1) * rows]
        sels.append((kb > thr) | (eq & ((seen + pj) <= need)))
        seen = seen + pj[:, LANES - 1:LANES]
    return sels, seen


def _dsa_prompt_kernel(q_ref, qi_ref, sm_ref, kit_ref, kt_ref, vd_ref, o_ref,
                       keys_ref, bias_ref, lg_ref, p_ref, m_ref, acc_ref, *, n_sel):
    qb = pl.program_id(1)
    tq = q_ref.shape[1]
    kc = kit_ref.shape[3]
    n_chunks = (qb * tq + tq - 1) // kc + 1
    lane = lax.broadcasted_iota(i32, (tq, LANES), 1)
    lo = lane < HEAD_DIM
    t_col = qb * tq + lax.broadcasted_iota(i32, (tq, 1), 0)
    sm = sm_ref[0]

    qi = qi_ref[0]
    qi_h = []
    for h in range(IDX_HEADS):
        blk = qi[:, (h // 2) * LANES:(h // 2 + 1) * LANES]
        qi_h.append(jnp.where(lo if h % 2 == 0 else ~lo, blk, 0.0).astype(bf16))
    w_h = [sm[:, SM_WI + h:SM_WI + h + 1] for h in range(IDX_HEADS)]

    def score_chunk(c, carry):
        kt = kit_ref[0, c]
        sc = jnp.zeros((tq, kc), f32)
        for h in range(IDX_HEADS):
            s = jnp.dot(qi_h[h], kt, preferred_element_type=f32)
            sc = sc + w_h[h] * jnp.maximum(s, 0.0)
        idx = c * kc + lax.broadcasted_iota(i32, (tq, kc), 1)
        keys_ref[c] = jnp.where(idx <= t_col, _sortable(sc), INT_MIN)
        return carry

    lax.fori_loop(0, n_chunks, score_chunk, 0)

    thr, need = _select_threshold(lambda c: keys_ref[c], n_chunks, kc, tq, n_sel, True)
    tri = _tie_prefix_matrix()

    q = q_ref[0] * (HEAD_DIM ** -0.5 * LOG2E)
    q_g = []
    for g in range(N_KV_HEADS):
        parts = []
        for j in range(GROUP):
            h = g * GROUP + j
            blk = q[:, (h // 2) * LANES:(h // 2 + 1) * LANES]
            parts.append(jnp.where(lo if h % 2 == 0 else ~lo, blk, 0.0).astype(bf16))
        q_g.append(jnp.concatenate(parts, axis=0))
    m_ref[...] = jnp.full(m_ref.shape, NEG_BIG, f32)
    acc_ref[...] = jnp.zeros(acc_ref.shape, f32)
    rs = min(ROW_SLICE, tq)

    def attend_chunk(c, seen):
        blocks = [keys_ref[c, :, j * LANES:(j + 1) * LANES] for j in range(kc // LANES)]
        sels, seen = _selected_blocks(blocks, thr, need, seen, tri)
        for j, sel in enumerate(sels):
            bias_ref[:, j * LANES:(j + 1) * LANES] = jnp.where(sel, 0.0, NEG_BIG)
        for g in range(N_KV_HEADS):
            lg_ref[...] = jnp.dot(q_g[g], kt_ref[0, g, c], preferred_element_type=f32)
            for r0 in range(0, GROUP * tq, rs):
                x = lg_ref[r0:r0 + rs] + bias_ref[r0 % tq:r0 % tq + rs]
                m_old = m_ref[g, r0:r0 + rs]
                m_new = jnp.maximum(m_old, jnp.max(x, axis=-1, keepdims=True))
                p_ref[r0:r0 + rs] = jnp.exp2(x - m_new).astype(bf16)
                acc_ref[g, r0:r0 + rs] = jnp.exp2(m_old - m_new) * acc_ref[g, r0:r0 + rs]
                m_ref[g, r0:r0 + rs] = m_new
            acc_ref[g] += jnp.dot(p_ref[...], vd_ref[0, g, c], preferred_element_type=f32)
        return seen

    lax.fori_loop(0, n_chunks, attend_chunk, jnp.zeros((tq, 1), f32))

    for g in range(N_KV_HEADS):
        acc = acc_ref[g]
        out = acc / acc[:, HEAD_DIM:HEAD_DIM + 1]
        for jp in range(GROUP // 2):
            even = out[(2 * jp) * tq:(2 * jp + 1) * tq]
            odd = pltpu.roll(out[(2 * jp + 1) * tq:(2 * jp + 2) * tq], HEAD_DIM, 1)
            o_ref[0, :, (g * 2 + jp) * LANES:(g * 2 + jp + 1) * LANES] = jnp.where(lo, even, odd)


def dsa_prompt(q, qi, sm, kit2, kt2, vd):
    b, t = q.shape[:2]
    nc, kc = kit2.shape[1], kit2.shape[3]
    tq = min(TQ, t)
    n_sel = min(TOPK_MAX, t // 4)
    tok = lambda w: pl.BlockSpec((1, tq, w), lambda bi, qb: (bi, qb, 0))
    return pl.pallas_call(
        functools.partial(_dsa_prompt_kernel, n_sel=n_sel),
        grid=(b, t // tq),
        in_specs=[tok(A_WIDTH), tok(IDX_HEADS * IDX_DIM), tok(LANES),
                  pl.BlockSpec((1, nc, LANES, kc), lambda bi, qb: (bi, 0, 0, 0)),
                  pl.BlockSpec((1, N_KV_HEADS, nc, LANES, kc), lambda bi, qb: (bi, 0, 0, 0, 0)),
                  pl.BlockSpec((1, N_KV_HEADS, nc, kc, LANES), lambda bi, qb: (bi, 0, 0, 0, 0))],
        out_specs=tok(A_WIDTH),
        out_shape=jax.ShapeDtypeStruct((b, t, A_WIDTH), f32),
        scratch_shapes=[pltpu.VMEM((nc, tq, kc), i32),
                        pltpu.VMEM((tq, kc), f32),
                        pltpu.VMEM((GROUP * tq, kc), f32),
                        pltpu.VMEM((GROUP * tq, kc), bf16),
                        pltpu.VMEM((N_KV_HEADS, GROUP * tq, 1), f32),
                        pltpu.VMEM((N_KV_HEADS, GROUP * tq, LANES), f32)],
        compiler_params=_cparams(("parallel", "arbitrary")),
        name="dsa_prompt",
    )(q, qi, sm, kit2, kt2, vd)


def dsa_prompt_branch(q, qi, sm, k, v):
    b, t = q.shape[:2]
    kc = min(KC, t)
    nc = t // kc
    kit = sm[:, :, :IDX_DIM].astype(bf16).reshape(b, nc, kc, IDX_DIM).transpose(0, 1, 3, 2)
    kit2 = jnp.concatenate([kit, kit], axis=2)
    kt = k.astype(bf16).reshape(b, nc, kc, N_KV_HEADS, HEAD_DIM).transpose(0, 3, 1, 4, 2)
    kt2 = jnp.concatenate([kt, kt], axis=3)
    vb = v.astype(bf16).reshape(b, nc, kc, N_KV_HEADS, HEAD_DIM).transpose(0, 3, 1, 2, 4)
    vd = jnp.concatenate([vb, jnp.ones_like(vb)], axis=4)
    return dsa_prompt(q, qi, sm, kit2, kt2, vd)


def _idx_scores(qi, w, ktpage):
    s = jnp.dot(qi, ktpage.astype(bf16), preferred_element_type=f32)
    r = (w * jnp.maximum(s, 0.0)).reshape(IDX_HEADS, SUBLANES, s.shape[1])
    sc = r[0]
    for h in range(1, IDX_HEADS):
        sc = sc + r[h]
    return sc


def _page_fetcher(pt_ref, srcs, bufs, sem, g):
    def start(step, slot):
        for j in range(g):
            pg = pt_ref[step * g + j]
            for i, (src, buf) in enumerate(zip(srcs, bufs)):
                pltpu.make_async_copy(src.at[pg], buf.at[slot, j], sem.at[i, slot]).start()

    def wait(slot):
        for i, buf in enumerate(bufs):
            pltpu.make_async_copy(buf.at[slot], buf.at[slot], sem.at[i, slot]).wait()

    return start, wait


def _run_paged(t, n_tot, active, start, wait, nbuf, body):
    @pl.when(active & (t == 0))
    def _():
        for d in range(nbuf - 1):
            pl.when(d < n_tot)(functools.partial(start, d, d))

    def one(slot):
        wait(slot)
        pl.when(t + nbuf - 1 < n_tot)(functools.partial(start, t + nbuf - 1, (slot + nbuf - 1) % nbuf))
        body(slot)

    for slot in range(nbuf):
        pl.when(active & (t % nbuf == slot))(functools.partial(one, slot))


def _ds_score_kernel(pt_ref, qi_ref, w_ref, kin_ref, cki_hbm, keys_ref, kbuf, sem, *, g):
    bi = pl.program_id(0)
    s = pl.program_id(1)
    last = pl.num_programs(1) - 1
    t = bi * last + s
    n_tot = pl.num_programs(0) * last
    qi = qi_ref[0]
    w = w_ref[0]
    start, wait = _page_fetcher(pt_ref, [cki_hbm], [kbuf], sem, g)

    def pages(slot):
        kcat = jnp.concatenate([kbuf[slot, j].astype(bf16) for j in range(g)], axis=1)
        keys_ref[0] = _sortable(_idx_scores(qi, w, kcat))

    _run_paged(t, n_tot, s < last, start, wait, kbuf.shape[0], pages)

    @pl.when(s == last)
    def _():
        keys_ref[0] = jnp.full(keys_ref.shape[1:], INT_MIN, i32)
        sc = _idx_scores(qi, w, kin_ref[0])
        tok = lax.broadcasted_iota(i32, sc.shape, 0)
        j = lax.broadcasted_iota(i32, sc.shape, 1)
        keys_ref[0, :, 0:PAGE_SIZE] = jnp.where(j <= tok, _sortable(sc), INT_MIN)


def _ds_thr_kernel(keys_ref, thr_ref, need_ref, *, n_sel, chunk_w):
    rows, width = keys_ref.shape

    def get_chunk(c):
        return keys_ref[:, c * chunk_w:(c + 1) * chunk_w]

    thr, need = _select_threshold(get_chunk, width // chunk_w, chunk_w, rows, n_sel, False)
    thr_ref[...] = jnp.broadcast_to(thr, thr_ref.shape)
    need_ref[...] = jnp.broadcast_to(need, need_ref.shape)


def _ds_attn_kernel(pt_ref, q_ref, keys_ref, thr_ref, need_ref, knew_ref, vnew_ref, ck_hbm, cv_hbm,
                    o_ref, m_ref, l_ref, acc_ref, seen_ref, kbuf, vbuf, sem, *, g):
    bi = pl.program_id(0)
    s = pl.program_id(1)
    last = pl.num_programs(1) - 1
    t = bi * last + s
    n_tot = pl.num_programs(0) * last
    start, wait = _page_fetcher(pt_ref, [ck_hbm, cv_hbm], [kbuf, vbuf], sem, g)
    q = q_ref[0]
    thr = thr_ref[0][:, 0:1]
    need = need_ref[0][:, 0:1]
    tri = _tie_prefix_matrix()

    def process(keys, kps, vps):
        n = len(kps)
        seen = seen_ref[...]
        sels, seen = _selected_blocks([keys[:, j * PAGE_SIZE:(j + 1) * PAGE_SIZE] for j in range(n)],
                                      thr, need, seen, tri)
        bias = [jnp.where(sel, 0.0, NEG_BIG) for sel in sels]
        seen_ref[...] = seen
        bias = jnp.concatenate(bias, axis=1) if n > 1 else bias[0]
        kcat = jnp.concatenate([kp.astype(bf16) for kp in kps], axis=1) if n > 1 else kps[0].astype(bf16)
        vcat = jnp.concatenate([vp.astype(bf16) for vp in vps], axis=1) if n > 1 else vps[0].astype(bf16)
        lg = jnp.dot(q, kcat, preferred_element_type=f32)
        lg = lg.reshape(N_HEADS, SUBLANES, n * PAGE_SIZE) + bias[None]
        m_old = m_ref[...]
        m_new = jnp.maximum(m_old, jnp.max(lg, axis=-1, keepdims=True))
        p = jnp.exp(lg - m_new)
        alpha = jnp.exp(m_old - m_new)
        l_ref[...] = alpha * l_ref[...] + jnp.sum(p, axis=-1, keepdims=True)
        pb = p.reshape(N_HEADS * SUBLANES, n * PAGE_SIZE).astype(bf16)
        pv = lax.dot_general(pb, vcat, (((1,), (1,)), ((), ())), preferred_element_type=f32)
        acc_ref[...] = alpha * acc_ref[...] + pv.reshape(N_HEADS, SUBLANES, LANES)
        m_ref[...] = m_new

    @pl.when(s == 0)
    def _():
        m_ref[...] = jnp.full(m_ref.shape, NEG_BIG, f32)
        l_ref[...] = jnp.zeros(l_ref.shape, f32)
        acc_ref[...] = jnp.zeros(acc_ref.shape, f32)
        seen_ref[...] = jnp.zeros(seen_ref.shape, f32)

    def pages(slot):
        process(keys_ref[0], [kbuf[slot, j] for j in range(g)], [vbuf[slot, j] for j in range(g)])

    _run_paged(t, n_tot, s < last, start, wait, kbuf.shape[0], pages)

    @pl.when(s == last)
    def _():
        process(keys_ref[0, :, 0:PAGE_SIZE], [knew_ref[0]], [vnew_ref[0]])
        o_ref[0] = (acc_ref[...] / l_ref[...]).reshape(N_HEADS * SUBLANES, LANES)


def dsa_sample(q, qi, sm, k, v, cache_k, cache_v, cache_idx_k, page_table):
    b, t = q.shape[:2]
    assert t <= SUBLANES
    n_pages = page_table.shape[1]
    g = min(PAGES_PER_STEP, n_pages)
    assert n_pages % g == 0
    ns = n_pages // g
    past = n_pages * PAGE_SIZE
    n_sel = min(TOPK_MAX, (past + t) // 4)
    pt = page_table.reshape(-1).astype(i32)
    padt = lambda a: jnp.pad(a, ((0, 0), (0, SUBLANES - t)) + ((0, 0),) * (a.ndim - 2))

    qi_r = padt(qi.reshape(b, t, IDX_HEADS, IDX_DIM)).transpose(0, 2, 1, 3).reshape(b, IDX_HEADS * SUBLANES, IDX_DIM)
    w_r = padt(sm[:, :, SM_WI:SM_WI + IDX_HEADS]).transpose(0, 2, 1).reshape(b, IDX_HEADS * SUBLANES, 1)
    padk = lambda a: jnp.pad(a, ((0, 0), (0, PAGE_SIZE - t), (0, 0))).transpose(0, 2, 1)
    ki_new = padk(sm[:, :, :IDX_DIM])
    k_new = padk(k)
    v_new = padk(v)
    qh = padt(q.reshape(b, t, N_HEADS, HEAD_DIM)).transpose(0, 2, 1, 3) * (HEAD_DIM ** -0.5)
    grp = (jnp.arange(N_HEADS) // GROUP)[None, :, None, None]
    q_r = jnp.concatenate([jnp.where(grp == 0, qh, 0.0), jnp.where(grp == 1, qh, 0.0)], axis=-1)
    q_r = q_r.reshape(b, N_HEADS * SUBLANES, LANES).astype(bf16)
    n_pool = cache_k.shape[0]
    ck = cache_k.transpose(0, 2, 3, 1).reshape(n_pool, N_KV_HEADS * HEAD_DIM, PAGE_SIZE)
    cv = cache_v.transpose(0, 2, 3, 1).reshape(n_pool, N_KV_HEADS * HEAD_DIM, PAGE_SIZE)
    cki = cache_idx_k.transpose(0, 2, 1)

    hbm = pl.BlockSpec(memory_space=pl.ANY)
    per_b = lambda a: pl.BlockSpec((1,) + a.shape[1:], lambda bi, s, ptr: (bi,) + (0,) * (a.ndim - 1))
    blk_w = g * PAGE_SIZE
    keys_spec = pl.BlockSpec((1, SUBLANES, blk_w), lambda bi, s, ptr: (bi, 0, s))
    width = past + blk_w

    keys = pl.pallas_call(
        functools.partial(_ds_score_kernel, g=g),
        grid_spec=pltpu.PrefetchScalarGridSpec(
            num_scalar_prefetch=1, grid=(b, ns + 1),
            in_specs=[per_b(qi_r), per_b(w_r), per_b(ki_new), hbm],
            out_specs=keys_spec,
            scratch_shapes=[pltpu.VMEM((PAGE_SLOTS, g, IDX_DIM, PAGE_SIZE), f32),
                            pltpu.SemaphoreType.DMA((1, PAGE_SLOTS))]),
        out_shape=jax.ShapeDtypeStruct((b, SUBLANES, width), i32),
        compiler_params=_cparams(("arbitrary", "arbitrary")),
        name="dsa_sample_scores",
    )(pt, qi_r.astype(bf16), w_r, ki_new, cki)

    rows = b * SUBLANES
    thr, need = pl.pallas_call(
        functools.partial(_ds_thr_kernel, n_sel=n_sel, chunk_w=blk_w),
        out_shape=[jax.ShapeDtypeStruct((rows, LANES), i32), jax.ShapeDtypeStruct((rows, LANES), f32)],
        compiler_params=pltpu.CompilerParams(vmem_limit_bytes=VMEM_LIMIT),
        name="dsa_sample_threshold",
    )(keys.reshape(rows, width))
    thr = thr.reshape(b, SUBLANES, LANES)
    need = need.reshape(b, SUBLANES, LANES)

    out = pl.pallas_call(
        functools.partial(_ds_attn_kernel, g=g),
        grid_spec=pltpu.PrefetchScalarGridSpec(
            num_scalar_prefetch=1, grid=(b, ns + 1),
            in_specs=[per_b(q_r), keys_spec, per_b(thr), per_b(need), per_b(k_new), per_b(v_new), hbm, hbm],
            out_specs=per_b(q_r),
            scratch_shapes=[pltpu.VMEM((N_HEADS, SUBLANES, 1), f32), pltpu.VMEM((N_HEADS, SUBLANES, 1), f32),
                            pltpu.VMEM((N_HEADS, SUBLANES, LANES), f32), pltpu.VMEM((SUBLANES, 1), f32),
                            pltpu.VMEM((PAGE_SLOTS, g, LANES, PAGE_SIZE), f32),
                            pltpu.VMEM((PAGE_SLOTS, g, LANES, PAGE_SIZE), f32),
                            pltpu.SemaphoreType.DMA((2, PAGE_SLOTS))]),
        out_shape=jax.ShapeDtypeStruct((b, N_HEADS * SUBLANES, LANES), f32),
        compiler_params=_cparams(("arbitrary", "arbitrary")),
        name="dsa_sample_attention",
    )(pt, q_r, keys, thr, need, k_new, v_new, ck, cv)
    out = out.reshape(b, N_HEADS, SUBLANES, N_KV_HEADS, HEAD_DIM)[:, :, :t]
    out = jnp.concatenate([out[:, :GROUP, :, 0], out[:, GROUP:, :, 1]], axis=1)
    return out.transpose(0, 2, 1, 3).reshape(b, t, A_WIDTH)


def _mid_kernel(x_ref, oa_ref, ob_ref, ga_ref, gb_ref, wa_ref, wb_ref, wo_ref, gf_ref, wrh_ref, wrl_ref, br_ref,
                x1_ref, te_ref, tg_ref):
    a = jnp.dot(oa_ref[...].astype(bf16), wa_ref[...], preferred_element_type=f32)
    b = jnp.dot(ob_ref[...].astype(bf16), wb_ref[...], preferred_element_type=f32)
    merged = ga_ref[...] * a + gb_ref[...] * b
    x1 = x_ref[...] + jnp.dot(merged.astype(bf16), wo_ref[...], preferred_element_type=f32)
    x1_ref[...] = x1
    h2 = _rms(x1, gf_ref[...])
    hi = h2.astype(bf16)
    lo = (h2 - hi.astype(f32)).astype(bf16)
    lg = (jnp.dot(hi, wrh_ref[...], preferred_element_type=f32) + jnp.dot(lo, wrh_ref[...], preferred_element_type=f32)
          + jnp.dot(hi, wrl_ref[...], preferred_element_type=f32)) + br_ref[...]
    lane = lax.broadcasted_iota(i32, lg.shape, 1)
    lane_f = lane.astype(f32)
    vals, ids = [], []
    for _ in range(TOP_K):
        m = jnp.max(lg, axis=1, keepdims=True)
        idx = jnp.min(jnp.where(lg == m, lane_f, float(LANES)), axis=1, keepdims=True).astype(i32)
        vals.append(m)
        ids.append(idx)
        lg = jnp.where(lane == idx, -jnp.inf, lg)
    ex = [jnp.exp(v - vals[0]) for v in vals]
    tot = ex[0] + ex[1] + ex[2] + ex[3]
    te = jnp.zeros(lg.shape, i32)
    tg = jnp.zeros(lg.shape, f32)
    for j in range(TOP_K):
        te = jnp.where(lane == j, ids[j], te)
        tg = jnp.where(lane == j, ex[j] / tot, tg)
    te_ref[...] = te
    tg_ref[...] = tg


def mid(x, oa, ob, ga, gb, wa, wb, wo, g_ffn, wr_hi, wr_lo, br):
    n = x.shape[0]
    tm = min(256, n)
    assert n % tm == 0
    row = lambda w: pl.BlockSpec((tm, w), lambda i: (i, 0))
    full = lambda a: pl.BlockSpec(a.shape, lambda i: (0,) * a.ndim)
    return pl.pallas_call(
        _mid_kernel,
        grid=(n // tm,),
        in_specs=[row(D_MODEL), row(A_WIDTH), row(M_WIDTH), row(D_MODEL), row(D_MODEL),
                  full(wa), full(wb), full(wo), full(g_ffn), full(wr_hi), full(wr_lo), full(br)],
        out_specs=[row(D_MODEL), row(LANES), row(LANES)],
        out_shape=[jax.ShapeDtypeStruct((n, D_MODEL), f32),
                   jax.ShapeDtypeStruct((n, LANES), i32), jax.ShapeDtypeStruct((n, LANES), f32)],
        compiler_params=_cparams(("parallel",)),
        name="mid",
    )(x, oa, ob, ga, gb, wa, wb, wo, g_ffn, wr_hi, wr_lo, br)


def _rank_kernel(te_ref, dest_ref, meta_ref, einfo_ref, cnt_ref, carry_ref, *, blk):
    ph = pl.program_id(0)
    i = pl.program_id(1)
    tm = te_ref.shape[0]
    lane = lax.broadcasted_iota(i32, (tm, LANES), 1)
    te = te_ref[...]
    oh = jnp.zeros((tm, LANES), f32)
    for j in range(TOP_K):
        oh = oh + jnp.where(lane == te[:, j:j + 1], 1.0, 0.0)
    tile_cnt = jnp.sum(oh, axis=0, keepdims=True)

    @pl.when((ph == 0) & (i == 0))
    def _():
        cnt_ref[...] = jnp.zeros(cnt_ref.shape, f32)

    @pl.when(ph == 0)
    def _():
        cnt_ref[...] += tile_cnt

    @pl.when((ph == 1) & (i == 0))
    def _():
        cnt = cnt_ref[...]
        padded = jnp.floor((cnt + (blk - 1)) / blk) * blk
        r = lax.broadcasted_iota(i32, (LANES, LANES), 0)
        c = lax.broadcasted_iota(i32, (LANES, LANES), 1)
        col = jnp.sum(jnp.where(r == c, padded, 0.0), axis=1, keepdims=True)
        start = jnp.sum(jnp.where(r < c, col, 0.0), axis=0, keepdims=True)
        carry_ref[...] = start
        pad_end = start + padded
        row8 = lax.broadcasted_iota(i32, einfo_ref.shape, 0)
        einfo_ref[...] = jnp.where(row8 == 0, start + cnt, jnp.where(row8 == 1, pad_end, 0.0)).astype(i32)
        nbp = meta_ref.shape[0]
        jb = (lax.broadcasted_iota(i32, (nbp, LANES), 0) * blk).astype(f32)
        lane2 = lax.broadcasted_iota(i32, (nbp, LANES), 1)
        be = jnp.sum(jnp.where((pad_end <= jb) & (lane2 < N_EXPERTS), 1.0, 0.0), axis=1, keepdims=True)
        be = jnp.minimum(be, float(N_EXPERTS - 1))
        n_act = jnp.sum(jnp.where(lane2 == N_EXPERTS - 1, pad_end, 0.0), axis=1, keepdims=True) / blk
        meta_ref[...] = jnp.where(lane2 == 0, be, jnp.where(lane2 == 1, n_act, 0.0)).astype(i32)

    @pl.when(ph == 1)
    def _():
        r = lax.broadcasted_iota(i32, (tm, tm), 0)
        c = lax.broadcasted_iota(i32, (tm, tm), 1)
        before = jnp.where(c < r, 1.0, 0.0).astype(bf16)
        pos = carry_ref[...] + jnp.dot(before, oh.astype(bf16), preferred_element_type=f32)
        d = jnp.zeros((tm, LANES), f32)
        for j in range(TOP_K):
            dj = jnp.sum(jnp.where(lane == te[:, j:j + 1], pos, 0.0), axis=1, keepdims=True)
            d = jnp.where(lane == j, dj, d)
        dest_ref[...] = d.astype(i32)
        carry_ref[...] += tile_cnt


def rank(te, blk, n_blocks):
    n = te.shape[0]
    tm = next(c for c in (512, 384, 256, 128, 64, 32, 16, SUBLANES) if n % c == 0)
    nbp = -(-n_blocks // SUBLANES) * SUBLANES
    return pl.pallas_call(
        functools.partial(_rank_kernel, blk=blk),
        grid=(2, n // tm),
        in_specs=[pl.BlockSpec((tm, LANES), lambda ph, i: (i, 0))],
        out_specs=[pl.BlockSpec((tm, LANES), lambda ph, i: (i * ph, 0)),
                   pl.BlockSpec((nbp, LANES), lambda ph, i: (0, 0)),
                   pl.BlockSpec((SUBLANES, LANES), lambda ph, i: (0, 0))],
        out_shape=[jax.ShapeDtypeStruct((n, LANES), i32), jax.ShapeDtypeStruct((nbp, LANES), i32),
                   jax.ShapeDtypeStruct((SUBLANES, LANES), i32)],
        scratch_shapes=[pltpu.VMEM((1, LANES), f32), pltpu.VMEM((1, LANES), f32)],
        compiler_params=_cparams(("arbitrary", "arbitrary")),
        name="rank",
    )(te)


def _ffn_kernel(be_ref, nact_ref, rt_ref, x_hbm, *rest):
    wgu_refs = rest[:W_SPLIT]
    bgu_ref = rest[W_SPLIT]
    wdn_refs = rest[W_SPLIT + 1:2 * W_SPLIT + 1]
    bdn_ref, y_ref, xbuf, sem, wgu_bf, wdn_bf = rest[2 * W_SPLIT + 1:]
    i = pl.program_id(0)
    n_act = nact_ref[0]
    blk = xbuf.shape[1]

    def row_copy(b, r, slot):
        tok = rt_ref[b * blk + r]
        return pltpu.make_async_copy(x_hbm.at[pl.ds(tok, 1)], xbuf.at[slot, pl.ds(r, 1)], sem.at[slot])

    def wait_slot(slot):
        pltpu.make_async_copy(xbuf.at[slot], xbuf.at[slot], sem.at[slot]).wait()

    @pl.when(i == 0)
    def _():
        def body(r, c):
            row_copy(0, r, 0).start()
            return c
        lax.fori_loop(0, blk, body, 0)

    @pl.when(i == n_act)
    def _():
        wait_slot(i % 2)

    @pl.when(i >= n_act)
    def _():
        y_ref[...] = jnp.zeros(y_ref.shape, f32)

    @pl.when((i < n_act) & ((i == 0) | (be_ref[i] != be_ref[jnp.maximum(i - 1, 0)])))
    def _():
        wg = wgu_bf.shape[1] // W_SPLIT
        wd = wdn_bf.shape[1] // W_SPLIT
        for c in range(W_SPLIT):
            wgu_bf[:, c * wg:(c + 1) * wg] = wgu_refs[c][0].astype(bf16)
            wdn_bf[:, c * wd:(c + 1) * wd] = wdn_refs[c][0].astype(bf16)

    def block(slot):
        wait_slot(slot)
        for r in range(blk):
            row_copy(i + 1, r, 1 - slot).start()
        x = xbuf[slot].astype(bf16)
        gu = jnp.dot(x, wgu_bf[...], preferred_element_type=f32) + bgu_ref[0]
        gate = jnp.minimum(gu[:, :EXPERT_FF], SWIGLU_LIMIT)
        up = jnp.clip(gu[:, EXPERT_FF:], -SWIGLU_LIMIT, SWIGLU_LIMIT)
        act = (up + 1.0) * (gate * _sigmoid(SWIGLU_ALPHA * gate))
        y_ref[...] = jnp.dot(act.astype(bf16), wdn_bf[...], preferred_element_type=f32) + bdn_ref[0]

    for slot in range(2):
        pl.when((i < n_act) & (i % 2 == slot))(functools.partial(block, slot))


def ffn(blk_e, n_act, row_tok, h2, w_gu, b_gu, w_dn, b_dn, blk, n_blocks):
    d = h2.shape[1]
    ff2 = w_gu.shape[2]
    return pl.pallas_call(
        _ffn_kernel,
        grid_spec=pltpu.PrefetchScalarGridSpec(
            num_scalar_prefetch=3, grid=(n_blocks,),
            in_specs=[pl.BlockSpec(memory_space=pl.ANY)]
            + [pl.BlockSpec((1, d, ff2 // W_SPLIT), functools.partial(lambda i, be, na, rt, c: (be[i], 0, c), c=c))
               for c in range(W_SPLIT)]
            + [pl.BlockSpec((1, 1, ff2), lambda i, be, na, rt: (be[i], 0, 0))]
            + [pl.BlockSpec((1, ff2 // 2, d // W_SPLIT), functools.partial(lambda i, be, na, rt, c: (be[i], 0, c), c=c))
               for c in range(W_SPLIT)]
            + [pl.BlockSpec((1, 1, d), lambda i, be, na, rt: (be[i], 0, 0))],
            out_specs=pl.BlockSpec((blk, d), lambda i, be, na, rt: (i, 0)),
            scratch_shapes=[pltpu.VMEM((2, blk, d), f32), pltpu.SemaphoreType.DMA((2,)),
                            pltpu.VMEM((d, ff2), bf16), pltpu.VMEM((ff2 // 2, d), bf16)]),
        out_shape=jax.ShapeDtypeStruct((n_blocks * blk, d), f32),
        compiler_params=_cparams(("arbitrary",)),
        name="ffn",
    )(blk_e, n_act, row_tok, h2, *([w_gu] * W_SPLIT), b_gu.reshape(b_gu.shape[0], 1, ff2),
      *([w_dn] * W_SPLIT), b_dn.reshape(b_dn.shape[0], 1, d))


def _dispatch_kernel(dest_ref, einfo_ref, x1_ref, g_ref, xs_hbm, hbuf, zrow, zblk, sem, zsem):
    i = pl.program_id(0)
    n = pl.num_programs(0)
    tm = x1_ref.shape[0]

    def pad_rows(e, act):
        def body(r, c):
            cp = pltpu.make_async_copy(zrow, xs_hbm.at[pl.ds(r, 1)], zsem)
            cp.start() if act == "start" else cp.wait()
            return c
        lax.fori_loop(einfo_ref[e], einfo_ref[LANES + e], body, 0)

    def tail_blocks(act):
        blk = zblk.shape[0]

        def body(b, c):
            cp = pltpu.make_async_copy(zblk, xs_hbm.at[pl.ds(b * blk, blk)], zsem)
            cp.start() if act == "start" else cp.wait()
            return c
        lax.fori_loop(einfo_ref[LANES + N_EXPERTS - 1] // blk, xs_hbm.shape[0] // blk, body, 0)

    @pl.when(i == 0)
    def _():
        zrow[...] = jnp.zeros(zrow.shape, f32)
        zblk[...] = jnp.zeros(zblk.shape, f32)
        for e in range(N_EXPERTS):
            pad_rows(e, "start")
        tail_blocks("start")
        for e in range(N_EXPERTS):
            pad_rows(e, "wait")
        tail_blocks("wait")

    def row_copy(r, j, slot):
        d = dest_ref[(i * tm + r) * TOP_K + j]
        return pltpu.make_async_copy(hbuf.at[slot, pl.ds(r, 1)], xs_hbm.at[pl.ds(d, 1)], sem.at[slot])

    def wait_slot(slot):
        for _ in range(TOP_K):
            pltpu.make_async_copy(hbuf.at[slot], hbuf.at[slot], sem.at[slot]).wait()

    def tile(slot):
        pl.when(i >= 2)(functools.partial(wait_slot, slot))
        hbuf[slot] = _rms(x1_ref[...], g_ref[...])
        for r in range(tm):
            for j in range(TOP_K):
                row_copy(r, j, slot).start()
        pl.when(i == n - 1)(functools.partial(wait_slot, slot))
        pl.when((i == n - 1) & (i >= 1))(functools.partial(wait_slot, 1 - slot))

    for slot in range(2):
        pl.when(i % 2 == slot)(functools.partial(tile, slot))


def dispatch(dest_flat, einfo, x1, g_ffn, n_rows):
    n, d = x1.shape
    tm = next(c for c in (OUT_TM, 64, 32, 16, SUBLANES) if n % c == 0)
    return pl.pallas_call(
        _dispatch_kernel,
        grid_spec=pltpu.PrefetchScalarGridSpec(
            num_scalar_prefetch=2, grid=(n // tm,),
            in_specs=[pl.BlockSpec((tm, d), lambda i, ds, ei: (i, 0)),
                      pl.BlockSpec(g_ffn.shape, lambda i, ds, ei: (0, 0))],
            out_specs=pl.BlockSpec(memory_space=pl.ANY),
            scratch_shapes=[pltpu.VMEM((2, tm, d), f32), pltpu.VMEM((1, d), f32), pltpu.VMEM((FFN_BLK, d), f32),
                            pltpu.SemaphoreType.DMA((2,)), pltpu.SemaphoreType.DMA(())]),
        out_shape=jax.ShapeDtypeStruct((n_rows, d), f32),
        compiler_params=_cparams(("arbitrary",)),
        name="dispatch",
    )(dest_flat, einfo.reshape(-1), x1, g_ffn)


def _ffn_rows_kernel(be_ref, nact_ref, xs_ref, *rest):
    wgu_refs = rest[:W_SPLIT]
    bgu_ref = rest[W_SPLIT]
    wdn_refs = rest[W_SPLIT + 1:2 * W_SPLIT + 1]
    bdn_ref, y_ref, wgu_bf, wdn_bf = rest[2 * W_SPLIT + 1:]
    i = pl.program_id(0)
    n_act = nact_ref[0]

    @pl.when(i >= n_act)
    def _():
        y_ref[...] = jnp.zeros(y_ref.shape, f32)

    @pl.when((i < n_act) & ((i == 0) | (be_ref[i] != be_ref[jnp.maximum(i - 1, 0)])))
    def _():
        wg = wgu_bf.shape[1] // W_SPLIT
        wd = wdn_bf.shape[1] // W_SPLIT
        for c in range(W_SPLIT):
            wgu_bf[:, c * wg:(c + 1) * wg] = wgu_refs[c][0].astype(bf16)
            wdn_bf[:, c * wd:(c + 1) * wd] = wdn_refs[c][0].astype(bf16)

    @pl.when(i < n_act)
    def _():
        x = xs_ref[...].astype(bf16)
        gu = jnp.dot(x, wgu_bf[...], preferred_element_type=f32) + bgu_ref[0]
        gate = jnp.minimum(gu[:, :EXPERT_FF], SWIGLU_LIMIT)
        up = jnp.clip(gu[:, EXPERT_FF:], -SWIGLU_LIMIT, SWIGLU_LIMIT)
        act = (up + 1.0) * (gate * _sigmoid(SWIGLU_ALPHA * gate))
        y_ref[...] = jnp.dot(act.astype(bf16), wdn_bf[...], preferred_element_type=f32) + bdn_ref[0]


def ffn_rows(blk_e, n_act, xs, w_gu, b_gu, w_dn, b_dn, blk, n_blocks):
    d = xs.shape[1]
    ff2 = w_gu.shape[2]
    wspec = lambda shape, col: pl.BlockSpec(shape, functools.partial(lambda i, be, na, c: (be[i], 0, c), c=col))
    return pl.pallas_call(
        _ffn_rows_kernel,
        grid_spec=pltpu.PrefetchScalarGridSpec(
            num_scalar_prefetch=2, grid=(n_blocks,),
            in_specs=[pl.BlockSpec((blk, d), lambda i, be, na: (jnp.minimum(i, jnp.maximum(na[0] - 1, 0)), 0))]
            + [wspec((1, d, ff2 // W_SPLIT), c) for c in range(W_SPLIT)]
            + [pl.BlockSpec((1, 1, ff2), lambda i, be, na: (be[i], 0, 0))]
            + [wspec((1, ff2 // 2, d // W_SPLIT), c) for c in range(W_SPLIT)]
            + [pl.BlockSpec((1, 1, d), lambda i, be, na: (be[i], 0, 0))],
            out_specs=pl.BlockSpec((blk, d), lambda i, be, na: (i, 0)),
            scratch_shapes=[pltpu.VMEM((d, ff2), bf16), pltpu.VMEM((ff2 // 2, d), bf16)]),
        out_shape=jax.ShapeDtypeStruct((n_blocks * blk, d), f32),
        compiler_params=_cparams(("arbitrary",)),
        name="ffn",
    )(blk_e, n_act, xs, *([w_gu] * W_SPLIT), b_gu.reshape(b_gu.shape[0], 1, ff2),
      *([w_dn] * W_SPLIT), b_dn.reshape(b_dn.shape[0], 1, d))


def _out_kernel(dest_ref, x1_ref, tg_ref, p_ref, ys_hbm, gp_ref, wg_ref, wp_ref, gfin_ref, y_ref, buf, sem):
    i = pl.program_id(0)
    n = pl.num_programs(0)
    tm = x1_ref.shape[0]

    def row_copy(t, r, j, slot):
        d = dest_ref[(t * tm + r) * TOP_K + j]
        return pltpu.make_async_copy(ys_hbm.at[pl.ds(d, 1)], buf.at[slot, j, pl.ds(r, 1)], sem.at[slot])

    def wait_slot(slot):
        pltpu.make_async_copy(buf.at[slot], buf.at[slot], sem.at[slot]).wait()

    @pl.when(i == 0)
    def _():
        def body(r, c):
            for j in range(TOP_K):
                row_copy(0, r, j, 0).start()
            return c
        lax.fori_loop(0, tm, body, 0)

    def tile(slot):
        wait_slot(slot)
        nxt = jnp.minimum(i + 1, n - 1)
        for r in range(tm):
            for j in range(TOP_K):
                row_copy(nxt, r, j, 1 - slot).start()
        tg = tg_ref[...]
        x2 = x1_ref[...]
        for j in range(TOP_K):
            x2 = x2 + tg[:, j:j + 1] * buf[slot, j]
        hn = _rms(x2, gp_ref[...]).astype(bf16)
        gate = _sigmoid(jnp.dot(hn, wg_ref[...], preferred_element_type=f32))
        x3 = x2 + gate * jnp.dot(p_ref[...].astype(bf16), wp_ref[...], preferred_element_type=f32)
        y_ref[...] = _rms(x3, gfin_ref[...])

    for slot in range(2):
        pl.when(i % 2 == slot)(functools.partial(tile, slot))

    @pl.when(i == n - 1)
    def _():
        wait_slot(1 - i % 2)


def out_stage(dest_flat, x1, tg, p, ys, g_ple, wg, wp, g_final):
    n, d = x1.shape
    tm = min(OUT_TM, n)
    assert n % tm == 0
    row = lambda w: pl.BlockSpec((tm, w), lambda i, ds: (i, 0))
    full = lambda a: pl.BlockSpec(a.shape, lambda i, ds: (0,) * a.ndim)
    return pl.pallas_call(
        _out_kernel,
        grid_spec=pltpu.PrefetchScalarGridSpec(
            num_scalar_prefetch=1, grid=(n // tm,),
            in_specs=[row(d), row(LANES), row(p.shape[1]), pl.BlockSpec(memory_space=pl.ANY),
                      full(g_ple), full(wg), full(wp), full(g_final)],
            out_specs=row(d),
            scratch_shapes=[pltpu.VMEM((2, TOP_K, tm, d), f32), pltpu.SemaphoreType.DMA((2,))]),
        out_shape=jax.ShapeDtypeStruct((n, d), f32),
        compiler_params=_cparams(("arbitrary",)),
        name="out",
    )(dest_flat, x1, tg, p, ys, g_ple, wg, wp, g_final)


def mid_stage(x, oa, ob, ga, gb, w):
    return mid(x, oa, ob, ga, gb, w["wa"], w["wb"], w["wo"], w["g_ffn"], w["wr_hi"], w["wr_lo"], w["br"])


def moe_out(parts, w):
    x1_all = jnp.concatenate([pt[0] for pt in parts], axis=0)
    te = jnp.concatenate([pt[1] for pt in parts], axis=0)
    n = x1_all.shape[0]
    blk = FFN_BLK
    n_blocks = -(-(n * TOP_K) // blk) + N_EXPERTS
    dest, meta, einfo = rank(te, blk, n_blocks)
    dest_flat = dest[:, :TOP_K].reshape(-1)
    blk_e = meta[:n_blocks, 0]
    n_act = meta[0:1, 1]
    xs = dispatch(dest_flat, einfo, x1_all, w["g_ffn"], n_blocks * blk)
    ys = ffn_rows(blk_e, n_act, xs, w["w_gu"], w["b_gu"], w["w_dn"], w["b_dn"], blk, n_blocks)
    outs, off = [], 0
    for x1, _, tg, p in parts:
        ni = x1.shape[0]
        outs.append(out_stage(dest_flat[off * TOP_K:(off + ni) * TOP_K], x1, tg, p, ys,
                              w["g_ple"], w["wg"], w["wp"], w["g_final"]))
        off += ni
    return outs


def tail(x, oa, ob, ga, gb, p, w):
    return moe_out([mid_stage(x, oa, ob, ga, gb, w) + (p,)], w)[0]


def _prep_weights(g_mix, w_in, b_gate, g_mnorm, w_up_a, w_up_b, w_out, g_ffn, w_router, b_router,
                  w_gu, b_gu, w_dn, b_dn, g_ple, w_ple_gate, w_ple_proj, g_final):
    wr = jnp.pad(w_router.astype(f32), ((0, 0), (0, LANES - N_EXPERTS)))
    wr_hi = wr.astype(bf16)
    wr_lo = (wr - wr_hi.astype(f32)).astype(bf16)
    br = jnp.full((1, LANES), -jnp.inf, f32).at[0, :N_EXPERTS].set(b_router.astype(f32))
    bias = jnp.zeros((1, LANES), f32).at[0, SM_IG:SM_IG + 2 * M_HEADS].set(b_gate.astype(f32))
    return dict(g_mix=g_mix.reshape(1, -1), w_in=_pack_w_in(w_in), bias=bias, g_mnorm=g_mnorm,
                wa=w_up_a.astype(bf16), wb=w_up_b.astype(bf16), wo=w_out.astype(bf16), g_ffn=g_ffn.reshape(1, -1),
                wr_hi=wr_hi, wr_lo=wr_lo, br=br, w_gu=w_gu, b_gu=b_gu, w_dn=w_dn, b_dn=b_dn,
                g_ple=g_ple.reshape(1, -1), wg=w_ple_gate.astype(bf16), wp=w_ple_proj.astype(bf16),
                g_final=g_final.reshape(1, -1))


def _layer(x, p, pos, attn_fn, c0, n0, m0, w):
    b, t = x.shape[:2]
    n = b * t
    cos, sin = _rope_tables(pos)
    cos = jnp.tile(cos, (b, 1))
    sin = jnp.tile(sin, (b, 1))
    x2d = x.reshape(n, D_MODEL)
    q, k, v, qi, sm, mq, mk, mv, og, ga, gb = in_proj(x2d, cos, sin, w["g_mix"], w["w_in"], w["bias"])
    r3 = lambda a: a.reshape(b, t, a.shape[-1])
    o_a = attn_fn(r3(q), r3(qi), r3(sm), r3(k), r3(v))
    o_b, c, nn, m = mlstm_branch(r3(mq), r3(mk), r3(mv), r3(og), r3(sm), c0, n0, m0, w["g_mnorm"])
    part = mid_stage(x2d, o_a.reshape(n, A_WIDTH), o_b.reshape(n, M_WIDTH), ga, gb, w) + (p.reshape(n, PLE_DIM),)
    state = (k.reshape(b, t, N_KV_HEADS, HEAD_DIM), v.reshape(b, t, N_KV_HEADS, HEAD_DIM),
             sm[:, :IDX_DIM].reshape(b, t, IDX_DIM), c, nn, m)
    return part, state


def kernel(x_prompt, x_sample, cache_k, cache_v, cache_idx_k, state_C, state_n, state_m, page_table,
           p_prompt, p_sample, g_mix, w_in, b_gate, g_mnorm, w_up_a, w_up_b, w_out, g_ffn,
           w_router, b_router, w_gu, b_gu, w_dn, b_dn, g_ple, w_ple_gate, w_ple_proj, g_final):
    assert x_prompt.shape[-1] == D_MODEL and w_in.shape[0] == 1, "single-layer model of width D_MODEL"
    bp, tp = x_prompt.shape[:2]
    ts = x_sample.shape[1]
    past = page_table.shape[1] * PAGE_SIZE
    w = _prep_weights(g_mix[0], w_in[0], b_gate[0], g_mnorm[0], w_up_a[0], w_up_b[0], w_out[0], g_ffn[0],
                      w_router[0], b_router[0], w_gu[0], b_gu[0], w_dn[0], b_dn[0], g_ple[0],
                      w_ple_gate[0], w_ple_proj[0], g_final)
    zeros = lambda *s: jnp.zeros(s, f32)
    part_p, sp = _layer(x_prompt, p_prompt[0], jnp.arange(tp, dtype=i32), dsa_prompt_branch,
                        zeros(bp, M_HEADS, M_V, M_QK), zeros(bp, M_HEADS, M_QK), zeros(bp, M_HEADS), w)
    attn_s = functools.partial(dsa_sample, cache_k=cache_k[0], cache_v=cache_v[0], cache_idx_k=cache_idx_k[0],
                               page_table=page_table)
    part_s, ss = _layer(x_sample, p_sample[0], past + jnp.arange(ts, dtype=i32), attn_s,
                        state_C[0], state_n[0], state_m[0], w)
    yp, ys = moe_out([part_p, part_s], w)
    return ((yp.reshape(x_prompt.shape), ys.reshape(x_sample.shape))
            + tuple(s[None] for s in sp) + tuple(s[None] for s in ss))
```

```python
import functools

import jax
import jax.numpy as jnp
from jax import lax
from jax.experimental import pallas as pl
from jax.experimental.pallas import tpu as pltpu

f32 = jnp.float32
bf16 = jnp.bfloat16
i32 = jnp.int32

D_MODEL = 1024
PAGE_SIZE = 128
N_HEADS = 8
N_KV_HEADS = 2
HEAD_DIM = 64
GROUP = N_HEADS // N_KV_HEADS
IDX_HEADS = 4
IDX_DIM = 64
TOPK_MAX = 256
ROPE_THETA = 10000.0
M_HEADS = 4
M_QK = 64
M_V = 128
N_EXPERTS = 32
TOP_K = 4
EXPERT_FF = D_MODEL
SWIGLU_LIMIT = 7.0
SWIGLU_ALPHA = 1.702
PLE_DIM = 256
EPS = 1e-6
A_WIDTH = N_HEADS * HEAD_DIM
M_WIDTH = M_HEADS * M_V

LANES = 128
SUBLANES = 8
VMEM_LIMIT = 56 * 1024 * 1024

M_CHUNK = 128
MLSTM_BATCH = 1
TQ = 256
KC = 1024
PAGE_SLOTS = 4
PAGES_PER_STEP = 16
FFN_BLK = 256
W_SPLIT = 1
OUT_TM = 128

INT_MIN = -2147483648
INT_MAX = 2147483647
NEG_BIG = -1e30
LOG2E = 1.4426950408889634
ROW_SLICE = 16

SM_KI = 0
SM_WI = IDX_DIM
SM_IG = SM_WI + IDX_HEADS
SM_LF = SM_IG + M_HEADS


def _cparams(sem):
    return pltpu.CompilerParams(dimension_semantics=sem, vmem_limit_bytes=VMEM_LIMIT)


def _rms(x, g):
    return x * lax.rsqrt(jnp.mean(x * x, axis=-1, keepdims=True) + EPS) * g


def _sigmoid(x):
    return 1.0 / (1.0 + jnp.exp(-x))


def _log_sigmoid(x):
    return jnp.minimum(x, 0.0) - jnp.log1p(jnp.exp(-jnp.abs(x)))


def _sortable(x):
    bits = lax.bitcast_convert_type(x, i32)
    key = bits ^ ((bits >> 31) & INT_MAX)
    return jnp.where(x == 0.0, 0, key)


_G_Q = (0, 512)
_G_K = (512, 640)
_G_V = (640, 768)
_G_QI = (768, 1024)
_G_SM = (1024, 1152)
_G_MQ = (1152, 1408)
_G_MK = (1408, 1664)
_G_MV = (1664, 2176)
_G_MO = (2176, 2688)
_G_GA = (2688, 3712)
_G_GB = (3712, 4736)
_W_COLS = 4736


def _pack_w_in(w_in):
    o = [0]
    for s in (A_WIDTH, 128, 128, 256, 64, 4, 256, 256, 512, 4, 4, 512, 1024, 1024):
        o.append(o[-1] + s)
    aq, ak, av, iq, ik, iw, mq, mk, mv, mi, mf, mo, ga, gb = [w_in[:, o[i]:o[i + 1]] for i in range(14)]
    pad = jnp.zeros((w_in.shape[0], LANES - (IDX_DIM + IDX_HEADS + 2 * M_HEADS)), w_in.dtype)
    small = jnp.concatenate([ik, iw, mi, mf, pad], axis=1)
    w = jnp.concatenate([aq, ak, av, iq, small, mq, mk, mv, mo, ga, gb], axis=1)
    return w.astype(bf16)


def _in_kernel(x_ref, cos_ref, sin_ref, g_ref, w_ref, bias_ref,
               q_ref, k_ref, v_ref, qi_ref, sm_ref, mq_ref, mk_ref, mv_ref, og_ref, ga_ref, gb_ref):
    x = x_ref[...]
    hb = _rms(x, g_ref[...]).astype(bf16)
    cos = cos_ref[...]
    sin = sin_ref[...]
    tm = x.shape[0]
    lane = lax.broadcasted_iota(i32, (tm, LANES), 1)
    first_half = (lane % HEAD_DIM) < (HEAD_DIM // 2)

    def rope(z):
        rot = jnp.where(first_half, pltpu.roll(z, LANES - HEAD_DIM // 2, 1), pltpu.roll(z, HEAD_DIM // 2, 1))
        return z * cos + rot * sin

    def proj(grp):
        return jnp.dot(hb, w_ref[:, grp[0]:grp[1]], preferred_element_type=f32)

    z = proj(_G_Q)
    for j in range(4):
        q_ref[:, j * LANES:(j + 1) * LANES] = rope(z[:, j * LANES:(j + 1) * LANES])
    k_ref[...] = rope(proj(_G_K))
    v_ref[...] = proj(_G_V)
    z = proj(_G_QI)
    for j in range(2):
        qi_ref[:, j * LANES:(j + 1) * LANES] = rope(z[:, j * LANES:(j + 1) * LANES])
    z = proj(_G_SM)
    zb = z + bias_ref[...]
    sm = jnp.where(lane < SM_WI, rope(z),
                   jnp.where(lane < SM_IG, z * (IDX_HEADS ** -0.5 * IDX_DIM ** -0.5),
                             jnp.where(lane < SM_LF, zb,
                                       jnp.where(lane < SM_LF + M_HEADS, _log_sigmoid(zb), 0.0))))
    sm_ref[...] = sm
    mq_ref[...] = proj(_G_MQ)
    mk_ref[...] = proj(_G_MK) * (M_QK ** -0.5)
    mv_ref[...] = proj(_G_MV)
    og_ref[...] = _sigmoid(proj(_G_MO))
    ga_ref[...] = _sigmoid(proj(_G_GA))
    gb_ref[...] = _sigmoid(proj(_G_GB))


def _rope_tables(pos):
    half = HEAD_DIM // 2
    inv = ROPE_THETA ** (-jnp.arange(half, dtype=f32) / half)
    ang = pos.astype(f32)[:, None] * inv[None, :]
    cos = jnp.cos(ang)
    sin = jnp.sin(ang)
    cos128 = jnp.tile(cos, (1, 4))
    sin128 = jnp.tile(jnp.concatenate([-sin, sin], axis=1), (1, 2))
    return cos128, sin128


def in_proj(x2d, cos128, sin128, g_mix, w_packed, bias128):
    n = x2d.shape[0]
    tm = min(256, n)
    assert n % tm == 0
    row = lambda w: pl.BlockSpec((tm, w), lambda i: (i, 0))
    full = lambda a: pl.BlockSpec(a.shape, lambda i: (0,) * a.ndim)
    widths = (512, 128, 128, 256, 128, 256, 256, 512, 512, 1024, 1024)
    return pl.pallas_call(
        _in_kernel,
        grid=(n // tm,),
        in_specs=[row(D_MODEL), row(LANES), row(LANES), full(g_mix), full(w_packed), full(bias128)],
        out_specs=[row(w) for w in widths],
        out_shape=[jax.ShapeDtypeStruct((n, w), f32) for w in widths],
        compiler_params=_cparams(("parallel",)),
        name="in_proj",
    )(x2d, cos128, sin128, g_mix, w_packed, bias128)


def _mlstm_kernel(mq_ref, mk_ref, mv_ref, og_ref, sm_ref, gr_ref, c0_ref, n0_ref, m0_ref, gn_ref,
                  ob_ref, c_ref, n_ref, m_ref):
    ci = pl.program_id(1)
    L = mq_ref.shape[1]

    @pl.when(ci == 0)
    def _():
        c_ref[...] = c0_ref[...]
        n_ref[...] = n0_ref[...]
        m_ref[...] = m0_ref[...]

    row = lax.broadcasted_iota(i32, (L, L), 0)
    col = lax.broadcasted_iota(i32, (L, L), 1)
    tril = row >= col
    for bi, hd in [(bi, hd) for bi in range(mq_ref.shape[0]) for hd in range(M_HEADS)]:
        sm = sm_ref[bi]
        gr = gr_ref[bi]
        q = mq_ref[bi, :, hd * M_QK:(hd + 1) * M_QK]
        k = mk_ref[bi, :, hd * M_QK:(hd + 1) * M_QK]
        v = mv_ref[bi, :, hd * M_V:(hd + 1) * M_V]
        ig_r = gr[hd:hd + 1, :]
        lf_r = gr[M_HEADS + hd:M_HEADS + hd + 1, :]
        ig_c = sm[:, SM_IG + hd:SM_IG + hd + 1]
        lf_c = sm[:, SM_LF + hd:SM_LF + hd + 1]
        C = c_ref[bi, hd]
        nrow = n_ref[bi, hd]
        m_prev = m_ref[bi, hd]
        b_c = jnp.sum(jnp.where(tril, lf_r, 0.0), axis=1, keepdims=True)
        b_r = jnp.sum(jnp.where(tril, 0.0, lf_c) + jnp.where(row == col, lf_c, 0.0), axis=0, keepdims=True)
        dmat = jnp.where(tril, b_c - b_r + ig_r, -jnp.inf)
        inter = b_c + m_prev
        m_t = jnp.maximum(inter, jnp.max(dmat, axis=1, keepdims=True))
        qb = q.astype(bf16)
        kb = k.astype(bf16)
        qk = lax.dot_general(qb, kb, (((1,), (1,)), ((), ())), preferred_element_type=f32)
        s = qk * jnp.exp(dmat - m_t)
        w_inter = jnp.exp(inter - m_t)
        qc = lax.dot_general(qb, C.astype(bf16), (((1,), (1,)), ((), ())), preferred_element_type=f32)
        num = jnp.dot(s.astype(bf16), v.astype(bf16), preferred_element_type=f32) + w_inter * qc
        den = jnp.sum(s, axis=1, keepdims=True) + w_inter * jnp.sum(q * nrow, axis=1, keepdims=True)
        h = num / jnp.maximum(jnp.abs(den), jnp.exp(-m_t))
        b_last = b_c[L - 1:L, :]
        g_c = b_last - b_c + ig_c
        m_new = jnp.maximum(b_last + m_prev, jnp.max(g_c, axis=0, keepdims=True))
        w_k = jnp.exp(g_c - m_new)
        decay = jnp.exp(b_last + m_prev - m_new)
        wv = (w_k * v).astype(bf16)
        c_ref[bi, hd] = decay * C + lax.dot_general(wv, kb, (((0,), (0,)), ((), ())), preferred_element_type=f32)
        n_ref[bi, hd] = decay * nrow + jnp.sum(w_k * k, axis=0, keepdims=True)
        m_ref[bi, hd] = m_new
        gn = gn_ref[:, hd * M_V:(hd + 1) * M_V]
        ob_ref[bi, :, hd * M_V:(hd + 1) * M_V] = og_ref[bi, :, hd * M_V:(hd + 1) * M_V] * _rms(h, gn)


def mlstm(mq, mk, mv, og, sm, grow, c0, n0, m0, g_mnorm):
    b, t = mq.shape[:2]
    L = M_CHUNK
    bb = MLSTM_BATCH if b % MLSTM_BATCH == 0 else 1
    assert t % L == 0
    tok = lambda w: pl.BlockSpec((bb, L, w), lambda bi, ci: (bi, ci, 0))
    st = lambda a: pl.BlockSpec((bb,) + a.shape[1:], lambda bi, ci: (bi,) + (0,) * (a.ndim - 1))
    return pl.pallas_call(
        _mlstm_kernel,
        grid=(b // bb, t // L),
        in_specs=[tok(256), tok(256), tok(512), tok(512), tok(LANES),
                  pl.BlockSpec((bb, 2 * M_HEADS, L), lambda bi, ci: (bi, 0, ci)),
                  st(c0), st(n0), st(m0), pl.BlockSpec(g_mnorm.shape, lambda bi, ci: (0, 0))],
        out_specs=[tok(512), st(c0), st(n0), st(m0)],
        out_shape=[jax.ShapeDtypeStruct((b, t, M_WIDTH), f32), jax.ShapeDtypeStruct(c0.shape, f32),
                   jax.ShapeDtypeStruct(n0.shape, f32), jax.ShapeDtypeStruct(m0.shape, f32)],
        compiler_params=_cparams(("parallel", "arbitrary")),
        name="mlstm",
    )(mq, mk, mv, og, sm, grow, c0, n0, m0, g_mnorm)


def mlstm_branch(mq, mk, mv, og, sm, c0, n0, m0, g_mnorm):
    b, t = mq.shape[:2]
    tp = -(-t // M_CHUNK) * M_CHUNK
    if tp != t:
        pad = lambda a: jnp.pad(a, ((0, 0), (0, tp - t), (0, 0)))
        mq, mk, mv, og = pad(mq), pad(mk), pad(mv), pad(og)
        sm_pad = jnp.zeros((b, tp - t, LANES), f32).at[:, :, SM_IG:SM_IG + M_HEADS].set(NEG_BIG)
        sm = jnp.concatenate([sm, sm_pad], axis=1)
    grow = sm[:, :, SM_IG:SM_IG + 2 * M_HEADS].transpose(0, 2, 1)
    ob, c, n, m = mlstm(mq, mk, mv, og, sm, grow, c0.astype(f32), n0.astype(f32).reshape(b, M_HEADS, 1, M_QK),
                        m0.astype(f32).reshape(b, M_HEADS, 1, 1), g_mnorm.reshape(1, M_WIDTH))
    return ob[:, :t], c, n.reshape(b, M_HEADS, M_QK), m.reshape(b, M_HEADS)


def _select_threshold(get_chunk, n_chunks, chunk_w, rows, n_sel, dynamic):
    def count(pred):
        def body(c, acc):
            hit = jnp.where(pred(get_chunk(c)), 1.0, 0.0)
            for j in range(chunk_w // LANES):
                acc = acc + hit[:, j * LANES:(j + 1) * LANES]
            return acc
        acc0 = jnp.zeros((rows, LANES), f32)
        if dynamic:
            acc = lax.fori_loop(0, n_chunks, body, acc0)
        else:
            acc = acc0
            for c in range(n_chunks):
                acc = body(c, acc)
        return jnp.sum(acc, axis=1, keepdims=True)

    kf = float(n_sel)

    total = jnp.zeros((rows, 1), f32) + (n_chunks * chunk_w).astype(f32) if dynamic else \
        jnp.full((rows, 1), float(n_chunks * chunk_w), f32)
    c_nonneg = count(lambda keys: keys >= 0)
    c_pos = count(lambda keys: keys >= 1)
    take0 = c_nonneg >= kf
    zero_tie = take0 & (c_pos < kf)

    def unsettled(st):
        it, _, cnt = st
        return (it < 32) & (jnp.max(jnp.where(zero_tie, 0.0, jnp.abs(cnt - kf))) > 0.0)

    def bit_step(st):
        it, thr_u, cnt = st
        cand_u = thr_u | lax.shift_left(jnp.int32(1), 31 - it)
        cand_s = cand_u ^ INT_MIN
        c = count(lambda keys: keys >= cand_s)
        take = c >= kf
        return it + 1, jnp.where(take, cand_u, thr_u), jnp.where(take, c, cnt)

    state0 = (jnp.int32(1), jnp.where(take0, INT_MIN, 0).astype(i32), jnp.where(take0, c_nonneg, total))
    _, thr_u, _ = lax.while_loop(unsettled, lambda st: bit_step(bit_step(st)), bit_step(state0))
    thr = jnp.maximum(thr_u ^ INT_MIN, INT_MIN + 1)
    need = kf - count(lambda keys: keys > thr)
    return thr, need


def _tie_prefix_matrix():
    r = lax.broadcasted_iota(i32, (LANES, LANES), 0)
    c = lax.broadcasted_iota(i32, (LANES, LANES), 1)
    return jnp.where(r <= c, 1.0, 0.0).astype(bf16)


def _selected_blocks(key_blocks, thr, need, seen, tri):
    rows = key_blocks[0].shape[0]
    eqs = [kb == thr for kb in key_blocks]
    stack = jnp.concatenate([jnp.where(eq, 1.0, 0.0) for eq in eqs], axis=0).astype(bf16)
    pre = jnp.dot(stack, tri, preferred_element_type=f32)
    sels = []
    for j, (kb, eq) in enumerate(zip(key_blocks, eqs)):
        pj = pre[j * rows:(j + 1) * rows]
        sels.append((kb > thr) | (eq & ((seen + pj) <= need)))
        seen = seen + pj[:, LANES - 1:LANES]
    return sels, seen


def _dsa_prompt_kernel(q_ref, qi_ref, sm_ref, kit_ref, kt_ref, vd_ref, o_ref,
                       keys_ref, bias_ref, lg_ref, p_ref, m_ref, acc_ref, *, n_sel):
    qb = pl.program_id(1)
    tq = q_ref.shape[1]
    kc = kit_ref.shape[3]
    n_chunks = (qb * tq + tq - 1) // kc + 1
    lane = lax.broadcasted_iota(i32, (tq, LANES), 1)
    lo = lane < HEAD_DIM
    t_col = qb * tq + lax.broadcasted_iota(i32, (tq, 1), 0)
    sm = sm_ref[0]

    qi = qi_ref[0]
    qi_h = []
    for h in range(IDX_HEADS):
        blk = qi[:, (h // 2) * LANES:(h // 2 + 1) * LANES]
        qi_h.append(jnp.where(lo if h % 2 == 0 else ~lo, blk, 0.0).astype(bf16))
    w_h = [sm[:, SM_WI + h:SM_WI + h + 1] for h in range(IDX_HEADS)]

    def score_chunk(c, carry):
        kt = kit_ref[0, c]
        sc = jnp.zeros((tq, kc), f32)
        for h in range(IDX_HEADS):
            s = jnp.dot(qi_h[h], kt, preferred_element_type=f32)
            sc = sc + w_h[h] * jnp.maximum(s, 0.0)
        idx = c * kc + lax.broadcasted_iota(i32, (tq, kc), 1)
        keys_ref[c] = jnp.where(idx <= t_col, _sortable(sc), INT_MIN)
        return carry

    lax.fori_loop(0, n_chunks, score_chunk, 0)

    thr, need = _select_threshold(lambda c: keys_ref[c], n_chunks, kc, tq, n_sel, True)
    tri = _tie_prefix_matrix()

    q = q_ref[0] * (HEAD_DIM ** -0.5 * LOG2E)
    q_g = []
    for g in range(N_KV_HEADS):
        parts = []
        for j in range(GROUP):
            h = g * GROUP + j
            blk = q[:, (h // 2) * LANES:(h // 2 + 1) * LANES]
            parts.append(jnp.where(lo if h % 2 == 0 else ~lo, blk, 0.0).astype(bf16))
        q_g.append(jnp.concatenate(parts, axis=0))
    m_ref[...] = jnp.full(m_ref.shape, NEG_BIG, f32)
    acc_ref[...] = jnp.zeros(acc_ref.shape, f32)
    rs = min(ROW_SLICE, tq)

    def attend_chunk(c, seen):
        blocks = [keys_ref[c, :, j * LANES:(j + 1) * LANES] for j in range(kc // LANES)]
        sels, seen = _selected_blocks(blocks, thr, need, seen, tri)
        for j, sel in enumerate(sels):
            bias_ref[:, j * LANES:(j + 1) * LANES] = jnp.where(sel, 0.0, NEG_BIG)
        for g in range(N_KV_HEADS):
            lg_ref[...] = jnp.dot(q_g[g], kt_ref[0, g, c], preferred_element_type=f32)
            for r0 in range(0, GROUP * tq, rs):
                x = lg_ref[r0:r0 + rs] + bias_ref[r0 % tq:r0 % tq + rs]
                m_old = m_ref[g, r0:r0 + rs]
                m_new = jnp.maximum(m_old, jnp.max(x, axis=-1, keepdims=True))
                p_ref[r0:r0 + rs] = jnp.exp2(x - m_new).astype(bf16)
                acc_ref[g, r0:r0 + rs] = jnp.exp2(m_old - m_new) * acc_ref[g, r0:r0 + rs]
                m_ref[g, r0:r0 + rs] = m_new
            acc_ref[g] += jnp.dot(p_ref[...], vd_ref[0, g, c], preferred_element_type=f32)
        return seen

    lax.fori_loop(0, n_chunks, attend_chunk, jnp.zeros((tq, 1), f32))

    for g in range(N_KV_HEADS):
        acc = acc_ref[g]
        out = acc / acc[:, HEAD_DIM:HEAD_DIM + 1]
        for jp in range(GROUP // 2):
            even = out[(2 * jp) * tq:(2 * jp + 1) * tq]
            odd = pltpu.roll(out[(2 * jp + 1) * tq:(2 * jp + 2) * tq], HEAD_DIM, 1)
            o_ref[0, :, (g * 2 + jp) * LANES:(g * 2 + jp + 1) * LANES] = jnp.where(lo, even, odd)


def dsa_prompt(q, qi, sm, kit2, kt2, vd):
    b, t = q.shape[:2]
    nc, kc = kit2.shape[1], kit2.shape[3]
    tq = min(TQ, t)
    n_sel = min(TOPK_MAX, t // 4)
    tok = lambda w: pl.BlockSpec((1, tq, w), lambda bi, qb: (bi, qb, 0))
    return pl.pallas_call(
        functools.partial(_dsa_prompt_kernel, n_sel=n_sel),
        grid=(b, t // tq),
        in_specs=[tok(A_WIDTH), tok(IDX_HEADS * IDX_DIM), tok(LANES),
                  pl.BlockSpec((1, nc, LANES, kc), lambda bi, qb: (bi, 0, 0, 0)),
                  pl.BlockSpec((1, N_KV_HEADS, nc, LANES, kc), lambda bi, qb: (bi, 0, 0, 0, 0)),
                  pl.BlockSpec((1, N_KV_HEADS, nc, kc, LANES), lambda bi, qb: (bi, 0, 0, 0, 0))],
        out_specs=tok(A_WIDTH),
        out_shape=jax.ShapeDtypeStruct((b, t, A_WIDTH), f32),
        scratch_shapes=[pltpu.VMEM((nc, tq, kc), i32),
                        pltpu.VMEM((tq, kc), f32),
                        pltpu.VMEM((GROUP * tq, kc), f32),
                        pltpu.VMEM((GROUP * tq, kc), bf16),
                        pltpu.VMEM((N_KV_HEADS, GROUP * tq, 1), f32),
                        pltpu.VMEM((N_KV_HEADS, GROUP * tq, LANES), f32)],
        compiler_params=_cparams(("parallel", "arbitrary")),
        name="dsa_prompt",
    )(q, qi, sm, kit2, kt2, vd)


def dsa_prompt_branch(q, qi, sm, k, v):
    b, t = q.shape[:2]
    kc = min(KC, t)
    nc = t // kc
    kit = sm[:, :, :IDX_DIM].astype(bf16).reshape(b, nc, kc, IDX_DIM).transpose(0, 1, 3, 2)
    kit2 = jnp.concatenate([kit, kit], axis=2)
    kt = k.astype(bf16).reshape(b, nc, kc, N_KV_HEADS, HEAD_DIM).transpose(0, 3, 1, 4, 2)
    kt2 = jnp.concatenate([kt, kt], axis=3)
    vb = v.astype(bf16).reshape(b, nc, kc, N_KV_HEADS, HEAD_DIM).transpose(0, 3, 1, 2, 4)
    vd = jnp.concatenate([vb, jnp.ones_like(vb)], axis=4)
    return dsa_prompt(q, qi, sm, kit2, kt2, vd)


def _idx_scores(qi, w, ktpage):
    s = jnp.dot(qi, ktpage.astype(bf16), preferred_element_type=f32)
    r = (w * jnp.maximum(s, 0.0)).reshape(IDX_HEADS, SUBLANES, s.shape[1])
    sc = r[0]
    for h in range(1, IDX_HEADS):
        sc = sc + r[h]
    return sc


def _page_fetcher(pt_ref, srcs, bufs, sem, g):
    def start(step, slot):
        for j in range(g):
            pg = pt_ref[step * g + j]
            for i, (src, buf) in enumerate(zip(srcs, bufs)):
                pltpu.make_async_copy(src.at[pg], buf.at[slot, j], sem.at[i, slot]).start()

    def wait(slot):
        for i, buf in enumerate(bufs):
            pltpu.make_async_copy(buf.at[slot], buf.at[slot], sem.at[i, slot]).wait()

    return start, wait


def _run_paged(t, n_tot, active, start, wait, nbuf, body):
    @pl.when(active & (t == 0))
    def _():
        for d in range(nbuf - 1):
            pl.when(d < n_tot)(functools.partial(start, d, d))

    def one(slot):
        wait(slot)
        pl.when(t + nbuf - 1 < n_tot)(functools.partial(start, t + nbuf - 1, (slot + nbuf - 1) % nbuf))
        body(slot)

    for slot in range(nbuf):
        pl.when(active & (t % nbuf == slot))(functools.partial(one, slot))


def _ds_score_kernel(pt_ref, qi_ref, w_ref, kin_ref, cki_hbm, keys_ref, kbuf, sem, *, g):
    bi = pl.program_id(0)
    s = pl.program_id(1)
    last = pl.num_programs(1) - 1
    t = bi * last + s
    n_tot = pl.num_programs(0) * last
    qi = qi_ref[0]
    w = w_ref[0]
    start, wait = _page_fetcher(pt_ref, [cki_hbm], [kbuf], sem, g)

    def pages(slot):
        kcat = jnp.concatenate([kbuf[slot, j].astype(bf16) for j in range(g)], axis=1)
        keys_ref[0] = _sortable(_idx_scores(qi, w, kcat))

    _run_paged(t, n_tot, s < last, start, wait, kbuf.shape[0], pages)

    @pl.when(s == last)
    def _():
        keys_ref[0] = jnp.full(keys_ref.shape[1:], INT_MIN, i32)
        sc = _idx_scores(qi, w, kin_ref[0])
        tok = lax.broadcasted_iota(i32, sc.shape, 0)
        j = lax.broadcasted_iota(i32, sc.shape, 1)
        keys_ref[0, :, 0:PAGE_SIZE] = jnp.where(j <= tok, _sortable(sc), INT_MIN)


def _ds_thr_kernel(keys_ref, thr_ref, need_ref, *, n_sel, chunk_w):
    rows, width = keys_ref.shape

    def get_chunk(c):
        return keys_ref[:, c * chunk_w:(c + 1) * chunk_w]

    thr, need = _select_threshold(get_chunk, width // chunk_w, chunk_w, rows, n_sel, False)
    thr_ref[...] = jnp.broadcast_to(thr, thr_ref.shape)
    need_ref[...] = jnp.broadcast_to(need, need_ref.shape)


def _ds_attn_kernel(pt_ref, q_ref, keys_ref, thr_ref, need_ref, knew_ref, vnew_ref, ck_hbm, cv_hbm,
                    o_ref, m_ref, l_ref, acc_ref, seen_ref, kbuf, vbuf, sem, *, g):
    bi = pl.program_id(0)
    s = pl.program_id(1)
    last = pl.num_programs(1) - 1
    t = bi * last + s
    n_tot = pl.num_programs(0) * last
    start, wait = _page_fetcher(pt_ref, [ck_hbm, cv_hbm], [kbuf, vbuf], sem, g)
    q = q_ref[0]
    thr = thr_ref[0][:, 0:1]
    need = need_ref[0][:, 0:1]
    tri = _tie_prefix_matrix()

    def process(keys, kps, vps):
        n = len(kps)
        seen = seen_ref[...]
        sels, seen = _selected_blocks([keys[:, j * PAGE_SIZE:(j + 1) * PAGE_SIZE] for j in range(n)],
                                      thr, need, seen, tri)
        bias = [jnp.where(sel, 0.0, NEG_BIG) for sel in sels]
        seen_ref[...] = seen
        bias = jnp.concatenate(bias, axis=1) if n > 1 else bias[0]
        kcat = jnp.concatenate([kp.astype(bf16) for kp in kps], axis=1) if n > 1 else kps[0].astype(bf16)
        vcat = jnp.concatenate([vp.astype(bf16) for vp in vps], axis=1) if n > 1 else vps[0].astype(bf16)
        lg = jnp.dot(q, kcat, preferred_element_type=f32)
        lg = lg.reshape(N_HEADS, SUBLANES, n * PAGE_SIZE) + bias[None]
        m_old = m_ref[...]
        m_new = jnp.maximum(m_old, jnp.max(lg, axis=-1, keepdims=True))
        p = jnp.exp(lg - m_new)
        alpha = jnp.exp(m_old - m_new)
        l_ref[...] = alpha * l_ref[...] + jnp.sum(p, axis=-1, keepdims=True)
        pb = p.reshape(N_HEADS * SUBLANES, n * PAGE_SIZE).astype(bf16)
        pv = lax.dot_general(pb, vcat, (((1,), (1,)), ((), ())), preferred_element_type=f32)
        acc_ref[...] = alpha * acc_ref[...] + pv.reshape(N_HEADS, SUBLANES, LANES)
        m_ref[...] = m_new

    @pl.when(s == 0)
    def _():
        m_ref[...] = jnp.full(m_ref.shape, NEG_BIG, f32)
        l_ref[...] = jnp.zeros(l_ref.shape, f32)
        acc_ref[...] = jnp.zeros(acc_ref.shape, f32)
        seen_ref[...] = jnp.zeros(seen_ref.shape, f32)

    def pages(slot):
        process(keys_ref[0], [kbuf[slot, j] for j in range(g)], [vbuf[slot, j] for j in range(g)])

    _run_paged(t, n_tot, s < last, start, wait, kbuf.shape[0], pages)

    @pl.when(s == last)
    def _():
        process(keys_ref[0, :, 0:PAGE_SIZE], [knew_ref[0]], [vnew_ref[0]])
        o_ref[0] = (acc_ref[...] / l_ref[...]).reshape(N_HEADS * SUBLANES, LANES)


def dsa_sample(q, qi, sm, k, v, cache_k, cache_v, cache_idx_k, page_table):
    b, t = q.shape[:2]
    assert t <= SUBLANES
    n_pages = page_table.shape[1]
    g = min(PAGES_PER_STEP, n_pages)
    assert n_pages % g == 0
    ns = n_pages // g
    past = n_pages * PAGE_SIZE
    n_sel = min(TOPK_MAX, (past + t) // 4)
    pt = page_table.reshape(-1).astype(i32)
    padt = lambda a: jnp.pad(a, ((0, 0), (0, SUBLANES - t)) + ((0, 0),) * (a.ndim - 2))

    qi_r = padt(qi.reshape(b, t, IDX_HEADS, IDX_DIM)).transpose(0, 2, 1, 3).reshape(b, IDX_HEADS * SUBLANES, IDX_DIM)
    w_r = padt(sm[:, :, SM_WI:SM_WI + IDX_HEADS]).transpose(0, 2, 1).reshape(b, IDX_HEADS * SUBLANES, 1)
    padk = lambda a: jnp.pad(a, ((0, 0), (0, PAGE_SIZE - t), (0, 0))).transpose(0, 2, 1)
    ki_new = padk(sm[:, :, :IDX_DIM])
    k_new = padk(k)
    v_new = padk(v)
    qh = padt(q.reshape(b, t, N_HEADS, HEAD_DIM)).transpose(0, 2, 1, 3) * (HEAD_DIM ** -0.5)
    grp = (jnp.arange(N_HEADS) // GROUP)[None, :, None, None]
    q_r = jnp.concatenate([jnp.where(grp == 0, qh, 0.0), jnp.where(grp == 1, qh, 0.0)], axis=-1)
    q_r = q_r.reshape(b, N_HEADS * SUBLANES, LANES).astype(bf16)
    n_pool = cache_k.shape[0]
    ck = cache_k.transpose(0, 2, 3, 1).reshape(n_pool, N_KV_HEADS * HEAD_DIM, PAGE_SIZE)
    cv = cache_v.transpose(0, 2, 3, 1).reshape(n_pool, N_KV_HEADS * HEAD_DIM, PAGE_SIZE)
    cki = cache_idx_k.transpose(0, 2, 1)

    hbm = pl.BlockSpec(memory_space=pl.ANY)
    per_b = lambda a: pl.BlockSpec((1,) + a.shape[1:], lambda bi, s, ptr: (bi,) + (0,) * (a.ndim - 1))
    blk_w = g * PAGE_SIZE
    keys_spec = pl.BlockSpec((1, SUBLANES, blk_w), lambda bi, s, ptr: (bi, 0, s))
    width = past + blk_w

    keys = pl.pallas_call(
        functools.partial(_ds_score_kernel, g=g),
        grid_spec=pltpu.PrefetchScalarGridSpec(
            num_scalar_prefetch=1, grid=(b, ns + 1),
            in_specs=[per_b(qi_r), per_b(w_r), per_b(ki_new), hbm],
            out_specs=keys_spec,
            scratch_shapes=[pltpu.VMEM((PAGE_SLOTS, g, IDX_DIM, PAGE_SIZE), f32),
                            pltpu.SemaphoreType.DMA((1, PAGE_SLOTS))]),
        out_shape=jax.ShapeDtypeStruct((b, SUBLANES, width), i32),
        compiler_params=_cparams(("arbitrary", "arbitrary")),
        name="dsa_sample_scores",
    )(pt, qi_r.astype(bf16), w_r, ki_new, cki)

    rows = b * SUBLANES
    thr, need = pl.pallas_call(
        functools.partial(_ds_thr_kernel, n_sel=n_sel, chunk_w=blk_w),
        out_shape=[jax.ShapeDtypeStruct((rows, LANES), i32), jax.ShapeDtypeStruct((rows, LANES), f32)],
        compiler_params=pltpu.CompilerParams(vmem_limit_bytes=VMEM_LIMIT),
        name="dsa_sample_threshold",
    )(keys.reshape(rows, width))
    thr = thr.reshape(b, SUBLANES, LANES)
    need = need.reshape(b, SUBLANES, LANES)

    out = pl.pallas_call(
        functools.partial(_ds_attn_kernel, g=g),
        grid_spec=pltpu.PrefetchScalarGridSpec(
            num_scalar_prefetch=1, grid=(b, ns + 1),
            in_specs=[per_b(q_r), keys_spec, per_b(thr), per_b(need), per_b(k_new), per_b(v_new), hbm, hbm],
            out_specs=per_b(q_r),
            scratch_shapes=[pltpu.VMEM((N_HEADS, SUBLANES, 1), f32), pltpu.VMEM((N_HEADS, SUBLANES, 1), f32),
                            pltpu.VMEM((N_HEADS, SUBLANES, LANES), f32), pltpu.VMEM((SUBLANES, 1), f32),
                            pltpu.VMEM((PAGE_SLOTS, g, LANES, PAGE_SIZE), f32),
                            pltpu.VMEM((PAGE_SLOTS, g, LANES, PAGE_SIZE), f32),
                            pltpu.SemaphoreType.DMA((2, PAGE_SLOTS))]),
        out_shape=jax.ShapeDtypeStruct((b, N_HEADS * SUBLANES, LANES), f32),
        compiler_params=_cparams(("arbitrary", "arbitrary")),
        name="dsa_sample_attention",
    )(pt, q_r, keys, thr, need, k_new, v_new, ck, cv)
    out = out.reshape(b, N_HEADS, SUBLANES, N_KV_HEADS, HEAD_DIM)[:, :, :t]
    out = jnp.concatenate([out[:, :GROUP, :, 0], out[:, GROUP:, :, 1]], axis=1)
    return out.transpose(0, 2, 1, 3).reshape(b, t, A_WIDTH)


def _mid_kernel(x_ref, oa_ref, ob_ref, ga_ref, gb_ref, wa_ref, wb_ref, wo_ref, gf_ref, wrh_ref, wrl_ref, br_ref,
                x1_ref, te_ref, tg_ref):
    a = jnp.dot(oa_ref[...].astype(bf16), wa_ref[...], preferred_element_type=f32)
    b = jnp.dot(ob_ref[...].astype(bf16), wb_ref[...], preferred_element_type=f32)
    merged = ga_ref[...] * a + gb_ref[...] * b
    x1 = x_ref[...] + jnp.dot(merged.astype(bf16), wo_ref[...], preferred_element_type=f32)
    x1_ref[...] = x1
    h2 = _rms(x1, gf_ref[...])
    hi = h2.astype(bf16)
    lo = (h2 - hi.astype(f32)).astype(bf16)
    lg = (jnp.dot(hi, wrh_ref[...], preferred_element_type=f32) + jnp.dot(lo, wrh_ref[...], preferred_element_type=f32)
          + jnp.dot(hi, wrl_ref[...], preferred_element_type=f32)) + br_ref[...]
    lane = lax.broadcasted_iota(i32, lg.shape, 1)
    lane_f = lane.astype(f32)
    vals, ids = [], []
    for _ in range(TOP_K):
        m = jnp.max(lg, axis=1, keepdims=True)
        idx = jnp.min(jnp.where(lg == m, lane_f, float(LANES)), axis=1, keepdims=True).astype(i32)
        vals.append(m)
        ids.append(idx)
        lg = jnp.where(lane == idx, -jnp.inf, lg)
    ex = [jnp.exp(v - vals[0]) for v in vals]
    tot = ex[0] + ex[1] + ex[2] + ex[3]
    te = jnp.zeros(lg.shape, i32)
    tg = jnp.zeros(lg.shape, f32)
    for j in range(TOP_K):
        te = jnp.where(lane == j, ids[j], te)
        tg = jnp.where(lane == j, ex[j] / tot, tg)
    te_ref[...] = te
    tg_ref[...] = tg


def mid(x, oa, ob, ga, gb, wa, wb, wo, g_ffn, wr_hi, wr_lo, br):
    n = x.shape[0]
    tm = min(256, n)
    assert n % tm == 0
    row = lambda w: pl.BlockSpec((tm, w), lambda i: (i, 0))
    full = lambda a: pl.BlockSpec(a.shape, lambda i: (0,) * a.ndim)
    return pl.pallas_call(
        _mid_kernel,
        grid=(n // tm,),
        in_specs=[row(D_MODEL), row(A_WIDTH), row(M_WIDTH), row(D_MODEL), row(D_MODEL),
                  full(wa), full(wb), full(wo), full(g_ffn), full(wr_hi), full(wr_lo), full(br)],
        out_specs=[row(D_MODEL), row(LANES), row(LANES)],
        out_shape=[jax.ShapeDtypeStruct((n, D_MODEL), f32),
                   jax.ShapeDtypeStruct((n, LANES), i32), jax.ShapeDtypeStruct((n, LANES), f32)],
        compiler_params=_cparams(("parallel",)),
        name="mid",
    )(x, oa, ob, ga, gb, wa, wb, wo, g_ffn, wr_hi, wr_lo, br)


def _rank_kernel(te_ref, dest_ref, meta_ref, einfo_ref, cnt_ref, carry_ref, *, blk):
    ph = pl.program_id(0)
    i = pl.program_id(1)
    tm = te_ref.shape[0]
    lane = lax.broadcasted_iota(i32, (tm, LANES), 1)
    te = te_ref[...]
    oh = jnp.zeros((tm, LANES), f32)
    for j in range(TOP_K):
        oh = oh + jnp.where(lane == te[:, j:j + 1], 1.0, 0.0)
    tile_cnt = jnp.sum(oh, axis=0, keepdims=True)

    @pl.when((ph == 0) & (i == 0))
    def _():
        cnt_ref[...] = jnp.zeros(cnt_ref.shape, f32)

    @pl.when(ph == 0)
    def _():
        cnt_ref[...] += tile_cnt

    @pl.when((ph == 1) & (i == 0))
    def _():
        cnt = cnt_ref[...]
        padded = jnp.floor((cnt + (blk - 1)) / blk) * blk
        r = lax.broadcasted_iota(i32, (LANES, LANES), 0)
        c = lax.broadcasted_iota(i32, (LANES, LANES), 1)
        col = jnp.sum(jnp.where(r == c, padded, 0.0), axis=1, keepdims=True)
        start = jnp.sum(jnp.where(r < c, col, 0.0), axis=0, keepdims=True)
        carry_ref[...] = start
        pad_end = start + padded
        row8 = lax.broadcasted_iota(i32, einfo_ref.shape, 0)
        einfo_ref[...] = jnp.where(row8 == 0, start + cnt, jnp.where(row8 == 1, pad_end, 0.0)).astype(i32)
        nbp = meta_ref.shape[0]
        jb = (lax.broadcasted_iota(i32, (nbp, LANES), 0) * blk).astype(f32)
        lane2 = lax.broadcasted_iota(i32, (nbp, LANES), 1)
        be = jnp.sum(jnp.where((pad_end <= jb) & (lane2 < N_EXPERTS), 1.0, 0.0), axis=1, keepdims=True)
        be = jnp.minimum(be, float(N_EXPERTS - 1))
        n_act = jnp.sum(jnp.where(lane2 == N_EXPERTS - 1, pad_end, 0.0), axis=1, keepdims=True) / blk
        meta_ref[...] = jnp.where(lane2 == 0, be, jnp.where(lane2 == 1, n_act, 0.0)).astype(i32)

    @pl.when(ph == 1)
    def _():
        r = lax.broadcasted_iota(i32, (tm, tm), 0)
        c = lax.broadcasted_iota(i32, (tm, tm), 1)
        before = jnp.where(c < r, 1.0, 0.0).astype(bf16)
        pos = carry_ref[...] + jnp.dot(before, oh.astype(bf16), preferred_element_type=f32)
        d = jnp.zeros((tm, LANES), f32)
        for j in range(TOP_K):
            dj = jnp.sum(jnp.where(lane == te[:, j:j + 1], pos, 0.0), axis=1, keepdims=True)
            d = jnp.where(lane == j, dj, d)
        dest_ref[...] = d.astype(i32)
        carry_ref[...] += tile_cnt


def rank(te, blk, n_blocks):
    n = te.shape[0]
    tm = next(c for c in (512, 384, 256, 128, 64, 32, 16, SUBLANES) if n % c == 0)
    nbp = -(-n_blocks // SUBLANES) * SUBLANES
    return pl.pallas_call(
        functools.partial(_rank_kernel, blk=blk),
        grid=(2, n // tm),
        in_specs=[pl.BlockSpec((tm, LANES), lambda ph, i: (i, 0))],
        out_specs=[pl.BlockSpec((tm, LANES), lambda ph, i: (i * ph, 0)),
                   pl.BlockSpec((nbp, LANES), lambda ph, i: (0, 0)),
                   pl.BlockSpec((SUBLANES, LANES), lambda ph, i: (0, 0))],
        out_shape=[jax.ShapeDtypeStruct((n, LANES), i32), jax.ShapeDtypeStruct((nbp, LANES), i32),
                   jax.ShapeDtypeStruct((SUBLANES, LANES), i32)],
        scratch_shapes=[pltpu.VMEM((1, LANES), f32), pltpu.VMEM((1, LANES), f32)],
        compiler_params=_cparams(("arbitrary", "arbitrary")),
        name="rank",
    )(te)


def _ffn_kernel(be_ref, nact_ref, rt_ref, x_hbm, *rest):
    wgu_refs = rest[:W_SPLIT]
    bgu_ref = rest[W_SPLIT]
    wdn_refs = rest[W_SPLIT + 1:2 * W_SPLIT + 1]
    bdn_ref, y_ref, xbuf, sem, wgu_bf, wdn_bf = rest[2 * W_SPLIT + 1:]
    i = pl.program_id(0)
    n_act = nact_ref[0]
    blk = xbuf.shape[1]

    def row_copy(b, r, slot):
        tok = rt_ref[b * blk + r]
        return pltpu.make_async_copy(x_hbm.at[pl.ds(tok, 1)], xbuf.at[slot, pl.ds(r, 1)], sem.at[slot])

    def wait_slot(slot):
        pltpu.make_async_copy(xbuf.at[slot], xbuf.at[slot], sem.at[slot]).wait()

    @pl.when(i == 0)
    def _():
        def body(r, c):
            row_copy(0, r, 0).start()
            return c
        lax.fori_loop(0, blk, body, 0)

    @pl.when(i == n_act)
    def _():
        wait_slot(i % 2)

    @pl.when(i >= n_act)
    def _():
        y_ref[...] = jnp.zeros(y_ref.shape, f32)

    @pl.when((i < n_act) & ((i == 0) | (be_ref[i] != be_ref[jnp.maximum(i - 1, 0)])))
    def _():
        wg = wgu_bf.shape[1] // W_SPLIT
        wd = wdn_bf.shape[1] // W_SPLIT
        for c in range(W_SPLIT):
            wgu_bf[:, c * wg:(c + 1) * wg] = wgu_refs[c][0].astype(bf16)
            wdn_bf[:, c * wd:(c + 1) * wd] = wdn_refs[c][0].astype(bf16)

    def block(slot):
        wait_slot(slot)
        for r in range(blk):
            row_copy(i + 1, r, 1 - slot).start()
        x = xbuf[slot].astype(bf16)
        gu = jnp.dot(x, wgu_bf[...], preferred_element_type=f32) + bgu_ref[0]
        gate = jnp.minimum(gu[:, :EXPERT_FF], SWIGLU_LIMIT)
        up = jnp.clip(gu[:, EXPERT_FF:], -SWIGLU_LIMIT, SWIGLU_LIMIT)
        act = (up + 1.0) * (gate * _sigmoid(SWIGLU_ALPHA * gate))
        y_ref[...] = jnp.dot(act.astype(bf16), wdn_bf[...], preferred_element_type=f32) + bdn_ref[0]

    for slot in range(2):
        pl.when((i < n_act) & (i % 2 == slot))(functools.partial(block, slot))


def ffn(blk_e, n_act, row_tok, h2, w_gu, b_gu, w_dn, b_dn, blk, n_blocks):
    d = h2.shape[1]
    ff2 = w_gu.shape[2]
    return pl.pallas_call(
        _ffn_kernel,
        grid_spec=pltpu.PrefetchScalarGridSpec(
            num_scalar_prefetch=3, grid=(n_blocks,),
            in_specs=[pl.BlockSpec(memory_space=pl.ANY)]
            + [pl.BlockSpec((1, d, ff2 // W_SPLIT), functools.partial(lambda i, be, na, rt, c: (be[i], 0, c), c=c))
               for c in range(W_SPLIT)]
            + [pl.BlockSpec((1, 1, ff2), lambda i, be, na, rt: (be[i], 0, 0))]
            + [pl.BlockSpec((1, ff2 // 2, d // W_SPLIT), functools.partial(lambda i, be, na, rt, c: (be[i], 0, c), c=c))
               for c in range(W_SPLIT)]
            + [pl.BlockSpec((1, 1, d), lambda i, be, na, rt: (be[i], 0, 0))],
            out_specs=pl.BlockSpec((blk, d), lambda i, be, na, rt: (i, 0)),
            scratch_shapes=[pltpu.VMEM((2, blk, d), f32), pltpu.SemaphoreType.DMA((2,)),
                            pltpu.VMEM((d, ff2), bf16), pltpu.VMEM((ff2 // 2, d), bf16)]),
        out_shape=jax.ShapeDtypeStruct((n_blocks * blk, d), f32),
        compiler_params=_cparams(("arbitrary",)),
        name="ffn",
    )(blk_e, n_act, row_tok, h2, *([w_gu] * W_SPLIT), b_gu.reshape(b_gu.shape[0], 1, ff2),
      *([w_dn] * W_SPLIT), b_dn.reshape(b_dn.shape[0], 1, d))


def _dispatch_kernel(dest_ref, einfo_ref, x1_ref, g_ref, xs_hbm, hbuf, zrow, zblk, sem, zsem):
    i = pl.program_id(0)
    n = pl.num_programs(0)
    tm = x1_ref.shape[0]

    def pad_rows(e, act):
        def body(r, c):
            cp = pltpu.make_async_copy(zrow, xs_hbm.at[pl.ds(r, 1)], zsem)
            cp.start() if act == "start" else cp.wait()
            return c
        lax.fori_loop(einfo_ref[e], einfo_ref[LANES + e], body, 0)

    def tail_blocks(act):
        blk = zblk.shape[0]

        def body(b, c):
            cp = pltpu.make_async_copy(zblk, xs_hbm.at[pl.ds(b * blk, blk)], zsem)
            cp.start() if act == "start" else cp.wait()
            return c
        lax.fori_loop(einfo_ref[LANES + N_EXPERTS - 1] // blk, xs_hbm.shape[0] // blk, body, 0)

    @pl.when(i == 0)
    def _():
        zrow[...] = jnp.zeros(zrow.shape, f32)
        zblk[...] = jnp.zeros(zblk.shape, f32)
        for e in range(N_EXPERTS):
            pad_rows(e, "start")
        tail_blocks("start")
        for e in range(N_EXPERTS):
            pad_rows(e, "wait")
        tail_blocks("wait")

    def row_copy(r, j, slot):
        d = dest_ref[(i * tm + r) * TOP_K + j]
        return pltpu.make_async_copy(hbuf.at[slot, pl.ds(r, 1)], xs_hbm.at[pl.ds(d, 1)], sem.at[slot])

    def wait_slot(slot):
        for _ in range(TOP_K):
            pltpu.make_async_copy(hbuf.at[slot], hbuf.at[slot], sem.at[slot]).wait()

    def tile(slot):
        pl.when(i >= 2)(functools.partial(wait_slot, slot))
        hbuf[slot] = _rms(x1_ref[...], g_ref[...])
        for r in range(tm):
            for j in range(TOP_K):
                row_copy(r, j, slot).start(priority=j % 2)
        pl.when(i == n - 1)(functools.partial(wait_slot, slot))
        pl.when((i == n - 1) & (i >= 1))(functools.partial(wait_slot, 1 - slot))

    for slot in range(2):
        pl.when(i % 2 == slot)(functools.partial(tile, slot))


def dispatch(dest_flat, einfo, x1, g_ffn, n_rows):
    n, d = x1.shape
    tm = next(c for c in (OUT_TM, 64, 32, 16, SUBLANES) if n % c == 0)
    return pl.pallas_call(
        _dispatch_kernel,
        grid_spec=pltpu.PrefetchScalarGridSpec(
            num_scalar_prefetch=2, grid=(n // tm,),
            in_specs=[pl.BlockSpec((tm, d), lambda i, ds, ei: (i, 0)),
                      pl.BlockSpec(g_ffn.shape, lambda i, ds, ei: (0, 0))],
            out_specs=pl.BlockSpec(memory_space=pl.ANY),
            scratch_shapes=[pltpu.VMEM((2, tm, d), f32), pltpu.VMEM((1, d), f32), pltpu.VMEM((FFN_BLK, d), f32),
                            pltpu.SemaphoreType.DMA((2,)), pltpu.SemaphoreType.DMA(())]),
        out_shape=jax.ShapeDtypeStruct((n_rows, d), f32),
        compiler_params=_cparams(("arbitrary",)),
        name="dispatch",
    )(dest_flat, einfo.reshape(-1), x1, g_ffn)


def _ffn_rows_kernel(be_ref, nact_ref, xs_ref, *rest):
    wgu_refs = rest[:W_SPLIT]
    bgu_ref = rest[W_SPLIT]
    wdn_refs = rest[W_SPLIT + 1:2 * W_SPLIT + 1]
    bdn_ref, y_ref, wgu_bf, wdn_bf = rest[2 * W_SPLIT + 1:]
    i = pl.program_id(0)
    n_act = nact_ref[0]

    @pl.when(i >= n_act)
    def _():
        y_ref[...] = jnp.zeros(y_ref.shape, f32)

    @pl.when((i < n_act) & ((i == 0) | (be_ref[i] != be_ref[jnp.maximum(i - 1, 0)])))
    def _():
        wg = wgu_bf.shape[1] // W_SPLIT
        wd = wdn_bf.shape[1] // W_SPLIT
        for c in range(W_SPLIT):
            wgu_bf[:, c * wg:(c + 1) * wg] = wgu_refs[c][0].astype(bf16)
            wdn_bf[:, c * wd:(c + 1) * wd] = wdn_refs[c][0].astype(bf16)

    @pl.when(i < n_act)
    def _():
        x = xs_ref[...].astype(bf16)
        gu = jnp.dot(x, wgu_bf[...], preferred_element_type=f32) + bgu_ref[0]
        gate = jnp.minimum(gu[:, :EXPERT_FF], SWIGLU_LIMIT)
        up = jnp.clip(gu[:, EXPERT_FF:], -SWIGLU_LIMIT, SWIGLU_LIMIT)
        act = (up + 1.0) * (gate * _sigmoid(SWIGLU_ALPHA * gate))
        y_ref[...] = jnp.dot(act.astype(bf16), wdn_bf[...], preferred_element_type=f32) + bdn_ref[0]


def ffn_rows(blk_e, n_act, xs, w_gu, b_gu, w_dn, b_dn, blk, n_blocks):
    d = xs.shape[1]
    ff2 = w_gu.shape[2]
    wspec = lambda shape, col: pl.BlockSpec(shape, functools.partial(lambda i, be, na, c: (be[i], 0, c), c=col))
    return pl.pallas_call(
        _ffn_rows_kernel,
        grid_spec=pltpu.PrefetchScalarGridSpec(
            num_scalar_prefetch=2, grid=(n_blocks,),
            in_specs=[pl.BlockSpec((blk, d), lambda i, be, na: (jnp.minimum(i, jnp.maximum(na[0] - 1, 0)), 0))]
            + [wspec((1, d, ff2 // W_SPLIT), c) for c in range(W_SPLIT)]
            + [pl.BlockSpec((1, 1, ff2), lambda i, be, na: (be[i], 0, 0))]
            + [wspec((1, ff2 // 2, d // W_SPLIT), c) for c in range(W_SPLIT)]
            + [pl.BlockSpec((1, 1, d), lambda i, be, na: (be[i], 0, 0))],
            out_specs=pl.BlockSpec((blk, d), lambda i, be, na: (i, 0)),
            scratch_shapes=[pltpu.VMEM((d, ff2), bf16), pltpu.VMEM((ff2 // 2, d), bf16)]),
        out_shape=jax.ShapeDtypeStruct((n_blocks * blk, d), f32),
        compiler_params=_cparams(("arbitrary",)),
        name="ffn",
    )(blk_e, n_act, xs, *([w_gu] * W_SPLIT), b_gu.reshape(b_gu.shape[0], 1, ff2),
      *([w_dn] * W_SPLIT), b_dn.reshape(b_dn.shape[0], 1, d))


def _out_kernel(dest_ref, x1_ref, tg_ref, p_ref, ys_hbm, gp_ref, wg_ref, wp_ref, gfin_ref, y_ref, buf, sem):
    i = pl.program_id(0)
    n = pl.num_programs(0)
    tm = x1_ref.shape[0]

    def row_copy(t, r, j, slot):
        d = dest_ref[(t * tm + r) * TOP_K + j]
        return pltpu.make_async_copy(ys_hbm.at[pl.ds(d, 1)], buf.at[slot, j, pl.ds(r, 1)], sem.at[slot])

    def wait_slot(slot):
        pltpu.make_async_copy(buf.at[slot], buf.at[slot], sem.at[slot]).wait()

    @pl.when(i == 0)
    def _():
        def body(r, c):
            for j in range(TOP_K):
                row_copy(0, r, j, 0).start()
            return c
        lax.fori_loop(0, tm, body, 0)

    def tile(slot):
        wait_slot(slot)
        nxt = jnp.minimum(i + 1, n - 1)
        for r in range(tm):
            for j in range(TOP_K):
                row_copy(nxt, r, j, 1 - slot).start(priority=j % 2)
        tg = tg_ref[...]
        x2 = x1_ref[...]
        for j in range(TOP_K):
            x2 = x2 + tg[:, j:j + 1] * buf[slot, j]
        hn = _rms(x2, gp_ref[...]).astype(bf16)
        gate = _sigmoid(jnp.dot(hn, wg_ref[...], preferred_element_type=f32))
        x3 = x2 + gate * jnp.dot(p_ref[...].astype(bf16), wp_ref[...], preferred_element_type=f32)
        y_ref[...] = _rms(x3, gfin_ref[...])

    for slot in range(2):
        pl.when(i % 2 == slot)(functools.partial(tile, slot))

    @pl.when(i == n - 1)
    def _():
        wait_slot(1 - i % 2)


def out_stage(dest_flat, x1, tg, p, ys, g_ple, wg, wp, g_final):
    n, d = x1.shape
    tm = min(OUT_TM, n)
    assert n % tm == 0
    row = lambda w: pl.BlockSpec((tm, w), lambda i, ds: (i, 0))
    full = lambda a: pl.BlockSpec(a.shape, lambda i, ds: (0,) * a.ndim)
    return pl.pallas_call(
        _out_kernel,
        grid_spec=pltpu.PrefetchScalarGridSpec(
            num_scalar_prefetch=1, grid=(n // tm,),
            in_specs=[row(d), row(LANES), row(p.shape[1]), pl.BlockSpec(memory_space=pl.ANY),
                      full(g_ple), full(wg), full(wp), full(g_final)],
            out_specs=row(d),
            scratch_shapes=[pltpu.VMEM((2, TOP_K, tm, d), f32), pltpu.SemaphoreType.DMA((2,))]),
        out_shape=jax.ShapeDtypeStruct((n, d), f32),
        compiler_params=_cparams(("arbitrary",)),
        name="out",
    )(dest_flat, x1, tg, p, ys, g_ple, wg, wp, g_final)


def mid_stage(x, oa, ob, ga, gb, w):
    return mid(x, oa, ob, ga, gb, w["wa"], w["wb"], w["wo"], w["g_ffn"], w["wr_hi"], w["wr_lo"], w["br"])


def moe_out(parts, w):
    x1_all = jnp.concatenate([pt[0] for pt in parts], axis=0)
    te = jnp.concatenate([pt[1] for pt in parts], axis=0)
    n = x1_all.shape[0]
    blk = FFN_BLK
    n_blocks = -(-(n * TOP_K) // blk) + N_EXPERTS
    dest, meta, einfo = rank(te, blk, n_blocks)
    dest_flat = dest[:, :TOP_K].reshape(-1)
    blk_e = meta[:n_blocks, 0]
    n_act = meta[0:1, 1]
    xs = dispatch(dest_flat, einfo, x1_all, w["g_ffn"], n_blocks * blk)
    ys = ffn_rows(blk_e, n_act, xs, w["w_gu"], w["b_gu"], w["w_dn"], w["b_dn"], blk, n_blocks)
    outs, off = [], 0
    for x1, _, tg, p in parts:
        ni = x1.shape[0]
        outs.append(out_stage(dest_flat[off * TOP_K:(off + ni) * TOP_K], x1, tg, p, ys,
                              w["g_ple"], w["wg"], w["wp"], w["g_final"]))
        off += ni
    return outs


def tail(x, oa, ob, ga, gb, p, w):
    return moe_out([mid_stage(x, oa, ob, ga, gb, w) + (p,)], w)[0]


def _prep_weights(g_mix, w_in, b_gate, g_mnorm, w_up_a, w_up_b, w_out, g_ffn, w_router, b_router,
                  w_gu, b_gu, w_dn, b_dn, g_ple, w_ple_gate, w_ple_proj, g_final):
    wr = jnp.pad(w_router.astype(f32), ((0, 0), (0, LANES - N_EXPERTS)))
    wr_hi = wr.astype(bf16)
    wr_lo = (wr - wr_hi.astype(f32)).astype(bf16)
    br = jnp.full((1, LANES), -jnp.inf, f32).at[0, :N_EXPERTS].set(b_router.astype(f32))
    bias = jnp.zeros((1, LANES), f32).at[0, SM_IG:SM_IG + 2 * M_HEADS].set(b_gate.astype(f32))
    return dict(g_mix=g_mix.reshape(1, -1), w_in=_pack_w_in(w_in), bias=bias, g_mnorm=g_mnorm,
                wa=w_up_a.astype(bf16), wb=w_up_b.astype(bf16), wo=w_out.astype(bf16), g_ffn=g_ffn.reshape(1, -1),
                wr_hi=wr_hi, wr_lo=wr_lo, br=br, w_gu=w_gu, b_gu=b_gu, w_dn=w_dn, b_dn=b_dn,
                g_ple=g_ple.reshape(1, -1), wg=w_ple_gate.astype(bf16), wp=w_ple_proj.astype(bf16),
                g_final=g_final.reshape(1, -1))


def _layer(x, p, pos, attn_fn, c0, n0, m0, w):
    b, t = x.shape[:2]
    n = b * t
    cos, sin = _rope_tables(pos)
    cos = jnp.tile(cos, (b, 1))
    sin = jnp.tile(sin, (b, 1))
    x2d = x.reshape(n, D_MODEL)
    q, k, v, qi, sm, mq, mk, mv, og, ga, gb = in_proj(x2d, cos, sin, w["g_mix"], w["w_in"], w["bias"])
    r3 = lambda a: a.reshape(b, t, a.shape[-1])
    o_a = attn_fn(r3(q), r3(qi), r3(sm), r3(k), r3(v))
    o_b, c, nn, m = mlstm_branch(r3(mq), r3(mk), r3(mv), r3(og), r3(sm), c0, n0, m0, w["g_mnorm"])
    part = mid_stage(x2d, o_a.reshape(n, A_WIDTH), o_b.reshape(n, M_WIDTH), ga, gb, w) + (p.reshape(n, PLE_DIM),)
    state = (k.reshape(b, t, N_KV_HEADS, HEAD_DIM), v.reshape(b, t, N_KV_HEADS, HEAD_DIM),
             sm[:, :IDX_DIM].reshape(b, t, IDX_DIM), c, nn, m)
    return part, state


def kernel(x_prompt, x_sample, cache_k, cache_v, cache_idx_k, state_C, state_n, state_m, page_table,
           p_prompt, p_sample, g_mix, w_in, b_gate, g_mnorm, w_up_a, w_up_b, w_out, g_ffn,
           w_router, b_router, w_gu, b_gu, w_dn, b_dn, g_ple, w_ple_gate, w_ple_proj, g_final):
    assert x_prompt.shape[-1] == D_MODEL and w_in.shape[0] == 1, "single-layer model of width D_MODEL"
    bp, tp = x_prompt.shape[:2]
    ts = x_sample.shape[1]
    past = page_table.shape[1] * PAGE_SIZE
    w = _prep_weights(g_mix[0], w_in[0], b_gate[0], g_mnorm[0], w_up_a[0], w_up_b[0], w_out[0], g_ffn[0],
                      w_router[0], b_router[0], w_gu[0], b_gu[0], w_dn[0], b_dn[0], g_ple[0],
                      w_ple_gate[0], w_ple_proj[0], g_final)
    zeros = lambda *s: jnp.zeros(s, f32)
    part_p, sp = _layer(x_prompt, p_prompt[0], jnp.arange(tp, dtype=i32), dsa_prompt_branch,
                        zeros(bp, M_HEADS, M_V, M_QK), zeros(bp, M_HEADS, M_QK), zeros(bp, M_HEADS), w)
    attn_s = functools.partial(dsa_sample, cache_k=cache_k[0], cache_v=cache_v[0], cache_idx_k=cache_idx_k[0],
                               page_table=page_table)
    part_s, ss = _layer(x_sample, p_sample[0], past + jnp.arange(ts, dtype=i32), attn_s,
                        state_C[0], state_n[0], state_m[0], w)
    yp, ys = moe_out([part_p, part_s], w)
    return ((yp.reshape(x_prompt.shape), ys.reshape(x_sample.shape))
            + tuple(s[None] for s in sp) + tuple(s[None] for s in ss))
```
